```python
import jax, jax.numpy as jnp
from jax import lax
import numpy as np

D_MODEL = 4096
BATCH = 4
SEQ = 2048
DEPTH = 1

EPS = 1e-6
ROPE_THETA = 10000.0
NEG = -1e30
MLA_HEADS = 16
MLA_Q_RANK = 1024
MLA_KV_RANK = 512
MLA_NOPE_DIM = 128
MLA_ROPE_DIM = 64
MLA_V_DIM = 128
MLA_QK_DIM = MLA_NOPE_DIM + MLA_ROPE_DIM
Q_BLOCK = 128
SWA_HEADS = 32
SWA_KV_HEADS = 4
SWA_GROUP = SWA_HEADS // SWA_KV_HEADS
SWA_HEAD_DIM = 64
SWA_WINDOW = 128
MLA_OUT = MLA_HEADS * MLA_V_DIM
SWA_OUT = SWA_HEADS * SWA_HEAD_DIM
MIX_WIDTH = MLA_OUT + SWA_OUT
OFF_Q = MLA_Q_RANK
OFF_KV = OFF_Q + MLA_KV_RANK
OFF_KPE = OFF_KV + MLA_ROPE_DIM
OFF_SQ = OFF_KPE + SWA_HEADS * SWA_HEAD_DIM
OFF_SK = OFF_SQ + SWA_KV_HEADS * SWA_HEAD_DIM
IN_WIDTH = OFF_SK + SWA_KV_HEADS * SWA_HEAD_DIM
N_GROUPS = 8
EXPERTS_PER_GROUP = 8
N_EXPERTS = N_GROUPS * EXPERTS_PER_GROUP
TOP_K = 2
D_EXPERT = 1024
MOE_BLOCK = 128

kernel_name = "hymba_mla_swa_hier_moe_layer"


def rms_norm(x, g):
    xf = x.astype(jnp.float32)
    y = xf * lax.rsqrt(jnp.mean(xf * xf, axis=-1, keepdims=True) + EPS)
    return (y * g.astype(jnp.float32)).astype(x.dtype)


def rope(x, positions):
    d = x.shape[-1]
    inv_freq = ROPE_THETA ** (-jnp.arange(0, d, 2, dtype=jnp.float32) / d)
    ang = positions.astype(jnp.float32)[..., None] * inv_freq
    cos = jnp.cos(ang)[:, :, None, :]
    sin = jnp.sin(ang)[:, :, None, :]
    xf = x.astype(jnp.float32)
    x1, x2 = xf[..., : d // 2], xf[..., d // 2:]
    return jnp.concatenate([x1 * cos - x2 * sin, x2 * cos + x1 * sin], axis=-1).astype(x.dtype)


def mla_attention(c_q, c_kv, k_pe, positions, g_q, g_kv, w_q_up, w_kv_up):
    B, S, _ = c_q.shape
    q = (rms_norm(c_q, g_q) @ w_q_up).reshape(B, S, MLA_HEADS, MLA_QK_DIM)
    q = jnp.concatenate([q[..., :MLA_NOPE_DIM], rope(q[..., MLA_NOPE_DIM:], positions)], axis=-1)
    kv = (rms_norm(c_kv, g_kv) @ w_kv_up).reshape(B, S, MLA_HEADS, MLA_NOPE_DIM + MLA_V_DIM)
    k_nope, v = kv[..., :MLA_NOPE_DIM], kv[..., MLA_NOPE_DIM:]
    k_rot = rope(k_pe[:, :, None, :], positions)
    k = jnp.concatenate([k_nope, jnp.broadcast_to(k_rot, (B, S, MLA_HEADS, MLA_ROPE_DIM))], axis=-1)
    n_blocks = S // Q_BLOCK
    qb = q.reshape(B, n_blocks, Q_BLOCK, MLA_HEADS, MLA_QK_DIM).transpose(1, 0, 2, 3, 4)
    key_pos = jnp.arange(S)
    scale = MLA_QK_DIM ** -0.5

    def q_block(args):
        q_blk, i = args
        s = jnp.einsum('bqhd,bkhd->bhqk', q_blk, k).astype(jnp.float32) * scale
        q_pos = i * Q_BLOCK + jnp.arange(Q_BLOCK)
        mask = key_pos[None, :] <= q_pos[:, None]
        p = jax.nn.softmax(jnp.where(mask, s, NEG), axis=-1).astype(v.dtype)
        return jnp.einsum('bhqk,bkhd->bqhd', p, v)

    o = lax.map(q_block, (qb, jnp.arange(n_blocks)))
    return o.transpose(1, 0, 2, 3, 4).reshape(B, S, MLA_OUT)


def swa_attention(q, k, v, positions, sinks):
    B, S, _ = q.shape
    W = SWA_WINDOW
    nb = S // W
    q = rope(q.reshape(B, S, SWA_HEADS, SWA_HEAD_DIM), positions)
    k = rope(k.reshape(B, S, SWA_KV_HEADS, SWA_HEAD_DIM), positions)
    v = v.reshape(B, S, SWA_KV_HEADS, SWA_HEAD_DIM)
    qb = q.reshape(B, nb, W, SWA_KV_HEADS, SWA_GROUP, SWA_HEAD_DIM)

    def band(t):
        tb = t.reshape(B, nb, W, SWA_KV_HEADS, SWA_HEAD_DIM)
        prev = jnp.concatenate([jnp.zeros_like(tb[:, :1]), tb[:, :-1]], axis=1)
        return jnp.concatenate([prev, tb], axis=2)

    kb, vb = band(k), band(v)
    s = jnp.einsum('bnqhgd,bnkhd->bnhgqk', qb, kb).astype(jnp.float32) * (SWA_HEAD_DIM ** -0.5)
    diff = (W + jnp.arange(W))[:, None] - jnp.arange(2 * W)[None, :]
    in_window = (diff >= 0) & (diff < SWA_WINDOW)
    key_abs = (jnp.arange(nb) * W - W)[:, None] + jnp.arange(2 * W)[None, :]
    mask = in_window[None] & (key_abs >= 0)[:, None, :]
    s = jnp.where(mask[None, :, None, None], s, NEG)
    sink = sinks.astype(jnp.float32).reshape(SWA_KV_HEADS, SWA_GROUP)[None, None, :, :, None, None]
    sink = jnp.broadcast_to(sink, s.shape[:-1] + (1,))
    p = jax.nn.softmax(jnp.concatenate([s, sink], axis=-1), axis=-1)[..., :-1].astype(v.dtype)
    o = jnp.einsum('bnhgqk,bnkhd->bnqhgd', p, vb)
    return o.reshape(B, S, SWA_OUT)


def hier_moe(x, w_rg, b_rg, w_re, b_re, w_gate, w_up, w_down):
    B, S, D = x.shape
    T = B * S
    xt = x.reshape(T, D)
    xf = xt.astype(jnp.float32)
    group_prob = jax.nn.softmax(xf @ w_rg.astype(jnp.float32) + b_rg.astype(jnp.float32), axis=-1)
    g_sel = jnp.argmax(group_prob, axis=-1)
    g_p = jnp.take_along_axis(group_prob, g_sel[:, None], axis=-1)
    e_logits = (xf @ w_re.astype(jnp.float32) + b_re.astype(jnp.float32)).reshape(T, N_GROUPS, EXPERTS_PER_GROUP)
    in_group = jnp.take_along_axis(e_logits, g_sel[:, None, None], axis=1)[:, 0]
    top_p, top_i = lax.top_k(jax.nn.softmax(in_group, axis=-1), TOP_K)
    gates = g_p * top_p / jnp.sum(top_p, axis=-1, keepdims=True)
    expert_idx = (g_sel[:, None] * EXPERTS_PER_GROUP + top_i).astype(jnp.int32)

    TK = T * TOP_K
    flat_e = expert_idx.reshape(TK)
    flat_tok = (jnp.arange(TK, dtype=jnp.int32) // TOP_K)
    flat_gate = gates.reshape(TK)
    order = jnp.argsort(flat_e)
    sorted_e = flat_e[order]
    counts = jnp.bincount(flat_e, length=N_EXPERTS)
    padded = (counts + MOE_BLOCK - 1) // MOE_BLOCK * MOE_BLOCK
    pad_end = jnp.cumsum(padded)
    pad_start = pad_end - padded
    start = jnp.cumsum(counts) - counts
    dest = pad_start[sorted_e] + (jnp.arange(TK) - start[sorted_e])
    n_blocks = -(-TK // MOE_BLOCK) + N_EXPERTS
    P = n_blocks * MOE_BLOCK
    row_tok = jnp.full((P,), T, jnp.int32).at[dest].set(flat_tok[order])
    row_gate = jnp.zeros((P,), jnp.float32).at[dest].set(flat_gate[order])
    block_e = jnp.minimum(jnp.searchsorted(pad_end, jnp.arange(n_blocks) * MOE_BLOCK, side='right'), N_EXPERTS - 1)
    x_pad = jnp.concatenate([xt, jnp.zeros((1, D), xt.dtype)], axis=0)
    xs = x_pad[row_tok].reshape(n_blocks, MOE_BLOCK, D)

    def expert_block(args):
        xb, e = args
        h = jax.nn.silu(xb @ w_gate[e]) * (xb @ w_up[e])
        return h @ w_down[e]

    ys = lax.map(expert_block, (xs, block_e)).reshape(P, D)
    ys = ys * row_gate[:, None].astype(ys.dtype)
    out = jnp.zeros((T + 1, D), ys.dtype).at[row_tok].add(ys)[:T]
    return out.reshape(B, S, D)


def setup_inputs(seed: int = 0) -> dict:
    key = jax.random.key(seed)
    ks = jax.random.split(key, 20)
    f32 = jnp.float32
    L, D = DEPTH, D_MODEL

    def nrm(k, shape, scale):
        return jax.random.normal(k, shape, f32) * scale

    x = jax.random.normal(ks[0], (BATCH, SEQ, D), f32)
    offsets = jax.random.randint(ks[1], (BATCH, 1), 0, 4096, dtype=jnp.int32)
    positions = offsets + jnp.arange(SEQ, dtype=jnp.int32)[None, :]
    return {
        "x": x,
        "positions": positions,
        "g_attn_norm": 1.0 + nrm(ks[2], (L, D), 0.02),
        "w_in": nrm(ks[3], (L, D, IN_WIDTH), D ** -0.5),
        "g_q_norm": 1.0 + nrm(ks[4], (L, MLA_Q_RANK), 0.02),
        "w_q_up": nrm(ks[5], (L, MLA_Q_RANK, MLA_HEADS * MLA_QK_DIM), MLA_Q_RANK ** -0.5),
        "g_kv_norm": 1.0 + nrm(ks[6], (L, MLA_KV_RANK), 0.02),
        "w_kv_up": nrm(ks[7], (L, MLA_KV_RANK, MLA_HEADS * (MLA_NOPE_DIM + MLA_V_DIM)), MLA_KV_RANK ** -0.5),
        "sinks": nrm(ks[8], (L, SWA_HEADS), 0.5),
        "w_out": nrm(ks[9], (L, MIX_WIDTH, D), MIX_WIDTH ** -0.5),
        "g_ffn_norm": 1.0 + nrm(ks[10], (L, D), 0.02),
        "w_router_group": nrm(ks[11], (L, D, N_GROUPS), D ** -0.5),
        "b_router_group": nrm(ks[12], (L, N_GROUPS), 0.01),
        "w_router_expert": nrm(ks[13], (L, D, N_EXPERTS), D ** -0.5),
        "b_router_expert": nrm(ks[14], (L, N_EXPERTS), 0.01),
        "w_expert_gate": nrm(ks[15], (L, N_EXPERTS, D, D_EXPERT), D ** -0.5),
        "w_expert_up": nrm(ks[16], (L, N_EXPERTS, D, D_EXPERT), D ** -0.5),
        "w_expert_down": nrm(ks[17], (L, N_EXPERTS, D_EXPERT, D), D_EXPERT ** -0.5),
        "g_final_norm": 1.0 + nrm(ks[18], (D,), 0.02),
    }


def reference(x, positions, g_attn_norm, w_in, g_q_norm, w_q_up, g_kv_norm, w_kv_up, sinks, w_out,
              g_ffn_norm, w_router_group, b_router_group, w_router_expert, b_router_expert,
              w_expert_gate, w_expert_up, w_expert_down, g_final_norm):
    h = x
    for l in range(DEPTH):
        n = rms_norm(h, g_attn_norm[l])
        z = n @ w_in[l]
        mla_o = mla_attention(z[..., :OFF_Q], z[..., OFF_Q:OFF_KV], z[..., OFF_KV:OFF_KPE], positions,
                              g_q_norm[l], g_kv_norm[l], w_q_up[l], w_kv_up[l])
        swa_o = swa_attention(z[..., OFF_KPE:OFF_SQ], z[..., OFF_SQ:OFF_SK], z[..., OFF_SK:], positions, sinks[l])
        h = h + jnp.concatenate([mla_o, swa_o], axis=-1) @ w_out[l]
        h = h + hier_moe(rms_norm(h, g_ffn_norm[l]), w_router_group[l], b_router_group[l],
                         w_router_expert[l], b_router_expert[l],
                         w_expert_gate[l], w_expert_up[l], w_expert_down[l])
    return rms_norm(h, g_final_norm)
```

```python
import functools

import jax
import jax.numpy as jnp
from jax import lax
from jax.experimental import pallas as pl
from jax.experimental.pallas import tpu as pltpu

F32 = jnp.float32
BF16 = jnp.bfloat16

D_MODEL = 4096
EPS = 1e-6
ROPE_THETA = 10000.0
NEG = -1e30
MLA_HEADS = 16
MLA_Q_RANK = 1024
MLA_KV_RANK = 512
MLA_NOPE_DIM = 128
MLA_ROPE_DIM = 64
MLA_V_DIM = 128
MLA_QK_DIM = MLA_NOPE_DIM + MLA_ROPE_DIM
MLA_HEAD_PAD = 256
SWA_HEADS = 32
SWA_KV_HEADS = 4
SWA_GROUP = SWA_HEADS // SWA_KV_HEADS
SWA_HEAD_DIM = 64
SWA_WINDOW = 128
N_GROUPS = 8
EXPERTS_PER_GROUP = 8
N_EXPERTS = N_GROUPS * EXPERTS_PER_GROUP
TOP_K = 2
D_EXPERT = 1024

LANES = 128
Z_OFF_CQ = 0
Z_OFF_CKV = MLA_Q_RANK
Z_OFF_SQ = Z_OFF_CKV + MLA_KV_RANK
Z_OFF_SKV = Z_OFF_SQ + SWA_HEADS * SWA_HEAD_DIM
Z_OFF_KPE = Z_OFF_SKV + SWA_KV_HEADS * LANES
Z_WIDTH = Z_OFF_KPE + LANES
ROPE_NONE, ROPE_FULL, ROPE_HALF = 0, 1, 2

MOE_ROWS = 512
MOE_SUB = 128
MOE_CHUNK = 256
VMEM_LIMIT = 56 * 1024 * 1024


def _cparams(sem, vmem=VMEM_LIMIT):
    return pltpu.CompilerParams(dimension_semantics=sem, vmem_limit_bytes=vmem)


def _swap32(x):
    lane = lax.broadcasted_iota(jnp.int32, x.shape, 1)
    fwd = pltpu.roll(x, LANES - 32, 1)
    bwd = pltpu.roll(x, 32, 1)
    return jnp.where((lane % 64) < 32, fwd, bwd)


def _rope_tile(x, cos, sin, kind):
    if kind == ROPE_NONE:
        return x
    if kind == ROPE_HALF:
        lane = lax.broadcasted_iota(jnp.int32, x.shape, 1)
        cos = jnp.where(lane < 64, cos, 1.0)
        sin = jnp.where(lane < 64, sin, 0.0)
    return x * cos + _swap32(x) * sin


def _rope_table_kernel(pos_ref, invf_ref, sign_ref, cos_ref, sin_ref):
    ang = pos_ref[...].astype(F32) * invf_ref[...]
    cos_ref[...] = jnp.cos(ang)
    sin_ref[...] = jnp.sin(ang) * sign_ref[...]


def _rope_tables(positions):
    t = positions.size
    pos = positions.reshape(t, 1)
    half = SWA_HEAD_DIM // 2
    inv_freq = ROPE_THETA ** (-jnp.arange(0, SWA_HEAD_DIM, 2, dtype=F32) / SWA_HEAD_DIM)
    lane = jnp.arange(LANES)
    invf = inv_freq[lane % half].reshape(1, LANES)
    sign = jnp.where((lane % SWA_HEAD_DIM) < half, -1.0, 1.0).astype(F32).reshape(1, LANES)
    tm = 1024
    return pl.pallas_call(
        _rope_table_kernel,
        grid=(t // tm,),
        in_specs=[pl.BlockSpec((tm, 1), lambda i: (i, 0)),
                  pl.BlockSpec((1, LANES), lambda i: (0, 0)),
                  pl.BlockSpec((1, LANES), lambda i: (0, 0))],
        out_specs=[pl.BlockSpec((tm, LANES), lambda i: (i, 0))] * 2,
        out_shape=[jax.ShapeDtypeStruct((t, LANES), F32)] * 2,
        compiler_params=_cparams(("parallel",)),
        name="rope_tables",
    )(pos, invf, sign)


def _rmsnorm_kernel(x_ref, g_ref, o_ref):
    xf = x_ref[...].astype(F32)
    ms = jnp.mean(xf * xf, axis=-1, keepdims=True)
    o_ref[...] = (xf * lax.rsqrt(ms + EPS) * g_ref[...]).astype(o_ref.dtype)


def _rmsnorm(x, g, *, tm=512):
    t, d = x.shape
    return pl.pallas_call(
        _rmsnorm_kernel,
        grid=(t // tm,),
        in_specs=[pl.BlockSpec((tm, d), lambda i: (i, 0)), pl.BlockSpec((1, d), lambda i: (0, 0))],
        out_specs=pl.BlockSpec((tm, d), lambda i: (i, 0)),
        out_shape=jax.ShapeDtypeStruct((t, d), BF16),
        compiler_params=_cparams(("parallel",)),
        name="attn_norm",
    )(x, g.reshape(1, d))


def _rope_epilogue(acc, cos_ref, sin_ref, o_ref, j, kinds, scales):
    def epilogue(tile_kinds, tile_scales):
        needs_rope = any(k != ROPE_NONE for k in tile_kinds)
        cos = cos_ref[...] if needs_rope else None
        sin = sin_ref[...] if needs_rope else None
        for c, (kind, scale) in enumerate(zip(tile_kinds, tile_scales)):
            blk = _rope_tile(acc[:, c * LANES:(c + 1) * LANES], cos, sin, kind)
            if scale != 1.0:
                blk = blk * scale
            o_ref[:, c * LANES:(c + 1) * LANES] = blk.astype(o_ref.dtype)

    if all(k == kinds[0] for k in kinds) and all(s == scales[0] for s in scales):
        epilogue(kinds[0], scales[0])
    else:
        for jj in range(len(kinds)):
            pl.when(j == jj)(functools.partial(epilogue, kinds[jj], scales[jj]))


def _per_tile(values, n_j, per):
    return tuple(tuple(values[jj * per:(jj + 1) * per]) for jj in range(n_j))


def _matmul_rope_kernel(x_ref, w_ref, cos_ref, sin_ref, o_ref, *, kinds, scales):
    acc = jnp.dot(x_ref[...], w_ref[...], preferred_element_type=F32)
    _rope_epilogue(acc, cos_ref, sin_ref, o_ref, pl.program_id(0), kinds, scales)


def _matmul_rope(x, w, cos, sin, *, tm, tn, kinds, scales, name):
    t, k = x.shape
    n = w.shape[1]
    n_j = n // tn
    return pl.pallas_call(
        functools.partial(_matmul_rope_kernel, kinds=_per_tile(kinds, n_j, tn // LANES),
                          scales=_per_tile(scales, n_j, tn // LANES)),
        grid=(n_j, t // tm),
        in_specs=[pl.BlockSpec((tm, k), lambda j, i: (i, 0)),
                  pl.BlockSpec((k, tn), lambda j, i: (0, j)),
                  pl.BlockSpec((tm, LANES), lambda j, i: (i, 0)),
                  pl.BlockSpec((tm, LANES), lambda j, i: (i, 0))],
        out_specs=pl.BlockSpec((tm, tn), lambda j, i: (i, j)),
        out_shape=jax.ShapeDtypeStruct((t, n), BF16),
        compiler_params=_cparams(("parallel", "parallel")),
        name=name,
    )(x, w, cos, sin)


def _norm_matmul_kernel(x_ref, g_ref, w_ref, cos_ref, sin_ref, o_ref, n_scr, *, kinds, scales):
    j = pl.program_id(1)

    @pl.when(j == 0)
    def _():
        xf = x_ref[...].astype(F32)
        ms = jnp.mean(xf * xf, axis=-1, keepdims=True)
        n_scr[...] = (xf * lax.rsqrt(ms + EPS) * g_ref[...]).astype(BF16)

    acc = jnp.dot(n_scr[...], w_ref[...], preferred_element_type=F32)
    _rope_epilogue(acc, cos_ref, sin_ref, o_ref, j, kinds, scales)


def _norm_matmul(x, x_col_block, k, g, w, cos, sin, *, tm, tn, kinds, scales, name):
    t = x.shape[0]
    n = w.shape[1]
    n_j = n // tn
    kinds = _per_tile(kinds, n_j, tn // LANES)
    scales = _per_tile(scales, n_j, tn // LANES)
    return pl.pallas_call(
        functools.partial(_norm_matmul_kernel, kinds=kinds, scales=scales),
        grid=(t // tm, n_j),
        in_specs=[pl.BlockSpec((tm, k), lambda i, j: (i, x_col_block)),
                  pl.BlockSpec((1, k), lambda i, j: (0, 0)),
                  pl.BlockSpec((k, tn), lambda i, j: (0, j)),
                  pl.BlockSpec((tm, LANES), lambda i, j: (i, 0)),
                  pl.BlockSpec((tm, LANES), lambda i, j: (i, 0))],
        out_specs=pl.BlockSpec((tm, tn), lambda i, j: (i, j)),
        out_shape=jax.ShapeDtypeStruct((t, n), BF16),
        scratch_shapes=[pltpu.VMEM((tm, k), BF16)],
        compiler_params=_cparams(("parallel", "arbitrary")),
        name=name,
    )(x, g.reshape(1, k), w, cos, sin)


def _mla_kernel(q_ref, kv_ref, kpe_ref, o_ref, kcat_scr, *, tq):
    s_len = q_ref.shape[0]
    kcat_scr[:, :MLA_NOPE_DIM] = kv_ref[:, :MLA_NOPE_DIM]
    kcat_scr[:, MLA_NOPE_DIM:] = kpe_ref[...]
    row = lax.broadcasted_iota(jnp.int32, (tq, tq), 0)
    col = lax.broadcasted_iota(jnp.int32, (tq, tq), 1)
    tri = col <= row
    nt = (((1,), (1,)), ((), ()))
    for qi in range(s_len // tq):
        lo, hi = qi * tq, (qi + 1) * tq
        q = q_ref[lo:hi, :]
        s_diag = lax.dot_general(q, kcat_scr[lo:hi, :], nt, preferred_element_type=F32)
        s_diag = jnp.where(tri, s_diag, NEG)
        m = jnp.max(s_diag, axis=-1, keepdims=True)
        if qi > 0:
            s_past = lax.dot_general(q, kcat_scr[:lo, :], nt, preferred_element_type=F32)
            m = jnp.maximum(m, jnp.max(s_past, axis=-1, keepdims=True))
            p_past = jnp.exp(s_past - m)
        p_diag = jnp.exp(s_diag - m)
        l = jnp.sum(p_diag, axis=-1, keepdims=True)
        o = jnp.dot(p_diag.astype(BF16), kv_ref[lo:hi, MLA_NOPE_DIM:], preferred_element_type=F32)
        if qi > 0:
            l = l + jnp.sum(p_past, axis=-1, keepdims=True)
            o = o + jnp.dot(p_past.astype(BF16), kv_ref[:lo, MLA_NOPE_DIM:], preferred_element_type=F32)
        o_ref[lo:hi, :] = (o / l).astype(o_ref.dtype)


def _mla_attention(q, kv, z, batch, seq):
    t = batch * seq
    kpe_block = Z_OFF_KPE // LANES
    return pl.pallas_call(
        functools.partial(_mla_kernel, tq=256),
        grid=(batch, MLA_HEADS),
        in_specs=[pl.BlockSpec((seq, MLA_HEAD_PAD), lambda b, h: (b, h)),
                  pl.BlockSpec((seq, MLA_NOPE_DIM + MLA_V_DIM), lambda b, h: (b, h)),
                  pl.BlockSpec((seq, LANES), lambda b, h: (b, kpe_block))],
        out_specs=pl.BlockSpec((seq, MLA_V_DIM), lambda b, h: (b, h)),
        out_shape=jax.ShapeDtypeStruct((t, MLA_HEADS * MLA_V_DIM), BF16),
        scratch_shapes=[pltpu.VMEM((seq, MLA_HEAD_PAD), BF16)],
        compiler_params=_cparams(("parallel", "arbitrary")),
        name="mla_attention",
    )(q, kv, z)


def _swa_kernel(sinks_ref, q_ref, kv_ref, o_ref, klo, khi, vlo, vhi):
    h = pl.program_id(1)
    s_len = q_ref.shape[0]
    w = SWA_WINDOW
    pairs = SWA_GROUP // 2
    kv = kv_ref[...].astype(F32)
    lane = lax.broadcasted_iota(jnp.int32, kv.shape, 1)
    low = lane < SWA_HEAD_DIM
    rolled = pltpu.roll(kv, SWA_HEAD_DIM, 1)
    pad = jnp.zeros((w, LANES), BF16)
    for scr, val in ((klo, jnp.where(low, kv, 0.0)), (khi, jnp.where(low, 0.0, rolled)),
                     (vlo, jnp.where(low, rolled, 0.0)), (vhi, jnp.where(low, 0.0, kv))):
        scr[:w, :] = pad
        scr[w:, :] = val.astype(BF16)

    rows = pairs * w
    r = lax.broadcasted_iota(jnp.int32, (rows, 2 * w), 0) % w
    c = lax.broadcasted_iota(jnp.int32, (rows, 2 * w), 1)
    band = (c > r) & (c <= r + w)
    pair_of_row = lax.broadcasted_iota(jnp.int32, (rows, 1), 0) // w
    out_low = lax.broadcasted_iota(jnp.int32, (rows, LANES), 1) < SWA_HEAD_DIM

    def sink_column(parity):
        col = jnp.zeros((rows, 1), F32)
        for j in range(pairs):
            col = jnp.where(pair_of_row == j, sinks_ref[h * SWA_GROUP + 2 * j + parity], col)
        return col

    sink_even, sink_odd = sink_column(0), sink_column(1)
    nt = (((1,), (1,)), ((), ()))

    def block(n, carry):
        r0 = pl.multiple_of(n * w, w)
        qs = jnp.concatenate([q_ref[pl.ds(r0, w), j * LANES:(j + 1) * LANES] for j in range(pairs)], axis=0)
        mask = band & (c >= jnp.where(n > 0, 0, w))

        def head(k_scr, v_scr, sink):
            s = lax.dot_general(qs, k_scr[pl.ds(r0, 2 * w), :], nt, preferred_element_type=F32)
            s = jnp.where(mask, s, NEG)
            m = jnp.maximum(jnp.max(s, axis=-1, keepdims=True), sink)
            p = jnp.exp(s - m)
            l = jnp.sum(p, axis=-1, keepdims=True) + jnp.exp(sink - m)
            o = jnp.dot(p.astype(BF16), v_scr[pl.ds(r0, 2 * w), :], preferred_element_type=F32)
            return o, l

        o_even, l_even = head(klo, vlo, sink_even)
        o_odd, l_odd = head(khi, vhi, sink_odd)
        o = (o_even + o_odd) / jnp.where(out_low, l_even, l_odd)
        for j in range(pairs):
            o_ref[pl.ds(r0, w), j * LANES:(j + 1) * LANES] = o[j * w:(j + 1) * w].astype(o_ref.dtype)
        return carry

    lax.fori_loop(0, s_len // w, block, 0)


def _swa_attention(z, sinks, batch, seq):
    t = batch * seq
    gw = SWA_GROUP * SWA_HEAD_DIM
    q_block0 = Z_OFF_SQ // gw
    kv_block0 = Z_OFF_SKV // LANES
    return pl.pallas_call(
        _swa_kernel,
        grid_spec=pltpu.PrefetchScalarGridSpec(
            num_scalar_prefetch=1,
            grid=(batch, SWA_KV_HEADS),
            in_specs=[pl.BlockSpec((seq, gw), lambda b, h, s: (b, q_block0 + h)),
                      pl.BlockSpec((seq, LANES), lambda b, h, s: (b, kv_block0 + h))],
            out_specs=pl.BlockSpec((seq, gw), lambda b, h, s: (b, h)),
            scratch_shapes=[pltpu.VMEM((seq + SWA_WINDOW, LANES), BF16)] * 4),
        out_shape=jax.ShapeDtypeStruct((t, SWA_HEADS * SWA_HEAD_DIM), BF16),
        compiler_params=_cparams(("parallel", "arbitrary")),
        name="swa_attention",
    )(sinks.astype(F32), z, z)


def _split_bf16(v):
    hi = v.astype(BF16)
    return hi, (v - hi.astype(F32)).astype(BF16)


def _out_proj_kernel(x_ref, a_ref, s_ref, wt_ref, wb_ref, g_ref, wr_ref, br_ref,
                     h_ref, lg_ref, inv_ref, acc_scr, ss_scr):
    j = pl.program_id(1)
    h = (x_ref[...] + jnp.dot(a_ref[...], wt_ref[...], preferred_element_type=F32)
         + jnp.dot(s_ref[...], wb_ref[...], preferred_element_type=F32))
    h_ref[...] = h
    hg_hi, hg_lo = _split_bf16(h * g_ref[...])
    wr_hi, wr_lo = _split_bf16(wr_ref[...])
    part = (jnp.dot(hg_hi, wr_hi, preferred_element_type=F32)
            + (jnp.dot(hg_hi, wr_lo, preferred_element_type=F32)
               + jnp.dot(hg_lo, wr_hi, preferred_element_type=F32)))
    ss = jnp.sum(h * h, axis=-1, keepdims=True)

    @pl.when(j == 0)
    def _():
        acc_scr[...] = part
        ss_scr[...] = ss

    @pl.when(j > 0)
    def _():
        acc_scr[...] += part
        ss_scr[...] += ss

    @pl.when(j == pl.num_programs(1) - 1)
    def _():
        inv = lax.rsqrt(ss_scr[...] / D_MODEL + EPS)
        lg_ref[...] = acc_scr[...] * inv + br_ref[...]
        inv_ref[...] = inv


def _out_proj(x2, mla_o, swa_o, w_out, g_ffn, w_r, b_r, *, tm=1024, tn=512):
    t, d = x2.shape
    half = mla_o.shape[1]
    return pl.pallas_call(
        _out_proj_kernel,
        grid=(t // tm, d // tn),
        in_specs=[pl.BlockSpec((tm, tn), lambda i, j: (i, j)),
                  pl.BlockSpec((tm, half), lambda i, j: (i, 0)),
                  pl.BlockSpec((tm, half), lambda i, j: (i, 0)),
                  pl.BlockSpec((half, tn), lambda i, j: (0, j)),
                  pl.BlockSpec((half, tn), lambda i, j: (1, j)),
                  pl.BlockSpec((1, tn), lambda i, j: (0, j)),
                  pl.BlockSpec((tn, LANES), lambda i, j: (j, 0)),
                  pl.BlockSpec((1, LANES), lambda i, j: (0, 0))],
        out_specs=[pl.BlockSpec((tm, tn), lambda i, j: (i, j)),
                   pl.BlockSpec((tm, LANES), lambda i, j: (i, 0)),
                   pl.BlockSpec((tm, 1), lambda i, j: (i, 0))],
        out_shape=[jax.ShapeDtypeStruct((t, d), F32),
                   jax.ShapeDtypeStruct((t, LANES), F32),
                   jax.ShapeDtypeStruct((t, 1), F32)],
        scratch_shapes=[pltpu.VMEM((tm, LANES), F32), pltpu.VMEM((tm, 1), F32)],
        compiler_params=_cparams(("parallel", "arbitrary")),
        name="out_proj_router",
    )(x2, mla_o, swa_o, w_out, w_out, g_ffn.reshape(1, d), w_r, b_r)


ROW_CHUNKS = D_MODEL // LANES
ROW_PITCH = 40


def _ffn_norm_kernel(h_ref, inv_ref, g_ref, o_ref):
    tm = h_ref.shape[0]
    inv = inv_ref[...]
    for c in range(ROW_CHUNKS):
        sl = slice(c * LANES, (c + 1) * LANES)
        o_ref[pl.ds(c, tm, stride=ROW_PITCH), :] = h_ref[:, sl] * inv * g_ref[:, sl]
    for c in range(ROW_CHUNKS, ROW_PITCH):
        o_ref[pl.ds(c, tm, stride=ROW_PITCH), :] = jnp.zeros((tm, LANES), o_ref.dtype)


def _ffn_norm(h, inv, g, *, tm=512):
    t, d = h.shape
    return pl.pallas_call(
        _ffn_norm_kernel,
        grid=(t // tm,),
        in_specs=[pl.BlockSpec((tm, d), lambda i: (i, 0)),
                  pl.BlockSpec((tm, 1), lambda i: (i, 0)),
                  pl.BlockSpec((1, d), lambda i: (0, 0))],
        out_specs=pl.BlockSpec((tm * ROW_PITCH, LANES), lambda i: (i, 0)),
        out_shape=jax.ShapeDtypeStruct((t * ROW_PITCH, LANES), F32),
        compiler_params=_cparams(("parallel",)),
        name="ffn_norm",
    )(h, inv, g.reshape(1, d))


MOE_UP_STEPS = D_EXPERT // MOE_CHUNK
MOE_STEPS = 2 * MOE_UP_STEPS
MOE_DOWN_TILE = D_MODEL // MOE_UP_STEPS


def _moe_kernel(item_e, item_cnt, tok_ref, dst_ref, hn_ref, wg_ref, wu_ref, wd_ref, pairs_ref,
                x_stage, x_bf, hid, out_stage, sem_in, sem_out):
    del item_e
    w = pl.program_id(0)
    s = pl.program_id(1)
    cnt = item_cnt[w]
    nsub = (cnt + MOE_SUB - 1) // MOE_SUB

    def row_in(tok, r):
        return pltpu.make_async_copy(hn_ref.at[pl.ds(pl.multiple_of(tok * ROW_PITCH, 8), ROW_PITCH)],
                                     x_stage.at[pl.ds(pl.multiple_of(r * ROW_PITCH, 8), ROW_PITCH)], sem_in)

    def row_out(slot, r):
        return pltpu.make_async_copy(out_stage.at[pl.ds(pl.multiple_of(r * ROW_PITCH, 8), ROW_PITCH)],
                                     pairs_ref.at[pl.ds(pl.multiple_of(slot * ROW_PITCH, 8), ROW_PITCH)], sem_out)

    @pl.when((w == 0) & (s == 0))
    def _():
        x_stage[...] = jnp.zeros(x_stage.shape, x_stage.dtype)
        out_stage[...] = jnp.zeros(out_stage.shape, out_stage.dtype)

    @pl.when((s == 0) & (cnt > 0))
    def _():
        def start(r, carry):
            row_in(tok_ref[0, 0, r], r).start()
            return carry

        def wait(r, carry):
            row_in(0, r).wait()
            return carry

        lax.fori_loop(0, cnt, start, 0)
        lax.fori_loop(0, cnt, wait, 0)

    def compute(m):
        @pl.when(s == 0)
        def _():
            for c in range(ROW_CHUNKS):
                x_bf[:m, c * LANES:(c + 1) * LANES] = x_stage[pl.ds(c, m, stride=ROW_PITCH), :].astype(BF16)

        @pl.when(s < MOE_UP_STEPS)
        def _():
            x = x_bf[:m, :]
            g = jnp.dot(x, wg_ref[...].astype(BF16), preferred_element_type=F32)
            u = jnp.dot(x, wu_ref[...].astype(BF16), preferred_element_type=F32)
            hid[s, :m, :] = ((g * jax.nn.sigmoid(g)) * u).astype(BF16)

        @pl.when(s >= MOE_UP_STEPS)
        def _():
            y = jnp.zeros((m, MOE_DOWN_TILE), F32)
            for c in range(MOE_UP_STEPS):
                y = y + jnp.dot(hid[c, :m, :], wd_ref[c * MOE_CHUNK:(c + 1) * MOE_CHUNK, :].astype(BF16),
                                preferred_element_type=F32)
            base = (s - MOE_UP_STEPS) * (MOE_DOWN_TILE // LANES)
            for j in range(MOE_DOWN_TILE // LANES):
                out_stage[pl.ds(base + j, m, stride=ROW_PITCH), :] = y[:, j * LANES:(j + 1) * LANES]

    for k in range(1, MOE_ROWS // MOE_SUB + 1):
        pl.when(nsub == k)(functools.partial(compute, k * MOE_SUB))

    @pl.when((s == MOE_STEPS - 1) & (cnt > 0))
    def _():
        def start(r, carry):
            row_out(dst_ref[0, 0, r], r).start()
            return carry

        def wait(r, carry):
            row_out(0, r).wait()
            return carry

        lax.fori_loop(0, cnt, start, 0)
        lax.fori_loop(0, cnt, wait, 0)


def _moe_experts(hn_lin, w_gate, w_up, w_down, item_e, item_cnt, slot_tok, slot_dst, n_pairs):
    n_items = item_e.shape[0]
    d = D_MODEL

    def up_chunk(s, cnt_ref, w):
        return jnp.where(cnt_ref[w] > 0, jnp.minimum(s, MOE_UP_STEPS - 1), MOE_UP_STEPS - 1)

    def down_tile(s, cnt_ref, w):
        return jnp.where(cnt_ref[w] > 0, jnp.maximum(s - MOE_UP_STEPS, 0), MOE_UP_STEPS - 1)

    return pl.pallas_call(
        _moe_kernel,
        grid_spec=pltpu.PrefetchScalarGridSpec(
            num_scalar_prefetch=2,
            grid=(n_items, MOE_STEPS),
            in_specs=[pl.BlockSpec((1, 1, MOE_ROWS), lambda w, s, e, cn: (w, 0, 0), memory_space=pltpu.SMEM),
                      pl.BlockSpec((1, 1, MOE_ROWS), lambda w, s, e, cn: (w, 0, 0), memory_space=pltpu.SMEM),
                      pl.BlockSpec(memory_space=pl.ANY),
                      pl.BlockSpec((None, d, MOE_CHUNK), lambda w, s, e, cn: (e[w], 0, up_chunk(s, cn, w))),
                      pl.BlockSpec((None, d, MOE_CHUNK), lambda w, s, e, cn: (e[w], 0, up_chunk(s, cn, w))),
                      pl.BlockSpec((None, D_EXPERT, MOE_DOWN_TILE),
                                   lambda w, s, e, cn: (e[w], 0, down_tile(s, cn, w)))],
            out_specs=pl.BlockSpec(memory_space=pl.ANY),
            scratch_shapes=[pltpu.VMEM((MOE_ROWS * ROW_PITCH, LANES), F32),
                            pltpu.VMEM((MOE_ROWS, d), BF16),
                            pltpu.VMEM((MOE_UP_STEPS, MOE_ROWS, MOE_CHUNK), BF16),
                            pltpu.VMEM((MOE_ROWS * ROW_PITCH, LANES), F32),
                            pltpu.SemaphoreType.DMA(()),
                            pltpu.SemaphoreType.DMA(())]),
        out_shape=jax.ShapeDtypeStruct((n_pairs * ROW_PITCH, LANES), F32),
        compiler_params=_cparams(("arbitrary", "arbitrary")),
        name="moe_experts",
    )(item_e, item_cnt, slot_tok.reshape(n_items, 1, MOE_ROWS), slot_dst.reshape(n_items, 1, MOE_ROWS),
      hn_lin, w_gate, w_up, w_down)


def _combine_kernel(h_ref, p0_ref, p1_ref, gate_ref, g_ref, o_ref):
    tm, d = h_ref.shape
    g0 = gate_ref[:, 0:1]
    g1 = gate_ref[:, 1:2]
    ss = jnp.zeros((tm, 1), F32)
    for c in range(ROW_CHUNKS):
        sl = slice(c * LANES, (c + 1) * LANES)
        a = p0_ref[pl.ds(c, tm, stride=ROW_PITCH), :]
        b = p1_ref[pl.ds(c, tm, stride=ROW_PITCH), :]
        y = h_ref[:, sl] + (a * g0 + b * g1)
        o_ref[:, sl] = y
        ss = ss + jnp.sum(y * y, axis=-1, keepdims=True)
    inv = lax.rsqrt(ss / d + EPS)
    o_ref[...] = o_ref[...] * inv * g_ref[...]


def _combine(h, pairs, gates, g_final, *, tm=256):
    t, d = h.shape
    k1 = t // tm
    return pl.pallas_call(
        _combine_kernel,
        grid=(t // tm,),
        in_specs=[pl.BlockSpec((tm, d), lambda i: (i, 0)),
                  pl.BlockSpec((tm * ROW_PITCH, LANES), lambda i: (i, 0)),
                  pl.BlockSpec((tm * ROW_PITCH, LANES), lambda i: (k1 + i, 0)),
                  pl.BlockSpec((tm, TOP_K), lambda i: (i, 0)),
                  pl.BlockSpec((1, d), lambda i: (0, 0))],
        out_specs=pl.BlockSpec((tm, d), lambda i: (i, 0)),
        out_shape=jax.ShapeDtypeStruct((t, d), F32),
        compiler_params=_cparams(("parallel",)),
        name="combine_final_norm",
    )(h, pairs, pairs, gates, g_final.reshape(1, d))


def _route(logits, t):
    group_prob = jax.nn.softmax(logits[:, :N_GROUPS], axis=-1)
    g_sel = jnp.argmax(group_prob, axis=-1)
    g_p = jnp.take_along_axis(group_prob, g_sel[:, None], axis=-1)
    e_logits = logits[:, N_GROUPS:N_GROUPS + N_EXPERTS].reshape(t, N_GROUPS, EXPERTS_PER_GROUP)
    in_group = jnp.take_along_axis(e_logits, g_sel[:, None, None], axis=1)[:, 0]
    top_p, top_i = lax.top_k(jax.nn.softmax(in_group, axis=-1), TOP_K)
    gates = g_p * top_p / jnp.sum(top_p, axis=-1, keepdims=True)
    expert_idx = (g_sel[:, None] * EXPERTS_PER_GROUP + top_i).astype(jnp.int32)

    tk = t * TOP_K
    n_items_max = tk // MOE_ROWS + N_EXPERTS
    flat_e = expert_idx.reshape(tk)
    order = jnp.argsort(flat_e).astype(jnp.int32)
    sorted_e = flat_e[order]
    counts = jnp.bincount(flat_e, length=N_EXPERTS).astype(jnp.int32)
    padded = (counts + MOE_ROWS - 1) // MOE_ROWS * MOE_ROWS
    pad_end = jnp.cumsum(padded)
    pad_start = pad_end - padded
    start = jnp.cumsum(counts) - counts
    slot = pad_start[sorted_e] + (jnp.arange(tk, dtype=jnp.int32) - start[sorted_e])
    n_slots = n_items_max * MOE_ROWS
    slot_tok = jnp.zeros((n_slots,), jnp.int32).at[slot].set(order // TOP_K)
    slot_dst = jnp.zeros((n_slots,), jnp.int32).at[slot].set((order % TOP_K) * t + order // TOP_K)

    item_row = jnp.arange(n_items_max, dtype=jnp.int32) * MOE_ROWS
    item_active = item_row < pad_end[-1]
    item_e = jnp.minimum(jnp.searchsorted(pad_end, item_row, side="right"), N_EXPERTS - 1).astype(jnp.int32)
    item_cnt = jnp.clip(counts[item_e] - (item_row - pad_start[item_e]), 0, MOE_ROWS)
    item_cnt = jnp.where(item_active, item_cnt, 0).astype(jnp.int32)
    last = jnp.maximum(pad_end[-1] // MOE_ROWS - 1, 0)
    item_e = jnp.where(item_active, item_e, item_e[last]).astype(jnp.int32)
    return gates.astype(F32), item_e, item_cnt, slot_tok, slot_dst


def _prep_in_weight(w_in):
    d = w_in.shape[0]
    off_q = MLA_Q_RANK
    off_kv = off_q + MLA_KV_RANK
    off_kpe = off_kv + MLA_ROPE_DIM
    off_sq = off_kpe + SWA_HEADS * SWA_HEAD_DIM
    off_sk = off_sq + SWA_KV_HEADS * SWA_HEAD_DIM
    sk = w_in[:, off_sq:off_sk].reshape(d, SWA_KV_HEADS, SWA_HEAD_DIM)
    sv = w_in[:, off_sk:].reshape(d, SWA_KV_HEADS, SWA_HEAD_DIM)
    skv = jnp.concatenate([sk, sv], axis=2).reshape(d, SWA_KV_HEADS * LANES)
    kpe = jnp.concatenate([w_in[:, off_kv:off_kpe], jnp.zeros((d, LANES - MLA_ROPE_DIM), w_in.dtype)], axis=1)
    return jnp.concatenate([w_in[:, :off_kv], w_in[:, off_kpe:off_sq], skv, kpe], axis=1).astype(BF16)


def _prep_q_weight(w_q_up):
    r = w_q_up.shape[0]
    w = w_q_up.reshape(r, MLA_HEADS, MLA_QK_DIM)
    w = jnp.pad(w, ((0, 0), (0, 0), (0, MLA_HEAD_PAD - MLA_QK_DIM)))
    return w.reshape(r, MLA_HEADS * MLA_HEAD_PAD).astype(BF16)


def kernel(x, positions, g_attn_norm, w_in, g_q_norm, w_q_up, g_kv_norm, w_kv_up, sinks, w_out,
           g_ffn_norm, w_router_group, b_router_group, w_router_expert, b_router_expert,
           w_expert_gate, w_expert_up, w_expert_down, g_final_norm):
    batch, seq, d = x.shape
    t = batch * seq
    assert g_attn_norm.shape[0] == 1, "the final rmsnorm is fused into the (single) layer's combine"
    h = x.reshape(t, d)
    cos, sin = _rope_tables(positions)
    for l in range(g_attn_norm.shape[0]):
        z_kinds = ([ROPE_NONE] * (Z_OFF_SQ // LANES) + [ROPE_FULL] * ((Z_OFF_SKV - Z_OFF_SQ) // LANES)
                   + [ROPE_HALF] * ((Z_WIDTH - Z_OFF_SKV) // LANES))
        z_scales = [SWA_HEAD_DIM ** -0.5 if k == ROPE_FULL else 1.0 for k in z_kinds]
        z = _matmul_rope(_rmsnorm(h, g_attn_norm[l]), _prep_in_weight(w_in[l]), cos, sin,
                         tm=512, tn=Z_WIDTH // 3, kinds=z_kinds, scales=z_scales, name="in_proj")
        q_kinds = [ROPE_NONE, ROPE_HALF] * MLA_HEADS
        q = _norm_matmul(z, Z_OFF_CQ // MLA_Q_RANK, MLA_Q_RANK, g_q_norm[l], _prep_q_weight(w_q_up[l]), cos, sin,
                         tm=1024, tn=1024, kinds=q_kinds, scales=[MLA_QK_DIM ** -0.5] * len(q_kinds), name="q_up")
        kv_kinds = [ROPE_NONE] * (w_kv_up.shape[2] // LANES)
        kv = _norm_matmul(z, Z_OFF_CKV // MLA_KV_RANK, MLA_KV_RANK, g_kv_norm[l], w_kv_up[l].astype(BF16), cos, sin,
                          tm=1024, tn=1024, kinds=kv_kinds, scales=[1.0] * len(kv_kinds), name="kv_up")
        mla_o = _mla_attention(q, kv, z, batch, seq)
        swa_o = _swa_attention(z, sinks[l], batch, seq)
        w_r = jnp.concatenate([w_router_group[l], w_router_expert[l],
                               jnp.zeros((d, LANES - N_GROUPS - N_EXPERTS), F32)], axis=1)
        b_r = jnp.concatenate([b_router_group[l], b_router_expert[l],
                               jnp.zeros((LANES - N_GROUPS - N_EXPERTS,), F32)]).reshape(1, LANES)
        h, logits, inv = _out_proj(h, mla_o, swa_o, w_out[l].astype(BF16), g_ffn_norm[l], w_r, b_r)
        gates, item_e, item_cnt, slot_tok, slot_dst = _route(logits, t)
        hn = _ffn_norm(h, inv, g_ffn_norm[l])
        pairs = _moe_experts(hn, w_expert_gate[l], w_expert_up[l], w_expert_down[l],
                             item_e, item_cnt, slot_tok, slot_dst, TOP_K * t)
        out = _combine(h, pairs, gates, g_final_norm)
    return out.reshape(batch, seq, d)
```

```python
import functools

import jax
import jax.numpy as jnp
from jax import lax
from jax.experimental import pallas as pl
from jax.experimental.pallas import tpu as pltpu

F32 = jnp.float32
BF16 = jnp.bfloat16

D_MODEL = 4096
EPS = 1e-6
ROPE_THETA = 10000.0
NEG = -1e30
MLA_HEADS = 16
MLA_Q_RANK = 1024
MLA_KV_RANK = 512
MLA_NOPE_DIM = 128
MLA_ROPE_DIM = 64
MLA_V_DIM = 128
MLA_QK_DIM = MLA_NOPE_DIM + MLA_ROPE_DIM
MLA_HEAD_PAD = 256
SWA_HEADS = 32
SWA_KV_HEADS = 4
SWA_GROUP = SWA_HEADS // SWA_KV_HEADS
SWA_HEAD_DIM = 64
SWA_WINDOW = 128
N_GROUPS = 8
EXPERTS_PER_GROUP = 8
N_EXPERTS = N_GROUPS * EXPERTS_PER_GROUP
TOP_K = 2
D_EXPERT = 1024

LANES = 128
Z_OFF_CQ = 0
Z_OFF_CKV = MLA_Q_RANK
Z_OFF_SQ = Z_OFF_CKV + MLA_KV_RANK
Z_OFF_SKV = Z_OFF_SQ + SWA_HEADS * SWA_HEAD_DIM
Z_OFF_KPE = Z_OFF_SKV + SWA_KV_HEADS * LANES
Z_WIDTH = Z_OFF_KPE + LANES
ROPE_NONE, ROPE_FULL, ROPE_HALF = 0, 1, 2

MOE_ROWS = 512
MOE_SUB = 128
MOE_CHUNK = 256
VMEM_LIMIT = 56 * 1024 * 1024


def _cparams(sem, vmem=VMEM_LIMIT):
    return pltpu.CompilerParams(dimension_semantics=sem, vmem_limit_bytes=vmem)


def _swap32(x):
    lane = lax.broadcasted_iota(jnp.int32, x.shape, 1)
    fwd = pltpu.roll(x, LANES - 32, 1)
    bwd = pltpu.roll(x, 32, 1)
    return jnp.where((lane % 64) < 32, fwd, bwd)


def _rope_tile(x, cos, sin, kind):
    if kind == ROPE_NONE:
        return x
    if kind == ROPE_HALF:
        lane = lax.broadcasted_iota(jnp.int32, x.shape, 1)
        cos = jnp.where(lane < 64, cos, 1.0)
        sin = jnp.where(lane < 64, sin, 0.0)
    return x * cos + _swap32(x) * sin


def _rope_table_kernel(pos_ref, invf_ref, sign_ref, cos_ref, sin_ref):
    ang = pos_ref[...].astype(F32) * invf_ref[...]
    cos_ref[...] = jnp.cos(ang)
    sin_ref[...] = jnp.sin(ang) * sign_ref[...]


def _rope_tables(positions):
    t = positions.size
    pos = positions.reshape(t, 1)
    half = SWA_HEAD_DIM // 2
    inv_freq = ROPE_THETA ** (-jnp.arange(0, SWA_HEAD_DIM, 2, dtype=F32) / SWA_HEAD_DIM)
    lane = jnp.arange(LANES)
    invf = inv_freq[lane % half].reshape(1, LANES)
    sign = jnp.where((lane % SWA_HEAD_DIM) < half, -1.0, 1.0).astype(F32).reshape(1, LANES)
    tm = 1024
    return pl.pallas_call(
        _rope_table_kernel,
        grid=(t // tm,),
        in_specs=[pl.BlockSpec((tm, 1), lambda i: (i, 0)),
                  pl.BlockSpec((1, LANES), lambda i: (0, 0)),
                  pl.BlockSpec((1, LANES), lambda i: (0, 0))],
        out_specs=[pl.BlockSpec((tm, LANES), lambda i: (i, 0))] * 2,
        out_shape=[jax.ShapeDtypeStruct((t, LANES), F32)] * 2,
        compiler_params=_cparams(("parallel",)),
        name="rope_tables",
    )(pos, invf, sign)


def _rmsnorm_kernel(x_ref, g_ref, o_ref):
    xf = x_ref[...].astype(F32)
    ms = jnp.mean(xf * xf, axis=-1, keepdims=True)
    o_ref[...] = (xf * lax.rsqrt(ms + EPS) * g_ref[...]).astype(o_ref.dtype)


def _rmsnorm(x, g, *, tm=512):
    t, d = x.shape
    return pl.pallas_call(
        _rmsnorm_kernel,
        grid=(t // tm,),
        in_specs=[pl.BlockSpec((tm, d), lambda i: (i, 0)), pl.BlockSpec((1, d), lambda i: (0, 0))],
        out_specs=pl.BlockSpec((tm, d), lambda i: (i, 0)),
        out_shape=jax.ShapeDtypeStruct((t, d), BF16),
        compiler_params=_cparams(("parallel",)),
        name="attn_norm",
    )(x, g.reshape(1, d))


def _rope_epilogue(acc, cos_ref, sin_ref, o_ref, j, kinds, scales):
    def epilogue(tile_kinds, tile_scales):
        needs_rope = any(k != ROPE_NONE for k in tile_kinds)
        cos = cos_ref[...] if needs_rope else None
        sin = sin_ref[...] if needs_rope else None
        for c, (kind, scale) in enumerate(zip(tile_kinds, tile_scales)):
            blk = _rope_tile(acc[:, c * LANES:(c + 1) * LANES], cos, sin, kind)
            if scale != 1.0:
                blk = blk * scale
            o_ref[:, c * LANES:(c + 1) * LANES] = blk.astype(o_ref.dtype)

    if all(k == kinds[0] for k in kinds) and all(s == scales[0] for s in scales):
        epilogue(kinds[0], scales[0])
    else:
        for jj in range(len(kinds)):
            pl.when(j == jj)(functools.partial(epilogue, kinds[jj], scales[jj]))


def _per_tile(values, n_j, per):
    return tuple(tuple(values[jj * per:(jj + 1) * per]) for jj in range(n_j))


def _matmul_rope_kernel(x_ref, w_ref, cos_ref, sin_ref, o_ref, *, kinds, scales):
    acc = jnp.dot(x_ref[...], w_ref[...], preferred_element_type=F32)
    _rope_epilogue(acc, cos_ref, sin_ref, o_ref, pl.program_id(0), kinds, scales)


def _matmul_rope(x, w, cos, sin, *, tm, tn, kinds, scales, name):
    t, k = x.shape
    n = w.shape[1]
    n_j = n // tn
    return pl.pallas_call(
        functools.partial(_matmul_rope_kernel, kinds=_per_tile(kinds, n_j, tn // LANES),
                          scales=_per_tile(scales, n_j, tn // LANES)),
        grid=(n_j, t // tm),
        in_specs=[pl.BlockSpec((tm, k), lambda j, i: (i, 0)),
                  pl.BlockSpec((k, tn), lambda j, i: (0, j)),
                  pl.BlockSpec((tm, LANES), lambda j, i: (i, 0)),
                  pl.BlockSpec((tm, LANES), lambda j, i: (i, 0))],
        out_specs=pl.BlockSpec((tm, tn), lambda j, i: (i, j)),
        out_shape=jax.ShapeDtypeStruct((t, n), BF16),
        compiler_params=_cparams(("parallel", "parallel")),
        name=name,
    )(x, w, cos, sin)


def _norm_matmul_kernel(x_ref, g_ref, w_ref, cos_ref, sin_ref, o_ref, n_scr, *, kinds, scales):
    j = pl.program_id(1)

    @pl.when(j == 0)
    def _():
        xf = x_ref[...].astype(F32)
        ms = jnp.mean(xf * xf, axis=-1, keepdims=True)
        n_scr[...] = (xf * lax.rsqrt(ms + EPS) * g_ref[...]).astype(BF16)

    acc = jnp.dot(n_scr[...], w_ref[...], preferred_element_type=F32)
    _rope_epilogue(acc, cos_ref, sin_ref, o_ref, j, kinds, scales)


def _norm_matmul(x, x_col_block, k, g, w, cos, sin, *, tm, tn, kinds, scales, name):
    t = x.shape[0]
    n = w.shape[1]
    n_j = n // tn
    kinds = _per_tile(kinds, n_j, tn // LANES)
    scales = _per_tile(scales, n_j, tn // LANES)
    return pl.pallas_call(
        functools.partial(_norm_matmul_kernel, kinds=kinds, scales=scales),
        grid=(t // tm, n_j),
        in_specs=[pl.BlockSpec((tm, k), lambda i, j: (i, x_col_block)),
                  pl.BlockSpec((1, k), lambda i, j: (0, 0)),
                  pl.BlockSpec((k, tn), lambda i, j: (0, j)),
                  pl.BlockSpec((tm, LANES), lambda i, j: (i, 0)),
                  pl.BlockSpec((tm, LANES), lambda i, j: (i, 0))],
        out_specs=pl.BlockSpec((tm, tn), lambda i, j: (i, j)),
        out_shape=jax.ShapeDtypeStruct((t, n), BF16),
        scratch_shapes=[pltpu.VMEM((tm, k), BF16)],
        compiler_params=_cparams(("parallel", "arbitrary")),
        name=name,
    )(x, g.reshape(1, k), w, cos, sin)


def _mla_kernel(q_ref, kv_ref, kpe_ref, o_ref, kcat_scr, *, tq):
    s_len = q_ref.shape[0]
    kcat_scr[:, :MLA_NOPE_DIM] = kv_ref[:, :MLA_NOPE_DIM]
    kcat_scr[:, MLA_NOPE_DIM:] = kpe_ref[...]
    row = lax.broadcasted_iota(jnp.int32, (tq, tq), 0)
    col = lax.broadcasted_iota(jnp.int32, (tq, tq), 1)
    tri = col <= row
    nt = (((1,), (1,)), ((), ()))
    for qi in range(s_len // tq):
        lo, hi = qi * tq, (qi + 1) * tq
        q = q_ref[lo:hi, :]
        s_diag = lax.dot_general(q, kcat_scr[lo:hi, :], nt, preferred_element_type=F32)
        s_diag = jnp.where(tri, s_diag, NEG)
        m = jnp.max(s_diag, axis=-1, keepdims=True)
        if qi > 0:
            s_past = lax.dot_general(q, kcat_scr[:lo, :], nt, preferred_element_type=F32)
            m = jnp.maximum(m, jnp.max(s_past, axis=-1, keepdims=True))
            p_past = jnp.exp(s_past - m)
        p_diag = jnp.exp(s_diag - m)
        l = jnp.sum(p_diag, axis=-1, keepdims=True)
        o = jnp.dot(p_diag.astype(BF16), kv_ref[lo:hi, MLA_NOPE_DIM:], preferred_element_type=F32)
        if qi > 0:
            l = l + jnp.sum(p_past, axis=-1, keepdims=True)
            o = o + jnp.dot(p_past.astype(BF16), kv_ref[:lo, MLA_NOPE_DIM:], preferred_element_type=F32)
        o_ref[lo:hi, :] = (o / l).astype(o_ref.dtype)


def _mla_attention(q, kv, z, batch, seq):
    t = batch * seq
    kpe_block = Z_OFF_KPE // LANES
    return pl.pallas_call(
        functools.partial(_mla_kernel, tq=256),
        grid=(batch, MLA_HEADS),
        in_specs=[pl.BlockSpec((seq, MLA_HEAD_PAD), lambda b, h: (b, h)),
                  pl.BlockSpec((seq, MLA_NOPE_DIM + MLA_V_DIM), lambda b, h: (b, h)),
                  pl.BlockSpec((seq, LANES), lambda b, h: (b, kpe_block))],
        out_specs=pl.BlockSpec((seq, MLA_V_DIM), lambda b, h: (b, h)),
        out_shape=jax.ShapeDtypeStruct((t, MLA_HEADS * MLA_V_DIM), BF16),
        scratch_shapes=[pltpu.VMEM((seq, MLA_HEAD_PAD), BF16)],
        compiler_params=_cparams(("parallel", "arbitrary")),
        name="mla_attention",
    )(q, kv, z)


def _swa_kernel(sinks_ref, q_ref, kv_ref, o_ref, klo, khi, vlo, vhi):
    h = pl.program_id(1)
    s_len = q_ref.shape[0]
    w = SWA_WINDOW
    pairs = SWA_GROUP // 2
    kv = kv_ref[...].astype(F32)
    lane = lax.broadcasted_iota(jnp.int32, kv.shape, 1)
    low = lane < SWA_HEAD_DIM
    rolled = pltpu.roll(kv, SWA_HEAD_DIM, 1)
    pad = jnp.zeros((w, LANES), BF16)
    for scr, val in ((klo, jnp.where(low, kv, 0.0)), (khi, jnp.where(low, 0.0, rolled)),
                     (vlo, jnp.where(low, rolled, 0.0)), (vhi, jnp.where(low, 0.0, kv))):
        scr[:w, :] = pad
        scr[w:, :] = val.astype(BF16)

    rows = pairs * w
    r = lax.broadcasted_iota(jnp.int32, (rows, 2 * w), 0) % w
    c = lax.broadcasted_iota(jnp.int32, (rows, 2 * w), 1)
    band = (c > r) & (c <= r + w)
    pair_of_row = lax.broadcasted_iota(jnp.int32, (rows, 1), 0) // w
    out_low = lax.broadcasted_iota(jnp.int32, (rows, LANES), 1) < SWA_HEAD_DIM

    def sink_column(parity):
        col = jnp.zeros((rows, 1), F32)
        for j in range(pairs):
            col = jnp.where(pair_of_row == j, sinks_ref[h * SWA_GROUP + 2 * j + parity], col)
        return col

    sink_even, sink_odd = sink_column(0), sink_column(1)
    nt = (((1,), (1,)), ((), ()))

    def block(n, carry):
        r0 = pl.multiple_of(n * w, w)
        qs = jnp.concatenate([q_ref[pl.ds(r0, w), j * LANES:(j + 1) * LANES] for j in range(pairs)], axis=0)
        mask = band & (c >= jnp.where(n > 0, 0, w))

        def head(k_scr, v_scr, sink):
            s = lax.dot_general(qs, k_scr[pl.ds(r0, 2 * w), :], nt, preferred_element_type=F32)
            s = jnp.where(mask, s, NEG)
            m = jnp.maximum(jnp.max(s, axis=-1, keepdims=True), sink)
            p = jnp.exp(s - m)
            l = jnp.sum(p, axis=-1, keepdims=True) + jnp.exp(sink - m)
            o = jnp.dot(p.astype(BF16), v_scr[pl.ds(r0, 2 * w), :], preferred_element_type=F32)
            return o, l

        o_even, l_even = head(klo, vlo, sink_even)
        o_odd, l_odd = head(khi, vhi, sink_odd)
        o = (o_even + o_odd) / jnp.where(out_low, l_even, l_odd)
        for j in range(pairs):
            o_ref[pl.ds(r0, w), j * LANES:(j + 1) * LANES] = o[j * w:(j + 1) * w].astype(o_ref.dtype)
        return carry

    lax.fori_loop(0, s_len // w, block, 0)


def _swa_attention(z, sinks, batch, seq):
    t = batch * seq
    gw = SWA_GROUP * SWA_HEAD_DIM
    q_block0 = Z_OFF_SQ // gw
    kv_block0 = Z_OFF_SKV // LANES
    return pl.pallas_call(
        _swa_kernel,
        grid_spec=pltpu.PrefetchScalarGridSpec(
            num_scalar_prefetch=1,
            grid=(batch, SWA_KV_HEADS),
            in_specs=[pl.BlockSpec((seq, gw), lambda b, h, s: (b, q_block0 + h)),
                      pl.BlockSpec((seq, LANES), lambda b, h, s: (b, kv_block0 + h))],
            out_specs=pl.BlockSpec((seq, gw), lambda b, h, s: (b, h)),
            scratch_shapes=[pltpu.VMEM((seq + SWA_WINDOW, LANES), BF16)] * 4),
        out_shape=jax.ShapeDtypeStruct((t, SWA_HEADS * SWA_HEAD_DIM), BF16),
        compiler_params=_cparams(("parallel", "arbitrary")),
        name="swa_attention",
    )(sinks.astype(F32), z, z)


def _split_bf16(v):
    hi = v.astype(BF16)
    return hi, (v - hi.astype(F32)).astype(BF16)


def _out_proj_kernel(x_ref, a_ref, s_ref, wt_ref, wb_ref, g_ref, wr_ref, br_ref,
                     h_ref, lg_ref, inv_ref, acc_scr, ss_scr):
    j = pl.program_id(1)
    h = (x_ref[...] + jnp.dot(a_ref[...], wt_ref[...], preferred_element_type=F32)
         + jnp.dot(s_ref[...], wb_ref[...], preferred_element_type=F32))
    h_ref[...] = h
    hg_hi, hg_lo = _split_bf16(h * g_ref[...])
    wr_hi, wr_lo = _split_bf16(wr_ref[...])
    part = (jnp.dot(hg_hi, wr_hi, preferred_element_type=F32)
            + (jnp.dot(hg_hi, wr_lo, preferred_element_type=F32)
               + jnp.dot(hg_lo, wr_hi, preferred_element_type=F32)))
    ss = jnp.sum(h * h, axis=-1, keepdims=True)

    @pl.when(j == 0)
    def _():
        acc_scr[...] = part
        ss_scr[...] = ss

    @pl.when(j > 0)
    def _():
        acc_scr[...] += part
        ss_scr[...] += ss

    @pl.when(j == pl.num_programs(1) - 1)
    def _():
        inv = lax.rsqrt(ss_scr[...] / D_MODEL + EPS)
        lg_ref[...] = acc_scr[...] * inv + br_ref[...]
        inv_ref[...] = inv


def _out_proj(x2, mla_o, swa_o, w_out, g_ffn, w_r, b_r, *, tm=1024, tn=512):
    t, d = x2.shape
    half = mla_o.shape[1]
    return pl.pallas_call(
        _out_proj_kernel,
        grid=(t // tm, d // tn),
        in_specs=[pl.BlockSpec((tm, tn), lambda i, j: (i, j)),
                  pl.BlockSpec((tm, half), lambda i, j: (i, 0)),
                  pl.BlockSpec((tm, half), lambda i, j: (i, 0)),
                  pl.BlockSpec((half, tn), lambda i, j: (0, j)),
                  pl.BlockSpec((half, tn), lambda i, j: (1, j)),
                  pl.BlockSpec((1, tn), lambda i, j: (0, j)),
                  pl.BlockSpec((tn, LANES), lambda i, j: (j, 0)),
                  pl.BlockSpec((1, LANES), lambda i, j: (0, 0))],
        out_specs=[pl.BlockSpec((tm, tn), lambda i, j: (i, j)),
                   pl.BlockSpec((tm, LANES), lambda i, j: (i, 0)),
                   pl.BlockSpec((tm, 1), lambda i, j: (i, 0))],
        out_shape=[jax.ShapeDtypeStruct((t, d), F32),
                   jax.ShapeDtypeStruct((t, LANES), F32),
                   jax.ShapeDtypeStruct((t, 1), F32)],
        scratch_shapes=[pltpu.VMEM((tm, LANES), F32), pltpu.VMEM((tm, 1), F32)],
        compiler_params=_cparams(("parallel", "arbitrary")),
        name="out_proj_router",
    )(x2, mla_o, swa_o, w_out, w_out, g_ffn.reshape(1, d), w_r, b_r)


ROW_CHUNKS = D_MODEL // LANES
ROW_PITCH = 40


def _ffn_norm_kernel(h_ref, inv_ref, g_ref, o_ref):
    tm = h_ref.shape[0]
    inv = inv_ref[...]
    for c in range(ROW_CHUNKS):
        sl = slice(c * LANES, (c + 1) * LANES)
        o_ref[pl.ds(c, tm, stride=ROW_PITCH), :] = h_ref[:, sl] * inv * g_ref[:, sl]
    for c in range(ROW_CHUNKS, ROW_PITCH):
        o_ref[pl.ds(c, tm, stride=ROW_PITCH), :] = jnp.zeros((tm, LANES), o_ref.dtype)


def _ffn_norm(h, inv, g, *, tm=512):
    t, d = h.shape
    return pl.pallas_call(
        _ffn_norm_kernel,
        grid=(t // tm,),
        in_specs=[pl.BlockSpec((tm, d), lambda i: (i, 0)),
                  pl.BlockSpec((tm, 1), lambda i: (i, 0)),
                  pl.BlockSpec((1, d), lambda i: (0, 0))],
        out_specs=pl.BlockSpec((tm * ROW_PITCH, LANES), lambda i: (i, 0)),
        out_shape=jax.ShapeDtypeStruct((t * ROW_PITCH, LANES), F32),
        compiler_params=_cparams(("parallel",)),
        name="ffn_norm",
    )(h, inv, g.reshape(1, d))


MOE_UP_STEPS = D_EXPERT // MOE_CHUNK
MOE_STEPS = 2 * MOE_UP_STEPS
MOE_DOWN_TILE = D_MODEL // MOE_UP_STEPS


LANES_LOG2 = LANES.bit_length() - 1
IDX_TILE_ROWS = 8
IDX_TILE = IDX_TILE_ROWS * LANES
IDX_TILE_LOG2 = IDX_TILE.bit_length() - 1
IDX_WINDOW_ROWS = 2 * IDX_TILE_ROWS
assert MOE_ROWS <= IDX_TILE


def _moe_kernel(item_e, item_cnt, item_row0, dstk_ref, hn_ref, wg_ref, wu_ref, wd_ref, pairs_ref,
                idx_win, x_stage, x_bf, hid, out_stage, sem_idx, sem_in, sem_out, *, n_tokens):
    del item_e
    w = pl.program_id(0)
    s = pl.program_id(1)
    n_items = pl.num_programs(0)
    cnt = item_cnt[w]
    nsub = (cnt + MOE_SUB - 1) // MOE_SUB
    nxt_item = jnp.minimum(w + 1, n_items - 1)
    has_next = (w + 1 < n_items) & (item_cnt[nxt_item] > 0)
    cur = w & 1
    nxt = 1 - cur

    def idx_copy(item, buf):
        first = (item_row0[item] >> IDX_TILE_LOG2) * IDX_TILE_ROWS
        return pltpu.make_async_copy(dstk_ref.at[pl.ds(pl.multiple_of(first, IDX_TILE_ROWS), IDX_WINDOW_ROWS)],
                                     idx_win.at[buf], sem_idx)

    def for_rows(item, buf, fn):
        base = item_row0[item] & (IDX_TILE - 1)

        def body(r, carry):
            off = base + r
            fn(idx_win[buf, off >> LANES_LOG2, off & (LANES - 1)], r)
            return carry

        lax.fori_loop(0, item_cnt[item], body, 0)

    def row_in(tok, r):
        return pltpu.make_async_copy(hn_ref.at[pl.ds(pl.multiple_of(tok * ROW_PITCH, 8), ROW_PITCH)],
                                     x_stage.at[pl.ds(pl.multiple_of(r * ROW_PITCH, 8), ROW_PITCH)], sem_in)

    def row_out(slot, r):
        return pltpu.make_async_copy(out_stage.at[pl.ds(pl.multiple_of(r * ROW_PITCH, 8), ROW_PITCH)],
                                     pairs_ref.at[pl.ds(pl.multiple_of(slot * ROW_PITCH, 8), ROW_PITCH)], sem_out)

    def gather_start(item, buf):
        for_rows(item, buf, lambda v, r: row_in(v & (n_tokens - 1), r).start())

    def scatter_start(item, buf):
        for_rows(item, buf, lambda v, r: row_out(v, r).start())

    def wait_rows(n, copy):
        def body(r, carry):
            copy(0, r).wait()
            return carry

        lax.fori_loop(0, n, body, 0)

    @pl.when((w == 0) & (s == 0))
    def _():
        x_stage[...] = jnp.zeros(x_stage.shape, x_stage.dtype)
        out_stage[...] = jnp.zeros(out_stage.shape, out_stage.dtype)

        @pl.when(cnt > 0)
        def _():
            first_window = idx_copy(0, 0)
            first_window.start()
            first_window.wait()
            gather_start(0, 0)

    @pl.when((s == 0) & (cnt > 0))
    def _():
        wait_rows(cnt, row_in)

    @pl.when((s == 0) & has_next)
    def _():
        idx_copy(nxt_item, nxt).start()

    @pl.when((s == MOE_UP_STEPS) & (w > 0) & (cnt > 0))
    def _():
        wait_rows(item_cnt[jnp.maximum(w - 1, 0)], row_out)

    def compute(m):
        @pl.when(s == 0)
        def _():
            for c in range(ROW_CHUNKS):
                x_bf[:m, c * LANES:(c + 1) * LANES] = x_stage[pl.ds(c, m, stride=ROW_PITCH), :].astype(BF16)

        @pl.when(s < MOE_UP_STEPS)
        def _():
            x = x_bf[:m, :]
            g = jnp.dot(x, wg_ref[...].astype(BF16), preferred_element_type=F32)
            u = jnp.dot(x, wu_ref[...].astype(BF16), preferred_element_type=F32)
            hid[s, :m, :] = ((g * jax.nn.sigmoid(g)) * u).astype(BF16)

        @pl.when(s >= MOE_UP_STEPS)
        def _():
            y = jnp.zeros((m, MOE_DOWN_TILE), F32)
            for c in range(MOE_UP_STEPS):
                y = y + jnp.dot(hid[c, :m, :], wd_ref[c * MOE_CHUNK:(c + 1) * MOE_CHUNK, :].astype(BF16),
                                preferred_element_type=F32)
            base = (s - MOE_UP_STEPS) * (MOE_DOWN_TILE // LANES)
            for j in range(MOE_DOWN_TILE // LANES):
                out_stage[pl.ds(base + j, m, stride=ROW_PITCH), :] = y[:, j * LANES:(j + 1) * LANES]

    for k in range(1, MOE_ROWS // MOE_SUB + 1):
        pl.when(nsub == k)(functools.partial(compute, k * MOE_SUB))

    @pl.when((s == 1) & has_next)
    def _():
        idx_copy(nxt_item, nxt).wait()
        gather_start(nxt_item, nxt)

    @pl.when((s == MOE_STEPS - 1) & (cnt > 0))
    def _():
        scatter_start(w, cur)

        @pl.when(jnp.logical_not(has_next))
        def _():
            wait_rows(cnt, row_out)


def _moe_experts(hn_lin, w_gate, w_up, w_down, item_e, item_cnt, item_row0, dstk, n_tokens):
    n_items = item_e.shape[0]
    d = D_MODEL
    assert n_tokens & (n_tokens - 1) == 0, "token ids are unpacked from k * n_tokens + token with a mask"

    def up_chunk(s, cnt_ref, w):
        return jnp.where(cnt_ref[w] > 0, jnp.minimum(s, MOE_UP_STEPS - 1), MOE_UP_STEPS - 1)

    def down_tile(s, cnt_ref, w):
        return jnp.where(cnt_ref[w] > 0, jnp.maximum(s - MOE_UP_STEPS, 0), MOE_UP_STEPS - 1)

    return pl.pallas_call(
        functools.partial(_moe_kernel, n_tokens=n_tokens),
        grid_spec=pltpu.PrefetchScalarGridSpec(
            num_scalar_prefetch=3,
            grid=(n_items, MOE_STEPS),
            in_specs=[pl.BlockSpec(memory_space=pl.ANY),
                      pl.BlockSpec(memory_space=pl.ANY),
                      pl.BlockSpec((None, d, MOE_CHUNK), lambda w, s, e, cn, r0: (e[w], 0, up_chunk(s, cn, w))),
                      pl.BlockSpec((None, d, MOE_CHUNK), lambda w, s, e, cn, r0: (e[w], 0, up_chunk(s, cn, w))),
                      pl.BlockSpec((None, D_EXPERT, MOE_DOWN_TILE),
                                   lambda w, s, e, cn, r0: (e[w], 0, down_tile(s, cn, w)))],
            out_specs=pl.BlockSpec(memory_space=pl.ANY),
            scratch_shapes=[pltpu.SMEM((2, IDX_WINDOW_ROWS, LANES), jnp.int32),
                            pltpu.VMEM((MOE_ROWS * ROW_PITCH, LANES), F32),
                            pltpu.VMEM((MOE_ROWS, d), BF16),
                            pltpu.VMEM((MOE_UP_STEPS, MOE_ROWS, MOE_CHUNK), BF16),
                            pltpu.VMEM((MOE_ROWS * ROW_PITCH, LANES), F32),
                            pltpu.SemaphoreType.DMA(()),
                            pltpu.SemaphoreType.DMA(()),
                            pltpu.SemaphoreType.DMA(())]),
        out_shape=jax.ShapeDtypeStruct((TOP_K * n_tokens * ROW_PITCH, LANES), F32),
        compiler_params=_cparams(("arbitrary", "arbitrary")),
        name="moe_experts",
    )(item_e, item_cnt, item_row0, dstk, hn_lin, w_gate, w_up, w_down)


def _combine_kernel(h_ref, p0_ref, p1_ref, gate_ref, g_ref, o_ref):
    tm, d = h_ref.shape
    g0 = gate_ref[:, 0:1]
    g1 = gate_ref[:, 1:2]
    ss = jnp.zeros((tm, 1), F32)
    for c in range(ROW_CHUNKS):
        sl = slice(c * LANES, (c + 1) * LANES)
        a = p0_ref[pl.ds(c, tm, stride=ROW_PITCH), :]
        b = p1_ref[pl.ds(c, tm, stride=ROW_PITCH), :]
        y = h_ref[:, sl] + (a * g0 + b * g1)
        o_ref[:, sl] = y
        ss = ss + jnp.sum(y * y, axis=-1, keepdims=True)
    inv = lax.rsqrt(ss / d + EPS)
    o_ref[...] = o_ref[...] * inv * g_ref[...]


def _combine(h, pairs, gates, g_final, *, tm=256):
    t, d = h.shape
    k1 = t // tm
    return pl.pallas_call(
        _combine_kernel,
        grid=(t // tm,),
        in_specs=[pl.BlockSpec((tm, d), lambda i: (i, 0)),
                  pl.BlockSpec((tm * ROW_PITCH, LANES), lambda i: (i, 0)),
                  pl.BlockSpec((tm * ROW_PITCH, LANES), lambda i: (k1 + i, 0)),
                  pl.BlockSpec((tm, TOP_K), lambda i: (i, 0)),
                  pl.BlockSpec((1, d), lambda i: (0, 0))],
        out_specs=pl.BlockSpec((tm, d), lambda i: (i, 0)),
        out_shape=jax.ShapeDtypeStruct((t, d), F32),
        compiler_params=_cparams(("parallel",)),
        name="combine_final_norm",
    )(h, pairs, pairs, gates, g_final.reshape(1, d))


def _route(logits, t):
    group_prob = jax.nn.softmax(logits[:, :N_GROUPS], axis=-1)
    g_sel = jnp.argmax(group_prob, axis=-1)
    g_p = jnp.max(group_prob, axis=-1, keepdims=True)
    e_logits = logits[:, N_GROUPS:N_GROUPS + N_EXPERTS]
    in_group = (jnp.arange(N_EXPERTS) // EXPERTS_PER_GROUP)[None, :] == g_sel[:, None]
    e_prob = jax.nn.softmax(jnp.where(in_group, e_logits, -jnp.inf), axis=-1)
    top_p, expert_idx = lax.top_k(jnp.where(in_group, e_prob, -1.0), TOP_K)
    gates = g_p * top_p / jnp.sum(top_p, axis=-1, keepdims=True)

    tk = t * TOP_K
    n_items_max = tk // MOE_ROWS + N_EXPERTS
    flat_e = expert_idx.reshape(tk).astype(jnp.int32)
    order = jnp.argsort(flat_e).astype(jnp.int32)
    dstk = (order % TOP_K) * t + order // TOP_K
    dstk = jnp.concatenate([dstk, jnp.zeros((2 * IDX_TILE,), jnp.int32)]).reshape(-1, LANES)
    counts = jnp.sum((flat_e[:, None] == jnp.arange(N_EXPERTS, dtype=jnp.int32)[None, :]).astype(jnp.int32), axis=0)
    start = jnp.cumsum(counts) - counts
    items_of = (counts + MOE_ROWS - 1) // MOE_ROWS
    item_end = jnp.cumsum(items_of)
    wi = jnp.arange(n_items_max, dtype=jnp.int32)
    e_w = jnp.minimum(jnp.sum((item_end[None, :] <= wi[:, None]).astype(jnp.int32), axis=1), N_EXPERTS - 1)
    part = wi - (item_end[e_w] - items_of[e_w])
    active = wi < item_end[-1]
    item_cnt = jnp.where(active, jnp.clip(counts[e_w] - part * MOE_ROWS, 0, MOE_ROWS), 0).astype(jnp.int32)
    item_row0 = jnp.where(active, start[e_w] + part * MOE_ROWS, 0).astype(jnp.int32)
    item_e = jnp.where(active, e_w, e_w[jnp.maximum(item_end[-1] - 1, 0)]).astype(jnp.int32)
    return gates.astype(F32), item_e, item_cnt, item_row0, dstk


def _prep_in_weight(w_in):
    d = w_in.shape[0]
    off_q = MLA_Q_RANK
    off_kv = off_q + MLA_KV_RANK
    off_kpe = off_kv + MLA_ROPE_DIM
    off_sq = off_kpe + SWA_HEADS * SWA_HEAD_DIM
    off_sk = off_sq + SWA_KV_HEADS * SWA_HEAD_DIM
    sk = w_in[:, off_sq:off_sk].reshape(d, SWA_KV_HEADS, SWA_HEAD_DIM)
    sv = w_in[:, off_sk:].reshape(d, SWA_KV_HEADS, SWA_HEAD_DIM)
    skv = jnp.concatenate([sk, sv], axis=2).reshape(d, SWA_KV_HEADS * LANES)
    kpe = jnp.concatenate([w_in[:, off_kv:off_kpe], jnp.zeros((d, LANES - MLA_ROPE_DIM), w_in.dtype)], axis=1)
    return jnp.concatenate([w_in[:, :off_kv], w_in[:, off_kpe:off_sq], skv, kpe], axis=1).astype(BF16)


def _prep_q_weight(w_q_up):
    r = w_q_up.shape[0]
    w = w_q_up.reshape(r, MLA_HEADS, MLA_QK_DIM)
    w = jnp.pad(w, ((0, 0), (0, 0), (0, MLA_HEAD_PAD - MLA_QK_DIM)))
    return w.reshape(r, MLA_HEADS * MLA_HEAD_PAD).astype(BF16)


def kernel(x, positions, g_attn_norm, w_in, g_q_norm, w_q_up, g_kv_norm, w_kv_up, sinks, w_out,
           g_ffn_norm, w_router_group, b_router_group, w_router_expert, b_router_expert,
           w_expert_gate, w_expert_up, w_expert_down, g_final_norm):
    batch, seq, d = x.shape
    t = batch * seq
    assert g_attn_norm.shape[0] == 1, "the final rmsnorm is fused into the (single) layer's combine"
    h = x.reshape(t, d)
    cos, sin = _rope_tables(positions)
    for l in range(g_attn_norm.shape[0]):
        z_kinds = ([ROPE_NONE] * (Z_OFF_SQ // LANES) + [ROPE_FULL] * ((Z_OFF_SKV - Z_OFF_SQ) // LANES)
                   + [ROPE_HALF] * ((Z_WIDTH - Z_OFF_SKV) // LANES))
        z_scales = [SWA_HEAD_DIM ** -0.5 if k == ROPE_FULL else 1.0 for k in z_kinds]
        z = _matmul_rope(_rmsnorm(h, g_attn_norm[l]), _prep_in_weight(w_in[l]), cos, sin,
                         tm=512, tn=Z_WIDTH // 3, kinds=z_kinds, scales=z_scales, name="in_proj")
        q_kinds = [ROPE_NONE, ROPE_HALF] * MLA_HEADS
        q = _norm_matmul(z, Z_OFF_CQ // MLA_Q_RANK, MLA_Q_RANK, g_q_norm[l], _prep_q_weight(w_q_up[l]), cos, sin,
                         tm=1024, tn=1024, kinds=q_kinds, scales=[MLA_QK_DIM ** -0.5] * len(q_kinds), name="q_up")
        kv_kinds = [ROPE_NONE] * (w_kv_up.shape[2] // LANES)
        kv = _norm_matmul(z, Z_OFF_CKV // MLA_KV_RANK, MLA_KV_RANK, g_kv_norm[l], w_kv_up[l].astype(BF16), cos, sin,
                          tm=1024, tn=1024, kinds=kv_kinds, scales=[1.0] * len(kv_kinds), name="kv_up")
        mla_o = _mla_attention(q, kv, z, batch, seq)
        swa_o = _swa_attention(z, sinks[l], batch, seq)
        w_r = jnp.concatenate([w_router_group[l], w_router_expert[l],
                               jnp.zeros((d, LANES - N_GROUPS - N_EXPERTS), F32)], axis=1)
        b_r = jnp.concatenate([b_router_group[l], b_router_expert[l],
                               jnp.zeros((LANES - N_GROUPS - N_EXPERTS,), F32)]).reshape(1, LANES)
        h, logits, inv = _out_proj(h, mla_o, swa_o, w_out[l].astype(BF16), g_ffn_norm[l], w_r, b_r)
        gates, item_e, item_cnt, item_row0, dstk = _route(logits, t)
        hn = _ffn_norm(h, inv, g_ffn_norm[l])
        pairs = _moe_experts(hn, w_expert_gate[l], w_expert_up[l], w_expert_down[l],
                             item_e, item_cnt, item_row0, dstk, t)
        out = _combine(h, pairs, gates, g_final_norm)
    return out.reshape(batch, seq, d)
```

```python
import functools

import jax
import jax.numpy as jnp
from jax import lax
from jax.experimental import pallas as pl
from jax.experimental.pallas import tpu as pltpu

F32 = jnp.float32
BF16 = jnp.bfloat16

D_MODEL = 4096
EPS = 1e-6
ROPE_THETA = 10000.0
NEG = -1e30
MLA_HEADS = 16
MLA_Q_RANK = 1024
MLA_KV_RANK = 512
MLA_NOPE_DIM = 128
MLA_ROPE_DIM = 64
MLA_V_DIM = 128
MLA_QK_DIM = MLA_NOPE_DIM + MLA_ROPE_DIM
MLA_HEAD_PAD = 256
SWA_HEADS = 32
SWA_KV_HEADS = 4
SWA_GROUP = SWA_HEADS // SWA_KV_HEADS
SWA_HEAD_DIM = 64
SWA_WINDOW = 128
N_GROUPS = 8
EXPERTS_PER_GROUP = 8
N_EXPERTS = N_GROUPS * EXPERTS_PER_GROUP
TOP_K = 2
D_EXPERT = 1024

LANES = 128
Z_OFF_CQ = 0
Z_OFF_CKV = MLA_Q_RANK
Z_OFF_SQ = Z_OFF_CKV + MLA_KV_RANK
Z_OFF_SKV = Z_OFF_SQ + SWA_HEADS * SWA_HEAD_DIM
Z_OFF_KPE = Z_OFF_SKV + SWA_KV_HEADS * LANES
Z_WIDTH = Z_OFF_KPE + LANES
ROPE_NONE, ROPE_FULL, ROPE_HALF = 0, 1, 2

MOE_ROWS = 512
MOE_SUB = 128
MOE_CHUNK = 256
VMEM_LIMIT = 56 * 1024 * 1024


def _cparams(sem, vmem=VMEM_LIMIT):
    return pltpu.CompilerParams(dimension_semantics=sem, vmem_limit_bytes=vmem)


def _swap32(x):
    lane = lax.broadcasted_iota(jnp.int32, x.shape, 1)
    fwd = pltpu.roll(x, LANES - 32, 1)
    bwd = pltpu.roll(x, 32, 1)
    return jnp.where((lane % 64) < 32, fwd, bwd)


def _rope_tile(x, cos, sin, kind):
    if kind == ROPE_NONE:
        return x
    if kind == ROPE_HALF:
        lane = lax.broadcasted_iota(jnp.int32, x.shape, 1)
        cos = jnp.where(lane < 64, cos, 1.0)
        sin = jnp.where(lane < 64, sin, 0.0)
    return x * cos + _swap32(x) * sin


def _rope_table_kernel(pos_ref, invf_ref, sign_ref, cos_ref, sin_ref):
    ang = pos_ref[...].astype(F32) * invf_ref[...]
    cos_ref[...] = jnp.cos(ang)
    sin_ref[...] = jnp.sin(ang) * sign_ref[...]


def _rope_tables(positions):
    t = positions.size
    pos = positions.reshape(t, 1)
    half = SWA_HEAD_DIM // 2
    inv_freq = ROPE_THETA ** (-jnp.arange(0, SWA_HEAD_DIM, 2, dtype=F32) / SWA_HEAD_DIM)
    lane = jnp.arange(LANES)
    invf = inv_freq[lane % half].reshape(1, LANES)
    sign = jnp.where((lane % SWA_HEAD_DIM) < half, -1.0, 1.0).astype(F32).reshape(1, LANES)
    tm = 1024
    return pl.pallas_call(
        _rope_table_kernel,
        grid=(t // tm,),
        in_specs=[pl.BlockSpec((tm, 1), lambda i: (i, 0)),
                  pl.BlockSpec((1, LANES), lambda i: (0, 0)),
                  pl.BlockSpec((1, LANES), lambda i: (0, 0))],
        out_specs=[pl.BlockSpec((tm, LANES), lambda i: (i, 0))] * 2,
        out_shape=[jax.ShapeDtypeStruct((t, LANES), F32)] * 2,
        compiler_params=_cparams(("parallel",)),
        name="rope_tables",
    )(pos, invf, sign)


def _rmsnorm_kernel(x_ref, g_ref, o_ref):
    xf = x_ref[...].astype(F32)
    ms = jnp.mean(xf * xf, axis=-1, keepdims=True)
    o_ref[...] = (xf * lax.rsqrt(ms + EPS) * g_ref[...]).astype(o_ref.dtype)


def _rmsnorm(x, g, *, tm=512):
    t, d = x.shape
    return pl.pallas_call(
        _rmsnorm_kernel,
        grid=(t // tm,),
        in_specs=[pl.BlockSpec((tm, d), lambda i: (i, 0)), pl.BlockSpec((1, d), lambda i: (0, 0))],
        out_specs=pl.BlockSpec((tm, d), lambda i: (i, 0)),
        out_shape=jax.ShapeDtypeStruct((t, d), BF16),
        compiler_params=_cparams(("parallel",)),
        name="attn_norm",
    )(x, g.reshape(1, d))


def _rope_epilogue(acc, cos_ref, sin_ref, o_ref, j, kinds, scales):
    def epilogue(tile_kinds, tile_scales):
        needs_rope = any(k != ROPE_NONE for k in tile_kinds)
        cos = cos_ref[...] if needs_rope else None
        sin = sin_ref[...] if needs_rope else None
        for c, (kind, scale) in enumerate(zip(tile_kinds, tile_scales)):
            blk = _rope_tile(acc[:, c * LANES:(c + 1) * LANES], cos, sin, kind)
            if scale != 1.0:
                blk = blk * scale
            o_ref[:, c * LANES:(c + 1) * LANES] = blk.astype(o_ref.dtype)

    if all(k == kinds[0] for k in kinds) and all(s == scales[0] for s in scales):
        epilogue(kinds[0], scales[0])
    else:
        for jj in range(len(kinds)):
            pl.when(j == jj)(functools.partial(epilogue, kinds[jj], scales[jj]))


def _per_tile(values, n_j, per):
    return tuple(tuple(values[jj * per:(jj + 1) * per]) for jj in range(n_j))


def _matmul_rope_kernel(x_ref, w_ref, cos_ref, sin_ref, o_ref, *, kinds, scales):
    acc = jnp.dot(x_ref[...], w_ref[...], preferred_element_type=F32)
    _rope_epilogue(acc, cos_ref, sin_ref, o_ref, pl.program_id(0), kinds, scales)


def _matmul_rope(x, w, cos, sin, *, tm, tn, kinds, scales, name):
    t, k = x.shape
    n = w.shape[1]
    n_j = n // tn
    return pl.pallas_call(
        functools.partial(_matmul_rope_kernel, kinds=_per_tile(kinds, n_j, tn // LANES),
                          scales=_per_tile(scales, n_j, tn // LANES)),
        grid=(n_j, t // tm),
        in_specs=[pl.BlockSpec((tm, k), lambda j, i: (i, 0)),
                  pl.BlockSpec((k, tn), lambda j, i: (0, j)),
                  pl.BlockSpec((tm, LANES), lambda j, i: (i, 0)),
                  pl.BlockSpec((tm, LANES), lambda j, i: (i, 0))],
        out_specs=pl.BlockSpec((tm, tn), lambda j, i: (i, j)),
        out_shape=jax.ShapeDtypeStruct((t, n), BF16),
        compiler_params=_cparams(("parallel", "parallel")),
        name=name,
    )(x, w, cos, sin)


def _norm_matmul_kernel(x_ref, g_ref, w_ref, cos_ref, sin_ref, o_ref, n_scr, *, kinds, scales):
    j = pl.program_id(1)

    @pl.when(j == 0)
    def _():
        xf = x_ref[...].astype(F32)
        ms = jnp.mean(xf * xf, axis=-1, keepdims=True)
        n_scr[...] = (xf * lax.rsqrt(ms + EPS) * g_ref[...]).astype(BF16)

    acc = jnp.dot(n_scr[...], w_ref[...].astype(BF16), preferred_element_type=F32)
    _rope_epilogue(acc, cos_ref, sin_ref, o_ref, j, kinds, scales)


def _norm_matmul(x, x_col_block, k, g, w, cos, sin, *, tm, tn, kinds, scales, name):
    t = x.shape[0]
    n = w.shape[1]
    n_j = n // tn
    kinds = _per_tile(kinds, n_j, tn // LANES)
    scales = _per_tile(scales, n_j, tn // LANES)
    return pl.pallas_call(
        functools.partial(_norm_matmul_kernel, kinds=kinds, scales=scales),
        grid=(t // tm, n_j),
        in_specs=[pl.BlockSpec((tm, k), lambda i, j: (i, x_col_block)),
                  pl.BlockSpec((1, k), lambda i, j: (0, 0)),
                  pl.BlockSpec((k, tn), lambda i, j: (0, j)),
                  pl.BlockSpec((tm, LANES), lambda i, j: (i, 0)),
                  pl.BlockSpec((tm, LANES), lambda i, j: (i, 0))],
        out_specs=pl.BlockSpec((tm, tn), lambda i, j: (i, j)),
        out_shape=jax.ShapeDtypeStruct((t, n), BF16),
        scratch_shapes=[pltpu.VMEM((tm, k), BF16)],
        compiler_params=_cparams(("parallel", "arbitrary")),
        name=name,
    )(x, g.reshape(1, k), w, cos, sin)


def _mla_kernel(q_ref, kv_ref, kpe_ref, o_ref, kcat_scr, *, tq):
    s_len = q_ref.shape[0]
    kcat_scr[:, :MLA_NOPE_DIM] = kv_ref[:, :MLA_NOPE_DIM]
    kcat_scr[:, MLA_NOPE_DIM:] = kpe_ref[...]
    row = lax.broadcasted_iota(jnp.int32, (tq, tq), 0)
    col = lax.broadcasted_iota(jnp.int32, (tq, tq), 1)
    tri = col <= row
    nt = (((1,), (1,)), ((), ()))
    for qi in range(s_len // tq):
        lo, hi = qi * tq, (qi + 1) * tq
        q = q_ref[lo:hi, :]
        s_diag = lax.dot_general(q, kcat_scr[lo:hi, :], nt, preferred_element_type=F32)
        s_diag = jnp.where(tri, s_diag, NEG)
        m = jnp.max(s_diag, axis=-1, keepdims=True)
        if qi > 0:
            s_past = lax.dot_general(q, kcat_scr[:lo, :], nt, preferred_element_type=F32)
            m = jnp.maximum(m, jnp.max(s_past, axis=-1, keepdims=True))
            p_past = jnp.exp(s_past - m)
        p_diag = jnp.exp(s_diag - m)
        l = jnp.sum(p_diag, axis=-1, keepdims=True)
        o = jnp.dot(p_diag.astype(BF16), kv_ref[lo:hi, MLA_NOPE_DIM:], preferred_element_type=F32)
        if qi > 0:
            l = l + jnp.sum(p_past, axis=-1, keepdims=True)
            o = o + jnp.dot(p_past.astype(BF16), kv_ref[:lo, MLA_NOPE_DIM:], preferred_element_type=F32)
        o_ref[lo:hi, :] = (o / l).astype(o_ref.dtype)


def _mla_attention(q, kv, z, batch, seq):
    t = batch * seq
    kpe_block = Z_OFF_KPE // LANES
    return pl.pallas_call(
        functools.partial(_mla_kernel, tq=256),
        grid=(batch, MLA_HEADS),
        in_specs=[pl.BlockSpec((seq, MLA_HEAD_PAD), lambda b, h: (b, h)),
                  pl.BlockSpec((seq, MLA_NOPE_DIM + MLA_V_DIM), lambda b, h: (b, h)),
                  pl.BlockSpec((seq, LANES), lambda b, h: (b, kpe_block))],
        out_specs=pl.BlockSpec((seq, MLA_V_DIM), lambda b, h: (b, h)),
        out_shape=jax.ShapeDtypeStruct((t, MLA_HEADS * MLA_V_DIM), BF16),
        scratch_shapes=[pltpu.VMEM((seq, MLA_HEAD_PAD), BF16)],
        compiler_params=_cparams(("parallel", "arbitrary")),
        name="mla_attention",
    )(q, kv, z)


def _swa_kernel(sinks_ref, q_ref, kv_ref, o_ref, klo, khi, vlo, vhi):
    h = pl.program_id(1)
    s_len = q_ref.shape[0]
    w = SWA_WINDOW
    pairs = SWA_GROUP // 2
    kv = kv_ref[...].astype(F32)
    lane = lax.broadcasted_iota(jnp.int32, kv.shape, 1)
    low = lane < SWA_HEAD_DIM
    rolled = pltpu.roll(kv, SWA_HEAD_DIM, 1)
    pad = jnp.zeros((w, LANES), BF16)
    for scr, val in ((klo, jnp.where(low, kv, 0.0)), (khi, jnp.where(low, 0.0, rolled)),
                     (vlo, jnp.where(low, rolled, 0.0)), (vhi, jnp.where(low, 0.0, kv))):
        scr[:w, :] = pad
        scr[w:, :] = val.astype(BF16)

    rows = pairs * w
    r = lax.broadcasted_iota(jnp.int32, (rows, 2 * w), 0) % w
    c = lax.broadcasted_iota(jnp.int32, (rows, 2 * w), 1)
    band = (c > r) & (c <= r + w)
    pair_of_row = lax.broadcasted_iota(jnp.int32, (rows, 1), 0) // w
    out_low = lax.broadcasted_iota(jnp.int32, (rows, LANES), 1) < SWA_HEAD_DIM

    def sink_column(parity):
        col = jnp.zeros((rows, 1), F32)
        for j in range(pairs):
            col = jnp.where(pair_of_row == j, sinks_ref[h * SWA_GROUP + 2 * j + parity], col)
        return col

    sink_even, sink_odd = sink_column(0), sink_column(1)
    nt = (((1,), (1,)), ((), ()))

    def block(n, carry):
        r0 = pl.multiple_of(n * w, w)
        qs = jnp.concatenate([q_ref[pl.ds(r0, w), j * LANES:(j + 1) * LANES] for j in range(pairs)], axis=0)
        mask = band & (c >= jnp.where(n > 0, 0, w))

        def head(k_scr, v_scr, sink):
            s = lax.dot_general(qs, k_scr[pl.ds(r0, 2 * w), :], nt, preferred_element_type=F32)
            s = jnp.where(mask, s, NEG)
            m = jnp.maximum(jnp.max(s, axis=-1, keepdims=True), sink)
            p = jnp.exp(s - m)
            l = jnp.sum(p, axis=-1, keepdims=True) + jnp.exp(sink - m)
            o = jnp.dot(p.astype(BF16), v_scr[pl.ds(r0, 2 * w), :], preferred_element_type=F32)
            return o, l

        o_even, l_even = head(klo, vlo, sink_even)
        o_odd, l_odd = head(khi, vhi, sink_odd)
        o = (o_even + o_odd) / jnp.where(out_low, l_even, l_odd)
        for j in range(pairs):
            o_ref[pl.ds(r0, w), j * LANES:(j + 1) * LANES] = o[j * w:(j + 1) * w].astype(o_ref.dtype)
        return carry

    lax.fori_loop(0, s_len // w, block, 0)


def _swa_attention(z, sinks, batch, seq):
    t = batch * seq
    gw = SWA_GROUP * SWA_HEAD_DIM
    q_block0 = Z_OFF_SQ // gw
    kv_block0 = Z_OFF_SKV // LANES
    return pl.pallas_call(
        _swa_kernel,
        grid_spec=pltpu.PrefetchScalarGridSpec(
            num_scalar_prefetch=1,
            grid=(batch, SWA_KV_HEADS),
            in_specs=[pl.BlockSpec((seq, gw), lambda b, h, s: (b, q_block0 + h)),
                      pl.BlockSpec((seq, LANES), lambda b, h, s: (b, kv_block0 + h))],
            out_specs=pl.BlockSpec((seq, gw), lambda b, h, s: (b, h)),
            scratch_shapes=[pltpu.VMEM((seq + SWA_WINDOW, LANES), BF16)] * 4),
        out_shape=jax.ShapeDtypeStruct((t, SWA_HEADS * SWA_HEAD_DIM), BF16),
        compiler_params=_cparams(("parallel", "arbitrary")),
        name="swa_attention",
    )(sinks.astype(F32), z, z)


def _out_proj_kernel(x_ref, a_ref, s_ref, wt_ref, wb_ref, h_ref):
    h_ref[...] = (x_ref[...] + jnp.dot(a_ref[...], wt_ref[...].astype(BF16), preferred_element_type=F32)
                  + jnp.dot(s_ref[...], wb_ref[...].astype(BF16), preferred_element_type=F32))


def _out_proj(x2, mla_o, swa_o, w_out, *, tm=1024, tn=512):
    t, d = x2.shape
    half = mla_o.shape[1]
    return pl.pallas_call(
        _out_proj_kernel,
        grid=(t // tm, d // tn),
        in_specs=[pl.BlockSpec((tm, tn), lambda i, j: (i, j)),
                  pl.BlockSpec((tm, half), lambda i, j: (i, 0)),
                  pl.BlockSpec((tm, half), lambda i, j: (i, 0)),
                  pl.BlockSpec((half, tn), lambda i, j: (0, j)),
                  pl.BlockSpec((half, tn), lambda i, j: (1, j))],
        out_specs=pl.BlockSpec((tm, tn), lambda i, j: (i, j)),
        out_shape=jax.ShapeDtypeStruct((t, d), F32),
        compiler_params=_cparams(("parallel", "parallel")),
        name="out_proj",
    )(x2, mla_o, swa_o, w_out, w_out)


ROW_CHUNKS = D_MODEL // LANES
ROW_PITCH = 40


def _split_bf16(v):
    hi = v.astype(BF16)
    return hi, (v - hi.astype(F32)).astype(BF16)


def _ffn_norm_router_kernel(h_ref, g_ref, wr_ref, br_ref, o_ref, lg_ref):
    tm = h_ref.shape[0]
    h = h_ref[...]
    inv = lax.rsqrt(jnp.mean(h * h, axis=-1, keepdims=True) + EPS)
    hg_hi, hg_lo = _split_bf16(h * g_ref[...])
    wr_hi, wr_lo = _split_bf16(wr_ref[...])
    acc = (jnp.dot(hg_hi, wr_hi, preferred_element_type=F32)
           + (jnp.dot(hg_hi, wr_lo, preferred_element_type=F32)
              + jnp.dot(hg_lo, wr_hi, preferred_element_type=F32)))
    lg_ref[...] = acc * inv + br_ref[...]
    for c in range(ROW_CHUNKS):
        sl = slice(c * LANES, (c + 1) * LANES)
        o_ref[pl.ds(c, tm, stride=ROW_PITCH), :] = h_ref[:, sl] * inv * g_ref[:, sl]
    for c in range(ROW_CHUNKS, ROW_PITCH):
        o_ref[pl.ds(c, tm, stride=ROW_PITCH), :] = jnp.zeros((tm, LANES), o_ref.dtype)


def _ffn_norm_router(h, g, w_r, b_r, *, tm=256):
    t, d = h.shape
    return pl.pallas_call(
        _ffn_norm_router_kernel,
        grid=(t // tm,),
        in_specs=[pl.BlockSpec((tm, d), lambda i: (i, 0)),
                  pl.BlockSpec((1, d), lambda i: (0, 0)),
                  pl.BlockSpec((d, LANES), lambda i: (0, 0)),
                  pl.BlockSpec((1, LANES), lambda i: (0, 0))],
        out_specs=[pl.BlockSpec((tm * ROW_PITCH, LANES), lambda i: (i, 0)),
                   pl.BlockSpec((tm, LANES), lambda i: (i, 0))],
        out_shape=[jax.ShapeDtypeStruct((t * ROW_PITCH, LANES), F32),
                   jax.ShapeDtypeStruct((t, LANES), F32)],
        compiler_params=_cparams(("parallel",)),
        name="ffn_norm_router",
    )(h, g.reshape(1, d), w_r, b_r)


MOE_UP_STEPS = D_EXPERT // MOE_CHUNK
MOE_STEPS = 2 * MOE_UP_STEPS
MOE_DOWN_TILE = D_MODEL // MOE_UP_STEPS


LANES_LOG2 = LANES.bit_length() - 1
IDX_TILE_ROWS = 8
IDX_TILE = IDX_TILE_ROWS * LANES
IDX_TILE_LOG2 = IDX_TILE.bit_length() - 1
IDX_WINDOW_ROWS = 2 * IDX_TILE_ROWS
assert MOE_ROWS <= IDX_TILE
ROW_UNROLL_LOG2 = 3
ROW_UNROLL = 1 << ROW_UNROLL_LOG2


def _moe_kernel(item_e, item_cnt, item_row0, dstk_ref, hn_ref, wg_ref, wu_ref, wd_ref, pairs_ref,
                idx_win, x_stage, x_bf, hid, out_stage, sem_idx, sem_in, sem_out, *, n_tokens):
    del item_e
    w = pl.program_id(0)
    s = pl.program_id(1)
    n_items = pl.num_programs(0)
    cnt = item_cnt[w]
    nsub = (cnt + MOE_SUB - 1) // MOE_SUB
    nxt_item = jnp.minimum(w + 1, n_items - 1)
    has_next = (w + 1 < n_items) & (item_cnt[nxt_item] > 0)
    cur = w & 1
    nxt = 1 - cur

    def idx_copy(item, buf):
        first = (item_row0[item] >> IDX_TILE_LOG2) * IDX_TILE_ROWS
        return pltpu.make_async_copy(dstk_ref.at[pl.ds(pl.multiple_of(first, IDX_TILE_ROWS), IDX_WINDOW_ROWS)],
                                     idx_win.at[buf], sem_idx)

    def unrolled_rows(n, one_row):
        groups = n >> ROW_UNROLL_LOG2

        def group(g, carry):
            for u in range(ROW_UNROLL):
                one_row(g * ROW_UNROLL + u)
            return carry

        def single(r, carry):
            one_row(r)
            return carry

        lax.fori_loop(0, groups, group, 0)
        lax.fori_loop(groups * ROW_UNROLL, n, single, 0)

    def for_rows(item, buf, fn):
        base = item_row0[item] & (IDX_TILE - 1)

        def one_row(r):
            off = base + r
            fn(idx_win[buf, off >> LANES_LOG2, off & (LANES - 1)], r)

        unrolled_rows(item_cnt[item], one_row)

    def row_in(tok, r):
        return pltpu.make_async_copy(hn_ref.at[pl.ds(pl.multiple_of(tok * ROW_PITCH, 8), ROW_PITCH)],
                                     x_stage.at[pl.ds(pl.multiple_of(r * ROW_PITCH, 8), ROW_PITCH)], sem_in)

    def row_out(slot, r):
        return pltpu.make_async_copy(out_stage.at[pl.ds(pl.multiple_of(r * ROW_PITCH, 8), ROW_PITCH)],
                                     pairs_ref.at[pl.ds(pl.multiple_of(slot * ROW_PITCH, 8), ROW_PITCH)], sem_out)

    def gather_start(item, buf):
        for_rows(item, buf, lambda v, r: row_in(v & (n_tokens - 1), r).start())

    def scatter_start(item, buf):
        for_rows(item, buf, lambda v, r: row_out(v, r).start())

    def wait_rows(n, copy):
        unrolled_rows(n, lambda r: copy(0, r).wait())

    @pl.when((w == 0) & (s == 0))
    def _():
        x_stage[...] = jnp.zeros(x_stage.shape, x_stage.dtype)
        out_stage[...] = jnp.zeros(out_stage.shape, out_stage.dtype)

        @pl.when(cnt > 0)
        def _():
            first_window = idx_copy(0, 0)
            first_window.start()
            first_window.wait()
            gather_start(0, 0)

    @pl.when((s == 0) & (cnt > 0))
    def _():
        wait_rows(cnt, row_in)

    @pl.when((s == 0) & has_next)
    def _():
        idx_copy(nxt_item, nxt).start()

    @pl.when((s == MOE_UP_STEPS) & (w > 0) & (cnt > 0))
    def _():
        wait_rows(item_cnt[jnp.maximum(w - 1, 0)], row_out)

    def compute(m):
        @pl.when(s == 0)
        def _():
            for c in range(ROW_CHUNKS):
                x_bf[:m, c * LANES:(c + 1) * LANES] = x_stage[pl.ds(c, m, stride=ROW_PITCH), :].astype(BF16)

        @pl.when(s < MOE_UP_STEPS)
        def _():
            x = x_bf[:m, :]
            g = jnp.dot(x, wg_ref[...].astype(BF16), preferred_element_type=F32)
            u = jnp.dot(x, wu_ref[...].astype(BF16), preferred_element_type=F32)
            hid[s, :m, :] = ((g * jax.nn.sigmoid(g)) * u).astype(BF16)

        @pl.when(s >= MOE_UP_STEPS)
        def _():
            y = jnp.zeros((m, MOE_DOWN_TILE), F32)
            for c in range(MOE_UP_STEPS):
                y = y + jnp.dot(hid[c, :m, :], wd_ref[c * MOE_CHUNK:(c + 1) * MOE_CHUNK, :].astype(BF16),
                                preferred_element_type=F32)
            base = (s - MOE_UP_STEPS) * (MOE_DOWN_TILE // LANES)
            for j in range(MOE_DOWN_TILE // LANES):
                out_stage[pl.ds(base + j, m, stride=ROW_PITCH), :] = y[:, j * LANES:(j + 1) * LANES]

    for k in range(1, MOE_ROWS // MOE_SUB + 1):
        pl.when(nsub == k)(functools.partial(compute, k * MOE_SUB))

    @pl.when((s == 1) & has_next)
    def _():
        idx_copy(nxt_item, nxt).wait()
        gather_start(nxt_item, nxt)

    @pl.when((s == MOE_STEPS - 1) & (cnt > 0))
    def _():
        scatter_start(w, cur)

        @pl.when(jnp.logical_not(has_next))
        def _():
            wait_rows(cnt, row_out)


def _moe_experts(hn_lin, w_gate, w_up, w_down, item_e, item_cnt, item_row0, dstk, n_tokens):
    n_items = item_e.shape[0]
    d = D_MODEL
    assert n_tokens & (n_tokens - 1) == 0, "token ids are unpacked from k * n_tokens + token with a mask"

    def up_chunk(s, cnt_ref, w):
        return jnp.where(cnt_ref[w] > 0, jnp.minimum(s, MOE_UP_STEPS - 1), MOE_UP_STEPS - 1)

    def down_block(w, s, e, cn):
        own = (cn[w] > 0) & (s >= MOE_UP_STEPS)
        return (jnp.where(own, e[w], e[jnp.maximum(w - 1, 0)]), 0,
                jnp.where(own, s - MOE_UP_STEPS, MOE_UP_STEPS - 1))

    return pl.pallas_call(
        functools.partial(_moe_kernel, n_tokens=n_tokens),
        grid_spec=pltpu.PrefetchScalarGridSpec(
            num_scalar_prefetch=3,
            grid=(n_items, MOE_STEPS),
            in_specs=[pl.BlockSpec(memory_space=pl.ANY),
                      pl.BlockSpec(memory_space=pl.ANY),
                      pl.BlockSpec((None, d, MOE_CHUNK), lambda w, s, e, cn, r0: (e[w], 0, up_chunk(s, cn, w))),
                      pl.BlockSpec((None, d, MOE_CHUNK), lambda w, s, e, cn, r0: (e[w], 0, up_chunk(s, cn, w))),
                      pl.BlockSpec((None, D_EXPERT, MOE_DOWN_TILE), lambda w, s, e, cn, r0: down_block(w, s, e, cn))],
            out_specs=pl.BlockSpec(memory_space=pl.ANY),
            scratch_shapes=[pltpu.SMEM((2, IDX_WINDOW_ROWS, LANES), jnp.int32),
                            pltpu.VMEM((MOE_ROWS * ROW_PITCH, LANES), F32),
                            pltpu.VMEM((MOE_ROWS, d), BF16),
                            pltpu.VMEM((MOE_UP_STEPS, MOE_ROWS, MOE_CHUNK), BF16),
                            pltpu.VMEM((MOE_ROWS * ROW_PITCH, LANES), F32),
                            pltpu.SemaphoreType.DMA(()),
                            pltpu.SemaphoreType.DMA(()),
                            pltpu.SemaphoreType.DMA(())]),
        out_shape=jax.ShapeDtypeStruct((TOP_K * n_tokens * ROW_PITCH, LANES), F32),
        compiler_params=_cparams(("arbitrary", "arbitrary")),
        name="moe_experts",
    )(item_e, item_cnt, item_row0, dstk, hn_lin, w_gate, w_up, w_down)


def _combine_kernel(h_ref, p0_ref, p1_ref, gate_ref, g_ref, o_ref):
    tm, d = h_ref.shape
    g0 = gate_ref[:, 0:1]
    g1 = gate_ref[:, 1:2]
    ss = jnp.zeros((tm, 1), F32)
    for c in range(ROW_CHUNKS):
        sl = slice(c * LANES, (c + 1) * LANES)
        a = p0_ref[pl.ds(c, tm, stride=ROW_PITCH), :]
        b = p1_ref[pl.ds(c, tm, stride=ROW_PITCH), :]
        y = h_ref[:, sl] + (a * g0 + b * g1)
        o_ref[:, sl] = y
        ss = ss + jnp.sum(y * y, axis=-1, keepdims=True)
    inv = lax.rsqrt(ss / d + EPS)
    o_ref[...] = o_ref[...] * inv * g_ref[...]


def _combine(h, pairs, gates, g_final, *, tm=256):
    t, d = h.shape
    k1 = t // tm
    return pl.pallas_call(
        _combine_kernel,
        grid=(t // tm,),
        in_specs=[pl.BlockSpec((tm, d), lambda i: (i, 0)),
                  pl.BlockSpec((tm * ROW_PITCH, LANES), lambda i: (i, 0)),
                  pl.BlockSpec((tm * ROW_PITCH, LANES), lambda i: (k1 + i, 0)),
                  pl.BlockSpec((tm, TOP_K), lambda i: (i, 0)),
                  pl.BlockSpec((1, d), lambda i: (0, 0))],
        out_specs=pl.BlockSpec((tm, d), lambda i: (i, 0)),
        out_shape=jax.ShapeDtypeStruct((t, d), F32),
        compiler_params=_cparams(("parallel",)),
        name="combine_final_norm",
    )(h, pairs, pairs, gates, g_final.reshape(1, d))


def _route(logits, t):
    group_prob = jax.nn.softmax(logits[:, :N_GROUPS], axis=-1)
    g_sel = jnp.argmax(group_prob, axis=-1)
    g_p = jnp.max(group_prob, axis=-1, keepdims=True)
    e_logits = logits[:, N_GROUPS:N_GROUPS + N_EXPERTS]
    in_group = (jnp.arange(N_EXPERTS) // EXPERTS_PER_GROUP)[None, :] == g_sel[:, None]
    e_prob = jax.nn.softmax(jnp.where(in_group, e_logits, -jnp.inf), axis=-1)
    key = jnp.where(in_group, e_prob, -1.0)
    i1 = jnp.argmax(key, axis=-1)
    key2 = jnp.where(jnp.arange(N_EXPERTS)[None, :] == i1[:, None], -2.0, key)
    i2 = jnp.argmax(key2, axis=-1)
    top_p = jnp.stack([jnp.max(key, axis=-1), jnp.max(key2, axis=-1)], axis=-1)
    expert_idx = jnp.stack([i1, i2], axis=-1)
    assert TOP_K == 2
    gates = g_p * top_p / jnp.sum(top_p, axis=-1, keepdims=True)

    tk = t * TOP_K
    n_items_max = tk // MOE_ROWS + N_EXPERTS
    flat_e = expert_idx.reshape(tk).astype(jnp.int32)
    order = jnp.argsort(flat_e).astype(jnp.int32)
    dstk = (order % TOP_K) * t + order // TOP_K
    dstk = jnp.concatenate([dstk, jnp.zeros((2 * IDX_TILE,), jnp.int32)]).reshape(-1, LANES)
    counts = jnp.sum((flat_e[:, None] == jnp.arange(N_EXPERTS, dtype=jnp.int32)[None, :]).astype(jnp.int32), axis=0)
    start = jnp.cumsum(counts) - counts
    items_of = (counts + MOE_ROWS - 1) // MOE_ROWS
    item_end = jnp.cumsum(items_of)
    wi = jnp.arange(n_items_max, dtype=jnp.int32)
    e_w = jnp.minimum(jnp.sum((item_end[None, :] <= wi[:, None]).astype(jnp.int32), axis=1), N_EXPERTS - 1)
    part = wi - (item_end[e_w] - items_of[e_w])
    active = wi < item_end[-1]
    item_cnt = jnp.where(active, jnp.clip(counts[e_w] - part * MOE_ROWS, 0, MOE_ROWS), 0).astype(jnp.int32)
    item_row0 = jnp.where(active, start[e_w] + part * MOE_ROWS, 0).astype(jnp.int32)
    item_e = jnp.where(active, e_w, e_w[jnp.maximum(item_end[-1] - 1, 0)]).astype(jnp.int32)
    return gates.astype(F32), item_e, item_cnt, item_row0, dstk


def _column_layout_kernel(w_ref, o_ref, *, pieces):
    w = w_ref[...]
    cols = [jnp.zeros((w.shape[0], width), w.dtype) if start is None else w[:, start:start + width]
            for start, width in pieces]
    o_ref[...] = jnp.concatenate(cols, axis=1).astype(o_ref.dtype)


def _column_layout(w, pieces, *, tk=256, name):
    d, n = w.shape
    n_out = sum(width for _, width in pieces)
    return pl.pallas_call(
        functools.partial(_column_layout_kernel, pieces=tuple(pieces)),
        grid=(d // tk,),
        in_specs=[pl.BlockSpec((tk, n), lambda i: (i, 0))],
        out_specs=pl.BlockSpec((tk, n_out), lambda i: (i, 0)),
        out_shape=jax.ShapeDtypeStruct((d, n_out), BF16),
        compiler_params=_cparams(("parallel",)),
        name=name,
    )(w)


def _prep_in_weight(w_in):
    off_kv = MLA_Q_RANK + MLA_KV_RANK
    off_kpe = off_kv + MLA_ROPE_DIM
    off_sq = off_kpe + SWA_HEADS * SWA_HEAD_DIM
    off_sk = off_sq + SWA_KV_HEADS * SWA_HEAD_DIM
    pieces = [(0, off_kv), (off_kpe, off_sq - off_kpe)]
    for h in range(SWA_KV_HEADS):
        pieces += [(off_sq + h * SWA_HEAD_DIM, SWA_HEAD_DIM), (off_sk + h * SWA_HEAD_DIM, SWA_HEAD_DIM)]
    pieces += [(off_kv, MLA_ROPE_DIM), (None, LANES - MLA_ROPE_DIM)]
    return _column_layout(w_in, pieces, name="in_weight_layout")


def _prep_q_weight(w_q_up):
    pieces = []
    for h in range(MLA_HEADS):
        pieces += [(h * MLA_QK_DIM, MLA_QK_DIM), (None, MLA_HEAD_PAD - MLA_QK_DIM)]
    return _column_layout(w_q_up, pieces, name="q_weight_layout")


def kernel(x, positions, g_attn_norm, w_in, g_q_norm, w_q_up, g_kv_norm, w_kv_up, sinks, w_out,
           g_ffn_norm, w_router_group, b_router_group, w_router_expert, b_router_expert,
           w_expert_gate, w_expert_up, w_expert_down, g_final_norm):
    batch, seq, d = x.shape
    t = batch * seq
    assert g_attn_norm.shape[0] == 1, "the final rmsnorm is fused into the (single) layer's combine"
    h = x.reshape(t, d)
    cos, sin = _rope_tables(positions)
    for l in range(g_attn_norm.shape[0]):
        z_kinds = ([ROPE_NONE] * (Z_OFF_SQ // LANES) + [ROPE_FULL] * ((Z_OFF_SKV - Z_OFF_SQ) // LANES)
                   + [ROPE_HALF] * ((Z_WIDTH - Z_OFF_SKV) // LANES))
        z_scales = [SWA_HEAD_DIM ** -0.5 if k == ROPE_FULL else 1.0 for k in z_kinds]
        z = _matmul_rope(_rmsnorm(h, g_attn_norm[l]), _prep_in_weight(w_in[l]), cos, sin,
                         tm=512, tn=Z_WIDTH // 3, kinds=z_kinds, scales=z_scales, name="in_proj")
        q_kinds = [ROPE_NONE, ROPE_HALF] * MLA_HEADS
        q = _norm_matmul(z, Z_OFF_CQ // MLA_Q_RANK, MLA_Q_RANK, g_q_norm[l], _prep_q_weight(w_q_up[l]), cos, sin,
                         tm=1024, tn=1024, kinds=q_kinds, scales=[MLA_QK_DIM ** -0.5] * len(q_kinds), name="q_up")
        kv_kinds = [ROPE_NONE] * (w_kv_up.shape[2] // LANES)
        kv = _norm_matmul(z, Z_OFF_CKV // MLA_KV_RANK, MLA_KV_RANK, g_kv_norm[l], w_kv_up[l], cos, sin,
                          tm=1024, tn=1024, kinds=kv_kinds, scales=[1.0] * len(kv_kinds), name="kv_up")
        mla_o = _mla_attention(q, kv, z, batch, seq)
        swa_o = _swa_attention(z, sinks[l], batch, seq)
        w_r = jnp.concatenate([w_router_group[l], w_router_expert[l],
                               jnp.zeros((d, LANES - N_GROUPS - N_EXPERTS), F32)], axis=1)
        b_r = jnp.concatenate([b_router_group[l], b_router_expert[l],
                               jnp.zeros((LANES - N_GROUPS - N_EXPERTS,), F32)]).reshape(1, LANES)
        h = _out_proj(h, mla_o, swa_o, w_out[l])
        hn, logits = _ffn_norm_router(h, g_ffn_norm[l], w_r, b_r)
        gates, item_e, item_cnt, item_row0, dstk = _route(logits, t)
        pairs = _moe_experts(hn, w_expert_gate[l], w_expert_up[l], w_expert_down[l],
                             item_e, item_cnt, item_row0, dstk, t)
        out = _combine(h, pairs, gates, g_final_norm)
    return out.reshape(batch, seq, d)
```

```python
import functools

import jax
import jax.numpy as jnp
from jax import lax
from jax.experimental import pallas as pl
from jax.experimental.pallas import tpu as pltpu

F32 = jnp.float32
BF16 = jnp.bfloat16

D_MODEL = 4096
EPS = 1e-6
ROPE_THETA = 10000.0
NEG = -1e30
MLA_HEADS = 16
MLA_Q_RANK = 1024
MLA_KV_RANK = 512
MLA_NOPE_DIM = 128
MLA_ROPE_DIM = 64
MLA_V_DIM = 128
MLA_QK_DIM = MLA_NOPE_DIM + MLA_ROPE_DIM
MLA_HEAD_PAD = 256
SWA_HEADS = 32
SWA_KV_HEADS = 4
SWA_GROUP = SWA_HEADS // SWA_KV_HEADS
SWA_HEAD_DIM = 64
SWA_WINDOW = 128
N_GROUPS = 8
EXPERTS_PER_GROUP = 8
N_EXPERTS = N_GROUPS * EXPERTS_PER_GROUP
TOP_K = 2
D_EXPERT = 1024

LANES = 128
Z_OFF_CQ = 0
Z_OFF_CKV = MLA_Q_RANK
Z_OFF_SQ = Z_OFF_CKV + MLA_KV_RANK
Z_OFF_SKV = Z_OFF_SQ + SWA_HEADS * SWA_HEAD_DIM
Z_OFF_KPE = Z_OFF_SKV + SWA_KV_HEADS * LANES
Z_WIDTH = Z_OFF_KPE + LANES
ROPE_NONE, ROPE_FULL, ROPE_HALF = 0, 1, 2

MOE_ROWS = 512
MOE_SUB = 128
MOE_CHUNK = 256
VMEM_LIMIT = 56 * 1024 * 1024


def _cparams(sem, vmem=VMEM_LIMIT):
    return pltpu.CompilerParams(dimension_semantics=sem, vmem_limit_bytes=vmem)


def _swap32(x):
    lane = lax.broadcasted_iota(jnp.int32, x.shape, 1)
    fwd = pltpu.roll(x, LANES - 32, 1)
    bwd = pltpu.roll(x, 32, 1)
    return jnp.where((lane % 64) < 32, fwd, bwd)


def _rope_tile(x, cos, sin, kind):
    if kind == ROPE_NONE:
        return x
    if kind == ROPE_HALF:
        lane = lax.broadcasted_iota(jnp.int32, x.shape, 1)
        cos = jnp.where(lane < 64, cos, 1.0)
        sin = jnp.where(lane < 64, sin, 0.0)
    return x * cos + _swap32(x) * sin


def _rope_table_kernel(pos_ref, invf_ref, sign_ref, cos_ref, sin_ref):
    ang = pos_ref[...].astype(F32) * invf_ref[...]
    cos_ref[...] = jnp.cos(ang)
    sin_ref[...] = jnp.sin(ang) * sign_ref[...]


def _rope_tables(positions):
    t = positions.size
    pos = positions.reshape(t, 1)
    half = SWA_HEAD_DIM // 2
    inv_freq = ROPE_THETA ** (-jnp.arange(0, SWA_HEAD_DIM, 2, dtype=F32) / SWA_HEAD_DIM)
    lane = jnp.arange(LANES)
    invf = inv_freq[lane % half].reshape(1, LANES)
    sign = jnp.where((lane % SWA_HEAD_DIM) < half, -1.0, 1.0).astype(F32).reshape(1, LANES)
    tm = 1024
    return pl.pallas_call(
        _rope_table_kernel,
        grid=(t // tm,),
        in_specs=[pl.BlockSpec((tm, 1), lambda i: (i, 0)),
                  pl.BlockSpec((1, LANES), lambda i: (0, 0)),
                  pl.BlockSpec((1, LANES), lambda i: (0, 0))],
        out_specs=[pl.BlockSpec((tm, LANES), lambda i: (i, 0))] * 2,
        out_shape=[jax.ShapeDtypeStruct((t, LANES), F32)] * 2,
        compiler_params=_cparams(("parallel",)),
        name="rope_tables",
    )(pos, invf, sign)


def _rmsnorm_kernel(x_ref, g_ref, o_ref):
    xf = x_ref[...].astype(F32)
    ms = jnp.mean(xf * xf, axis=-1, keepdims=True)
    o_ref[...] = (xf * lax.rsqrt(ms + EPS) * g_ref[...]).astype(o_ref.dtype)


def _rmsnorm(x, g, *, tm=512):
    t, d = x.shape
    return pl.pallas_call(
        _rmsnorm_kernel,
        grid=(t // tm,),
        in_specs=[pl.BlockSpec((tm, d), lambda i: (i, 0)), pl.BlockSpec((1, d), lambda i: (0, 0))],
        out_specs=pl.BlockSpec((tm, d), lambda i: (i, 0)),
        out_shape=jax.ShapeDtypeStruct((t, d), BF16),
        compiler_params=_cparams(("parallel",)),
        name="attn_norm",
    )(x, g.reshape(1, d))


def _rope_epilogue(acc, cos_ref, sin_ref, o_ref, j, kinds, scales):
    def epilogue(tile_kinds, tile_scales):
        needs_rope = any(k != ROPE_NONE for k in tile_kinds)
        cos = cos_ref[...] if needs_rope else None
        sin = sin_ref[...] if needs_rope else None
        for c, (kind, scale) in enumerate(zip(tile_kinds, tile_scales)):
            blk = _rope_tile(acc[:, c * LANES:(c + 1) * LANES], cos, sin, kind)
            if scale != 1.0:
                blk = blk * scale
            o_ref[:, c * LANES:(c + 1) * LANES] = blk.astype(o_ref.dtype)

    if all(k == kinds[0] for k in kinds) and all(s == scales[0] for s in scales):
        epilogue(kinds[0], scales[0])
    else:
        for jj in range(len(kinds)):
            pl.when(j == jj)(functools.partial(epilogue, kinds[jj], scales[jj]))


def _per_tile(values, n_j, per):
    return tuple(tuple(values[jj * per:(jj + 1) * per]) for jj in range(n_j))


def _matmul_rope_kernel(x_ref, w_ref, cos_ref, sin_ref, o_ref, *, kinds, scales):
    acc = jnp.dot(x_ref[...], w_ref[...], preferred_element_type=F32)
    _rope_epilogue(acc, cos_ref, sin_ref, o_ref, pl.program_id(0), kinds, scales)


def _matmul_rope(x, w, cos, sin, *, tm, tn, kinds, scales, name):
    t, k = x.shape
    n = w.shape[1]
    n_j = n // tn
    return pl.pallas_call(
        functools.partial(_matmul_rope_kernel, kinds=_per_tile(kinds, n_j, tn // LANES),
                          scales=_per_tile(scales, n_j, tn // LANES)),
        grid=(n_j, t // tm),
        in_specs=[pl.BlockSpec((tm, k), lambda j, i: (i, 0)),
                  pl.BlockSpec((k, tn), lambda j, i: (0, j), pipeline_mode=pl.Buffered(1)),
                  pl.BlockSpec((tm, LANES), lambda j, i: (i, 0)),
                  pl.BlockSpec((tm, LANES), lambda j, i: (i, 0))],
        out_specs=pl.BlockSpec((tm, tn), lambda j, i: (i, j)),
        out_shape=jax.ShapeDtypeStruct((t, n), BF16),
        compiler_params=_cparams(("parallel", "parallel")),
        name=name,
    )(x, w, cos, sin)


def _norm_matmul_kernel(x_ref, g_ref, w_ref, cos_ref, sin_ref, o_ref, n_scr, *, kinds, scales):
    j = pl.program_id(1)

    @pl.when(j == 0)
    def _():
        xf = x_ref[...].astype(F32)
        ms = jnp.mean(xf * xf, axis=-1, keepdims=True)
        n_scr[...] = (xf * lax.rsqrt(ms + EPS) * g_ref[...]).astype(BF16)

    acc = jnp.dot(n_scr[...], w_ref[...].astype(BF16), preferred_element_type=F32)
    _rope_epilogue(acc, cos_ref, sin_ref, o_ref, j, kinds, scales)


def _norm_matmul(x, x_col_block, k, g, w, cos, sin, *, tm, tn, kinds, scales, name):
    t = x.shape[0]
    n = w.shape[1]
    n_j = n // tn
    kinds = _per_tile(kinds, n_j, tn // LANES)
    scales = _per_tile(scales, n_j, tn // LANES)
    return pl.pallas_call(
        functools.partial(_norm_matmul_kernel, kinds=kinds, scales=scales),
        grid=(t // tm, n_j),
        in_specs=[pl.BlockSpec((tm, k), lambda i, j: (i, x_col_block)),
                  pl.BlockSpec((1, k), lambda i, j: (0, 0)),
                  pl.BlockSpec((k, tn), lambda i, j: (0, j)),
                  pl.BlockSpec((tm, LANES), lambda i, j: (i, 0)),
                  pl.BlockSpec((tm, LANES), lambda i, j: (i, 0))],
        out_specs=pl.BlockSpec((tm, tn), lambda i, j: (i, j)),
        out_shape=jax.ShapeDtypeStruct((t, n), BF16),
        scratch_shapes=[pltpu.VMEM((tm, k), BF16)],
        compiler_params=_cparams(("parallel", "arbitrary")),
        name=name,
    )(x, g.reshape(1, k), w, cos, sin)


def _mla_kernel(q_ref, kv_ref, kpe_ref, o_ref, kcat_scr, *, tq):
    s_len = q_ref.shape[0]
    kcat_scr[:, :MLA_NOPE_DIM] = kv_ref[:, :MLA_NOPE_DIM]
    kcat_scr[:, MLA_NOPE_DIM:] = kpe_ref[...]
    row = lax.broadcasted_iota(jnp.int32, (tq, tq), 0)
    col = lax.broadcasted_iota(jnp.int32, (tq, tq), 1)
    tri = col <= row
    nt = (((1,), (1,)), ((), ()))
    for qi in range(s_len // tq):
        lo, hi = qi * tq, (qi + 1) * tq
        q = q_ref[lo:hi, :]
        s_diag = lax.dot_general(q, kcat_scr[lo:hi, :], nt, preferred_element_type=F32)
        s_diag = jnp.where(tri, s_diag, NEG)
        m = jnp.max(s_diag, axis=-1, keepdims=True)
        if qi > 0:
            s_past = lax.dot_general(q, kcat_scr[:lo, :], nt, preferred_element_type=F32)
            m = jnp.maximum(m, jnp.max(s_past, axis=-1, keepdims=True))
            p_past = jnp.exp(s_past - m)
        p_diag = jnp.exp(s_diag - m)
        l = jnp.sum(p_diag, axis=-1, keepdims=True)
        o = jnp.dot(p_diag.astype(BF16), kv_ref[lo:hi, MLA_NOPE_DIM:], preferred_element_type=F32)
        if qi > 0:
            l = l + jnp.sum(p_past, axis=-1, keepdims=True)
            o = o + jnp.dot(p_past.astype(BF16), kv_ref[:lo, MLA_NOPE_DIM:], preferred_element_type=F32)
        o_ref[lo:hi, :] = (o / l).astype(o_ref.dtype)


def _mla_attention(q, kv, z, batch, seq):
    t = batch * seq
    kpe_block = Z_OFF_KPE // LANES
    return pl.pallas_call(
        functools.partial(_mla_kernel, tq=256),
        grid=(batch, MLA_HEADS),
        in_specs=[pl.BlockSpec((seq, MLA_HEAD_PAD), lambda b, h: (b, h)),
                  pl.BlockSpec((seq, MLA_NOPE_DIM + MLA_V_DIM), lambda b, h: (b, h)),
                  pl.BlockSpec((seq, LANES), lambda b, h: (b, kpe_block))],
        out_specs=pl.BlockSpec((seq, MLA_V_DIM), lambda b, h: (b, h)),
        out_shape=jax.ShapeDtypeStruct((t, MLA_HEADS * MLA_V_DIM), BF16),
        scratch_shapes=[pltpu.VMEM((seq, MLA_HEAD_PAD), BF16)],
        compiler_params=_cparams(("parallel", "arbitrary")),
        name="mla_attention",
    )(q, kv, z)


def _swa_kernel(sinks_ref, q_ref, kv_ref, o_ref, klo, khi, vlo, vhi):
    h = pl.program_id(1)
    s_len = q_ref.shape[0]
    w = SWA_WINDOW
    pairs = SWA_GROUP // 2
    kv = kv_ref[...].astype(F32)
    lane = lax.broadcasted_iota(jnp.int32, kv.shape, 1)
    low = lane < SWA_HEAD_DIM
    rolled = pltpu.roll(kv, SWA_HEAD_DIM, 1)
    pad = jnp.zeros((w, LANES), BF16)
    for scr, val in ((klo, jnp.where(low, kv, 0.0)), (khi, jnp.where(low, 0.0, rolled)),
                     (vlo, jnp.where(low, rolled, 0.0)), (vhi, jnp.where(low, 0.0, kv))):
        scr[:w, :] = pad
        scr[w:, :] = val.astype(BF16)

    rows = pairs * w
    r = lax.broadcasted_iota(jnp.int32, (rows, 2 * w), 0) % w
    c = lax.broadcasted_iota(jnp.int32, (rows, 2 * w), 1)
    band = (c > r) & (c <= r + w)
    pair_of_row = lax.broadcasted_iota(jnp.int32, (rows, 1), 0) // w
    out_low = lax.broadcasted_iota(jnp.int32, (rows, LANES), 1) < SWA_HEAD_DIM

    def sink_column(parity):
        col = jnp.zeros((rows, 1), F32)
        for j in range(pairs):
            col = jnp.where(pair_of_row == j, sinks_ref[h * SWA_GROUP + 2 * j + parity], col)
        return col

    sink_even, sink_odd = sink_column(0), sink_column(1)
    nt = (((1,), (1,)), ((), ()))

    def block(n, mask):
        r0 = n * w if isinstance(n, int) else pl.multiple_of(n * w, w)
        qs = jnp.concatenate([q_ref[pl.ds(r0, w), j * LANES:(j + 1) * LANES] for j in range(pairs)], axis=0)

        def head(k_scr, v_scr, sink):
            s = lax.dot_general(qs, k_scr[pl.ds(r0, 2 * w), :], nt, preferred_element_type=F32)
            s = jnp.where(mask, s, NEG)
            m = jnp.maximum(jnp.max(s, axis=-1, keepdims=True), sink)
            p = jnp.exp(s - m)
            l = jnp.sum(p, axis=-1, keepdims=True) + jnp.exp(sink - m)
            o = jnp.dot(p.astype(BF16), v_scr[pl.ds(r0, 2 * w), :], preferred_element_type=F32)
            return o, l

        o_even, l_even = head(klo, vlo, sink_even)
        o_odd, l_odd = head(khi, vhi, sink_odd)
        o = (o_even + o_odd) / jnp.where(out_low, l_even, l_odd)
        for j in range(pairs):
            o_ref[pl.ds(r0, w), j * LANES:(j + 1) * LANES] = o[j * w:(j + 1) * w].astype(o_ref.dtype)

    block(0, band & (c >= w))

    def later_block(n, carry):
        block(n, band)
        return carry

    lax.fori_loop(1, s_len // w, later_block, 0)


def _swa_attention(z, sinks, batch, seq):
    t = batch * seq
    gw = SWA_GROUP * SWA_HEAD_DIM
    q_block0 = Z_OFF_SQ // gw
    kv_block0 = Z_OFF_SKV // LANES
    return pl.pallas_call(
        _swa_kernel,
        grid_spec=pltpu.PrefetchScalarGridSpec(
            num_scalar_prefetch=1,
            grid=(batch, SWA_KV_HEADS),
            in_specs=[pl.BlockSpec((seq, gw), lambda b, h, s: (b, q_block0 + h)),
                      pl.BlockSpec((seq, LANES), lambda b, h, s: (b, kv_block0 + h))],
            out_specs=pl.BlockSpec((seq, gw), lambda b, h, s: (b, h)),
            scratch_shapes=[pltpu.VMEM((seq + SWA_WINDOW, LANES), BF16)] * 4),
        out_shape=jax.ShapeDtypeStruct((t, SWA_HEADS * SWA_HEAD_DIM), BF16),
        compiler_params=_cparams(("parallel", "arbitrary")),
        name="swa_attention",
    )(sinks.astype(F32), z, z)


def _out_proj_kernel(x_ref, a_ref, s_ref, wt_ref, wb_ref, h_ref):
    h_ref[...] = (x_ref[...] + jnp.dot(a_ref[...], wt_ref[...].astype(BF16), preferred_element_type=F32)
                  + jnp.dot(s_ref[...], wb_ref[...].astype(BF16), preferred_element_type=F32))


def _out_proj(x2, mla_o, swa_o, w_out, *, tm=1024, tn=512):
    t, d = x2.shape
    half = mla_o.shape[1]
    return pl.pallas_call(
        _out_proj_kernel,
        grid=(t // tm, d // tn),
        in_specs=[pl.BlockSpec((tm, tn), lambda i, j: (i, j)),
                  pl.BlockSpec((tm, half), lambda i, j: (i, 0)),
                  pl.BlockSpec((tm, half), lambda i, j: (i, 0)),
                  pl.BlockSpec((half, tn), lambda i, j: (0, j)),
                  pl.BlockSpec((half, tn), lambda i, j: (1, j))],
        out_specs=pl.BlockSpec((tm, tn), lambda i, j: (i, j)),
        out_shape=jax.ShapeDtypeStruct((t, d), F32),
        compiler_params=_cparams(("parallel", "parallel")),
        name="out_proj",
    )(x2, mla_o, swa_o, w_out, w_out)


ROW_CHUNKS = D_MODEL // (2 * LANES)
ROW_PITCH = 24
U32 = jnp.uint32
HIGH_HALF = 0xFFFF0000


def _pack_pair(lo, hi):
    lo_bits = lax.bitcast_convert_type(lo.astype(BF16).astype(F32), U32)
    hi_bits = lax.bitcast_convert_type(hi.astype(BF16).astype(F32), U32)
    return (lo_bits >> 16) | (hi_bits & U32(HIGH_HALF))


def _unpack_pair(words):
    return (lax.bitcast_convert_type(words << 16, F32),
            lax.bitcast_convert_type(words & U32(HIGH_HALF), F32))


def _chunk_cols(c):
    return (slice(2 * c * LANES, (2 * c + 1) * LANES), slice((2 * c + 1) * LANES, (2 * c + 2) * LANES))


def _split_bf16(v):
    hi = v.astype(BF16)
    return hi, (v - hi.astype(F32)).astype(BF16)


def _ffn_norm_router_kernel(h_ref, g_ref, wr_ref, br_ref, o_ref, lg_ref):
    tm = h_ref.shape[0]
    h = h_ref[...]
    inv = lax.rsqrt(jnp.mean(h * h, axis=-1, keepdims=True) + EPS)
    hg_hi, hg_lo = _split_bf16(h * g_ref[...])
    wr_hi, wr_lo = _split_bf16(wr_ref[...])
    acc = (jnp.dot(hg_hi, wr_hi, preferred_element_type=F32)
           + (jnp.dot(hg_hi, wr_lo, preferred_element_type=F32)
              + jnp.dot(hg_lo, wr_hi, preferred_element_type=F32)))
    lg_ref[...] = acc * inv + br_ref[...]
    for c in range(ROW_CHUNKS):
        lo, hi = _chunk_cols(c)
        o_ref[pl.ds(c, tm, stride=ROW_PITCH), :] = _pack_pair(h_ref[:, lo] * inv * g_ref[:, lo],
                                                              h_ref[:, hi] * inv * g_ref[:, hi])
    for c in range(ROW_CHUNKS, ROW_PITCH):
        o_ref[pl.ds(c, tm, stride=ROW_PITCH), :] = jnp.zeros((tm, LANES), o_ref.dtype)


def _ffn_norm_router(h, g, w_r, b_r, *, tm=256):
    t, d = h.shape
    return pl.pallas_call(
        _ffn_norm_router_kernel,
        grid=(t // tm,),
        in_specs=[pl.BlockSpec((tm, d), lambda i: (i, 0)),
                  pl.BlockSpec((1, d), lambda i: (0, 0)),
                  pl.BlockSpec((d, LANES), lambda i: (0, 0)),
                  pl.BlockSpec((1, LANES), lambda i: (0, 0))],
        out_specs=[pl.BlockSpec((tm * ROW_PITCH, LANES), lambda i: (i, 0)),
                   pl.BlockSpec((tm, LANES), lambda i: (i, 0))],
        out_shape=[jax.ShapeDtypeStruct((t * ROW_PITCH, LANES), U32),
                   jax.ShapeDtypeStruct((t, LANES), F32)],
        compiler_params=_cparams(("parallel",)),
        name="ffn_norm_router",
    )(h, g.reshape(1, d), w_r, b_r)


MOE_UP_STEPS = D_EXPERT // MOE_CHUNK
MOE_STEPS = 2 * MOE_UP_STEPS
MOE_DOWN_TILE = D_MODEL // MOE_UP_STEPS


LANES_LOG2 = LANES.bit_length() - 1
IDX_TILE_ROWS = 8
IDX_TILE = IDX_TILE_ROWS * LANES
IDX_TILE_LOG2 = IDX_TILE.bit_length() - 1
IDX_WINDOW_ROWS = 2 * IDX_TILE_ROWS
assert MOE_ROWS <= IDX_TILE
ROW_UNROLL_LOG2 = 3
ROW_UNROLL = 1 << ROW_UNROLL_LOG2


def _moe_kernel(item_e, item_cnt, item_row0, dstk_ref, hn_ref, wg_ref, wu_ref, wd_ref, pairs_ref,
                idx_win, x_stage, x_bf, hid, out_stage, sem_idx, sem_in, sem_out, *, n_tokens):
    del item_e
    w = pl.program_id(0)
    s = pl.program_id(1)
    n_items = pl.num_programs(0)
    cnt = item_cnt[w]
    nsub = (cnt + MOE_SUB - 1) // MOE_SUB
    nxt_item = jnp.minimum(w + 1, n_items - 1)
    has_next = (w + 1 < n_items) & (item_cnt[nxt_item] > 0)
    cur = w & 1
    nxt = 1 - cur

    def idx_copy(item, buf):
        first = (item_row0[item] >> IDX_TILE_LOG2) * IDX_TILE_ROWS
        return pltpu.make_async_copy(dstk_ref.at[pl.ds(pl.multiple_of(first, IDX_TILE_ROWS), IDX_WINDOW_ROWS)],
                                     idx_win.at[buf], sem_idx)

    def unrolled_rows(n, one_row):
        groups = n >> ROW_UNROLL_LOG2

        def group(g, carry):
            for u in range(ROW_UNROLL):
                one_row(g * ROW_UNROLL + u)
            return carry

        def single(r, carry):
            one_row(r)
            return carry

        lax.fori_loop(0, groups, group, 0)
        lax.fori_loop(groups * ROW_UNROLL, n, single, 0)

    def for_rows(item, buf, fn):
        base = item_row0[item] & (IDX_TILE - 1)

        def one_row(r):
            off = base + r
            fn(idx_win[buf, off >> LANES_LOG2, off & (LANES - 1)], r)

        unrolled_rows(item_cnt[item], one_row)

    def row_in(tok, r):
        return pltpu.make_async_copy(hn_ref.at[pl.ds(pl.multiple_of(tok * ROW_PITCH, 8), ROW_PITCH)],
                                     x_stage.at[pl.ds(pl.multiple_of(r * ROW_PITCH, 8), ROW_PITCH)], sem_in)

    def row_out(slot, r):
        return pltpu.make_async_copy(out_stage.at[pl.ds(pl.multiple_of(r * ROW_PITCH, 8), ROW_PITCH)],
                                     pairs_ref.at[pl.ds(pl.multiple_of(slot * ROW_PITCH, 8), ROW_PITCH)], sem_out)

    def gather_start(item, buf):
        for_rows(item, buf, lambda v, r: row_in(v & (n_tokens - 1), r).start())

    def scatter_start(item, buf):
        for_rows(item, buf, lambda v, r: row_out(v, r).start())

    def wait_rows(n, copy):
        unrolled_rows(n, lambda r: copy(0, r).wait())

    @pl.when((w == 0) & (s == 0))
    def _():
        x_stage[...] = jnp.zeros(x_stage.shape, x_stage.dtype)
        out_stage[...] = jnp.zeros(out_stage.shape, out_stage.dtype)

        @pl.when(cnt > 0)
        def _():
            first_window = idx_copy(0, 0)
            first_window.start()
            first_window.wait()
            gather_start(0, 0)

    @pl.when((s == 0) & (cnt > 0))
    def _():
        wait_rows(cnt, row_in)

    @pl.when((s == 0) & has_next)
    def _():
        idx_copy(nxt_item, nxt).start()

    @pl.when((s == MOE_UP_STEPS) & (w > 0) & (cnt > 0))
    def _():
        wait_rows(item_cnt[jnp.maximum(w - 1, 0)], row_out)

    def compute(m):
        @pl.when(s == 0)
        def _():
            for c in range(ROW_CHUNKS):
                lo, hi = _chunk_cols(c)
                x_lo, x_hi = _unpack_pair(x_stage[pl.ds(c, m, stride=ROW_PITCH), :])
                x_bf[:m, lo] = x_lo.astype(BF16)
                x_bf[:m, hi] = x_hi.astype(BF16)

        @pl.when(s < MOE_UP_STEPS)
        def _():
            x = x_bf[:m, :]
            g = jnp.dot(x, wg_ref[...].astype(BF16), preferred_element_type=F32)
            u = jnp.dot(x, wu_ref[...].astype(BF16), preferred_element_type=F32)
            hid[s, :m, :] = ((g * jax.nn.sigmoid(g)) * u).astype(BF16)

        @pl.when(s >= MOE_UP_STEPS)
        def _():
            y = jnp.zeros((m, MOE_DOWN_TILE), F32)
            for c in range(MOE_UP_STEPS):
                y = y + jnp.dot(hid[c, :m, :], wd_ref[c * MOE_CHUNK:(c + 1) * MOE_CHUNK, :].astype(BF16),
                                preferred_element_type=F32)
            tile_chunks = MOE_DOWN_TILE // (2 * LANES)
            base = (s - MOE_UP_STEPS) * tile_chunks
            for c in range(tile_chunks):
                lo, hi = _chunk_cols(c)
                out_stage[pl.ds(base + c, m, stride=ROW_PITCH), :] = _pack_pair(y[:, lo], y[:, hi])

    for k in range(1, MOE_ROWS // MOE_SUB + 1):
        pl.when(nsub == k)(functools.partial(compute, k * MOE_SUB))

    @pl.when((s == 1) & has_next)
    def _():
        idx_copy(nxt_item, nxt).wait()
        gather_start(nxt_item, nxt)

    @pl.when((s == MOE_STEPS - 1) & (cnt > 0))
    def _():
        scatter_start(w, cur)

        @pl.when(jnp.logical_not(has_next))
        def _():
            wait_rows(cnt, row_out)


def _moe_experts(hn_lin, w_gate, w_up, w_down, item_e, item_cnt, item_row0, dstk, n_tokens):
    n_items = item_e.shape[0]
    d = D_MODEL
    assert n_tokens & (n_tokens - 1) == 0, "token ids are unpacked from k * n_tokens + token with a mask"

    def up_chunk(s, cnt_ref, w):
        return jnp.where(cnt_ref[w] > 0, jnp.minimum(s, MOE_UP_STEPS - 1), MOE_UP_STEPS - 1)

    def down_block(w, s, e, cn):
        own = (cn[w] > 0) & (s >= MOE_UP_STEPS)
        return (jnp.where(own, e[w], e[jnp.maximum(w - 1, 0)]), 0,
                jnp.where(own, s - MOE_UP_STEPS, MOE_UP_STEPS - 1))

    return pl.pallas_call(
        functools.partial(_moe_kernel, n_tokens=n_tokens),
        grid_spec=pltpu.PrefetchScalarGridSpec(
            num_scalar_prefetch=3,
            grid=(n_items, MOE_STEPS),
            in_specs=[pl.BlockSpec(memory_space=pl.ANY),
                      pl.BlockSpec(memory_space=pl.ANY),
                      pl.BlockSpec((None, d, MOE_CHUNK), lambda w, s, e, cn, r0: (e[w], 0, up_chunk(s, cn, w))),
                      pl.BlockSpec((None, d, MOE_CHUNK), lambda w, s, e, cn, r0: (e[w], 0, up_chunk(s, cn, w))),
                      pl.BlockSpec((None, D_EXPERT, MOE_DOWN_TILE), lambda w, s, e, cn, r0: down_block(w, s, e, cn))],
            out_specs=pl.BlockSpec(memory_space=pl.ANY),
            scratch_shapes=[pltpu.SMEM((2, IDX_WINDOW_ROWS, LANES), jnp.int32),
                            pltpu.VMEM((MOE_ROWS * ROW_PITCH, LANES), U32),
                            pltpu.VMEM((MOE_ROWS, d), BF16),
                            pltpu.VMEM((MOE_UP_STEPS, MOE_ROWS, MOE_CHUNK), BF16),
                            pltpu.VMEM((MOE_ROWS * ROW_PITCH, LANES), U32),
                            pltpu.SemaphoreType.DMA(()),
                            pltpu.SemaphoreType.DMA(()),
                            pltpu.SemaphoreType.DMA(())]),
        out_shape=jax.ShapeDtypeStruct((TOP_K * n_tokens * ROW_PITCH, LANES), U32),
        compiler_params=_cparams(("arbitrary", "arbitrary")),
        name="moe_experts",
    )(item_e, item_cnt, item_row0, dstk, hn_lin, w_gate, w_up, w_down)


def _combine_kernel(h_ref, p0_ref, p1_ref, gate_ref, g_ref, o_ref):
    tm, d = h_ref.shape
    g0 = gate_ref[:, 0:1]
    g1 = gate_ref[:, 1:2]
    ss = jnp.zeros((tm, 1), F32)
    for c in range(ROW_CHUNKS):
        a = _unpack_pair(p0_ref[pl.ds(c, tm, stride=ROW_PITCH), :])
        b = _unpack_pair(p1_ref[pl.ds(c, tm, stride=ROW_PITCH), :])
        for sl, a_half, b_half in zip(_chunk_cols(c), a, b):
            y = h_ref[:, sl] + (a_half * g0 + b_half * g1)
            o_ref[:, sl] = y
            ss = ss + jnp.sum(y * y, axis=-1, keepdims=True)
    inv = lax.rsqrt(ss / d + EPS)
    o_ref[...] = o_ref[...] * inv * g_ref[...]


def _combine(h, pairs, gates, g_final, *, tm=256):
    t, d = h.shape
    k1 = t // tm
    return pl.pallas_call(
        _combine_kernel,
        grid=(t // tm,),
        in_specs=[pl.BlockSpec((tm, d), lambda i: (i, 0)),
                  pl.BlockSpec((tm * ROW_PITCH, LANES), lambda i: (i, 0)),
                  pl.BlockSpec((tm * ROW_PITCH, LANES), lambda i: (k1 + i, 0)),
                  pl.BlockSpec((tm, TOP_K), lambda i: (i, 0)),
                  pl.BlockSpec((1, d), lambda i: (0, 0))],
        out_specs=pl.BlockSpec((tm, d), lambda i: (i, 0)),
        out_shape=jax.ShapeDtypeStruct((t, d), F32),
        compiler_params=_cparams(("parallel",)),
        name="combine_final_norm",
    )(h, pairs, pairs, gates, g_final.reshape(1, d))


def _route(logits, t):
    group_prob = jax.nn.softmax(logits[:, :N_GROUPS], axis=-1)
    g_sel = jnp.argmax(group_prob, axis=-1)
    g_p = jnp.max(group_prob, axis=-1, keepdims=True)
    e_logits = logits[:, N_GROUPS:N_GROUPS + N_EXPERTS]
    in_group = (jnp.arange(N_EXPERTS) // EXPERTS_PER_GROUP)[None, :] == g_sel[:, None]
    e_prob = jax.nn.softmax(jnp.where(in_group, e_logits, -jnp.inf), axis=-1)
    key = jnp.where(in_group, e_prob, -1.0)
    i1 = jnp.argmax(key, axis=-1)
    key2 = jnp.where(jnp.arange(N_EXPERTS)[None, :] == i1[:, None], -2.0, key)
    i2 = jnp.argmax(key2, axis=-1)
    top_p = jnp.stack([jnp.max(key, axis=-1), jnp.max(key2, axis=-1)], axis=-1)
    expert_idx = jnp.stack([i1, i2], axis=-1)
    assert TOP_K == 2
    gates = g_p * top_p / jnp.sum(top_p, axis=-1, keepdims=True)

    tk = t * TOP_K
    n_items_max = tk // MOE_ROWS + N_EXPERTS
    flat_e = expert_idx.reshape(tk).astype(jnp.int32)
    order = jnp.argsort(flat_e).astype(jnp.int32)
    dstk = (order % TOP_K) * t + order // TOP_K
    dstk = jnp.concatenate([dstk, jnp.zeros((2 * IDX_TILE,), jnp.int32)]).reshape(-1, LANES)
    counts = jnp.sum((flat_e[:, None] == jnp.arange(N_EXPERTS, dtype=jnp.int32)[None, :]).astype(jnp.int32), axis=0)
    start = jnp.cumsum(counts) - counts
    items_of = (counts + MOE_ROWS - 1) // MOE_ROWS
    item_end = jnp.cumsum(items_of)
    wi = jnp.arange(n_items_max, dtype=jnp.int32)
    e_w = jnp.minimum(jnp.sum((item_end[None, :] <= wi[:, None]).astype(jnp.int32), axis=1), N_EXPERTS - 1)
    part = wi - (item_end[e_w] - items_of[e_w])
    active = wi < item_end[-1]
    item_cnt = jnp.where(active, jnp.clip(counts[e_w] - part * MOE_ROWS, 0, MOE_ROWS), 0).astype(jnp.int32)
    item_row0 = jnp.where(active, start[e_w] + part * MOE_ROWS, 0).astype(jnp.int32)
    item_e = jnp.where(active, e_w, e_w[jnp.maximum(item_end[-1] - 1, 0)]).astype(jnp.int32)
    return gates.astype(F32), item_e, item_cnt, item_row0, dstk


def _column_layout_kernel(w_ref, o_ref, *, pieces):
    w = w_ref[...]
    cols = [jnp.zeros((w.shape[0], width), w.dtype) if start is None else w[:, start:start + width]
            for start, width in pieces]
    o_ref[...] = jnp.concatenate(cols, axis=1).astype(o_ref.dtype)


def _column_layout(w, layer, pieces, *, tk=256, name):
    _, d, n = w.shape
    n_out = sum(width for _, width in pieces)
    return pl.pallas_call(
        functools.partial(_column_layout_kernel, pieces=tuple(pieces)),
        grid=(d // tk,),
        in_specs=[pl.BlockSpec((None, tk, n), lambda i: (layer, i, 0))],
        out_specs=pl.BlockSpec((tk, n_out), lambda i: (i, 0)),
        out_shape=jax.ShapeDtypeStruct((d, n_out), BF16),
        compiler_params=_cparams(("parallel",)),
        name=name,
    )(w)


def _prep_in_weight(w_in, layer):
    off_kv = MLA_Q_RANK + MLA_KV_RANK
    off_kpe = off_kv + MLA_ROPE_DIM
    off_sq = off_kpe + SWA_HEADS * SWA_HEAD_DIM
    off_sk = off_sq + SWA_KV_HEADS * SWA_HEAD_DIM
    pieces = [(0, off_kv), (off_kpe, off_sq - off_kpe)]
    for h in range(SWA_KV_HEADS):
        pieces += [(off_sq + h * SWA_HEAD_DIM, SWA_HEAD_DIM), (off_sk + h * SWA_HEAD_DIM, SWA_HEAD_DIM)]
    pieces += [(off_kv, MLA_ROPE_DIM), (None, LANES - MLA_ROPE_DIM)]
    return _column_layout(w_in, layer, pieces, name="in_weight_layout")


def _prep_q_weight(w_q_up, layer):
    pieces = []
    for h in range(MLA_HEADS):
        pieces += [(h * MLA_QK_DIM, MLA_QK_DIM), (None, MLA_HEAD_PAD - MLA_QK_DIM)]
    return _column_layout(w_q_up, layer, pieces, name="q_weight_layout")


def kernel(x, positions, g_attn_norm, w_in, g_q_norm, w_q_up, g_kv_norm, w_kv_up, sinks, w_out,
           g_ffn_norm, w_router_group, b_router_group, w_router_expert, b_router_expert,
           w_expert_gate, w_expert_up, w_expert_down, g_final_norm):
    batch, seq, d = x.shape
    t = batch * seq
    assert g_attn_norm.shape[0] == 1, "the final rmsnorm is fused into the (single) layer's combine"
    h = x.reshape(t, d)
    cos, sin = _rope_tables(positions)
    for l in range(g_attn_norm.shape[0]):
        z_kinds = ([ROPE_NONE] * (Z_OFF_SQ // LANES) + [ROPE_FULL] * ((Z_OFF_SKV - Z_OFF_SQ) // LANES)
                   + [ROPE_HALF] * ((Z_WIDTH - Z_OFF_SKV) // LANES))
        z_scales = [SWA_HEAD_DIM ** -0.5 if k == ROPE_FULL else 1.0 for k in z_kinds]
        z = _matmul_rope(_rmsnorm(h, g_attn_norm[l]), _prep_in_weight(w_in, l), cos, sin,
                         tm=1024, tn=Z_WIDTH // 3, kinds=z_kinds, scales=z_scales, name="in_proj")
        q_kinds = [ROPE_NONE, ROPE_HALF] * MLA_HEADS
        q = _norm_matmul(z, Z_OFF_CQ // MLA_Q_RANK, MLA_Q_RANK, g_q_norm[l], _prep_q_weight(w_q_up, l), cos, sin,
                         tm=1024, tn=1024, kinds=q_kinds, scales=[MLA_QK_DIM ** -0.5] * len(q_kinds), name="q_up")
        kv_kinds = [ROPE_NONE] * (w_kv_up.shape[2] // LANES)
        kv = _norm_matmul(z, Z_OFF_CKV // MLA_KV_RANK, MLA_KV_RANK, g_kv_norm[l], w_kv_up[l], cos, sin,
                          tm=1024, tn=1024, kinds=kv_kinds, scales=[1.0] * len(kv_kinds), name="kv_up")
        mla_o = _mla_attention(q, kv, z, batch, seq)
        swa_o = _swa_attention(z, sinks[l], batch, seq)
        w_r = jnp.concatenate([w_router_group[l], w_router_expert[l],
                               jnp.zeros((d, LANES - N_GROUPS - N_EXPERTS), F32)], axis=1)
        b_r = jnp.concatenate([b_router_group[l], b_router_expert[l],
                               jnp.zeros((LANES - N_GROUPS - N_EXPERTS,), F32)]).reshape(1, LANES)
        h = _out_proj(h, mla_o, swa_o, w_out[l])
        hn, logits = _ffn_norm_router(h, g_ffn_norm[l], w_r, b_r)
        gates, item_e, item_cnt, item_row0, dstk = _route(logits, t)
        pairs = _moe_experts(hn, w_expert_gate[l], w_expert_up[l], w_expert_down[l],
                             item_e, item_cnt, item_row0, dstk, t)
        out = _combine(h, pairs, gates, g_final_norm)
    return out.reshape(batch, seq, d)
```

```python
import functools

import jax
import jax.numpy as jnp
from jax import lax
from jax.experimental import pallas as pl
from jax.experimental.pallas import tpu as pltpu

F32 = jnp.float32
BF16 = jnp.bfloat16

D_MODEL = 4096
EPS = 1e-6
ROPE_THETA = 10000.0
NEG = -1e30
MLA_HEADS = 16
MLA_Q_RANK = 1024
MLA_KV_RANK = 512
MLA_NOPE_DIM = 128
MLA_ROPE_DIM = 64
MLA_V_DIM = 128
MLA_QK_DIM = MLA_NOPE_DIM + MLA_ROPE_DIM
MLA_HEAD_PAD = 256
SWA_HEADS = 32
SWA_KV_HEADS = 4
SWA_GROUP = SWA_HEADS // SWA_KV_HEADS
SWA_HEAD_DIM = 64
SWA_WINDOW = 128
N_GROUPS = 8
EXPERTS_PER_GROUP = 8
N_EXPERTS = N_GROUPS * EXPERTS_PER_GROUP
TOP_K = 2
D_EXPERT = 1024

LANES = 128
Z_OFF_CQ = 0
Z_OFF_CKV = MLA_Q_RANK
Z_OFF_SQ = Z_OFF_CKV + MLA_KV_RANK
Z_OFF_SKV = Z_OFF_SQ + SWA_HEADS * SWA_HEAD_DIM
Z_OFF_KPE = Z_OFF_SKV + SWA_KV_HEADS * LANES
Z_WIDTH = Z_OFF_KPE + LANES
ROPE_NONE, ROPE_FULL, ROPE_HALF = 0, 1, 2

MOE_ROWS = 512
MOE_SUB = 128
MOE_CHUNK = 256
VMEM_LIMIT = 56 * 1024 * 1024


def _cparams(sem, vmem=VMEM_LIMIT):
    return pltpu.CompilerParams(dimension_semantics=sem, vmem_limit_bytes=vmem)


def _swap32(x):
    lane = lax.broadcasted_iota(jnp.int32, x.shape, 1)
    fwd = pltpu.roll(x, LANES - 32, 1)
    bwd = pltpu.roll(x, 32, 1)
    return jnp.where((lane % 64) < 32, fwd, bwd)


def _rope_tile(x, cos, sin, kind):
    if kind == ROPE_NONE:
        return x
    if kind == ROPE_HALF:
        lane = lax.broadcasted_iota(jnp.int32, x.shape, 1)
        cos = jnp.where(lane < 64, cos, 1.0)
        sin = jnp.where(lane < 64, sin, 0.0)
    return x * cos + _swap32(x) * sin


def _rope_table_kernel(pos_ref, invf_ref, sign_ref, cos_ref, sin_ref):
    ang = pos_ref[...].astype(F32) * invf_ref[...]
    cos_ref[...] = jnp.cos(ang)
    sin_ref[...] = jnp.sin(ang) * sign_ref[...]


def _rope_tables(positions):
    t = positions.size
    pos = positions.reshape(t, 1)
    half = SWA_HEAD_DIM // 2
    inv_freq = ROPE_THETA ** (-jnp.arange(0, SWA_HEAD_DIM, 2, dtype=F32) / SWA_HEAD_DIM)
    lane = jnp.arange(LANES)
    invf = inv_freq[lane % half].reshape(1, LANES)
    sign = jnp.where((lane % SWA_HEAD_DIM) < half, -1.0, 1.0).astype(F32).reshape(1, LANES)
    tm = 1024
    return pl.pallas_call(
        _rope_table_kernel,
        grid=(t // tm,),
        in_specs=[pl.BlockSpec((tm, 1), lambda i: (i, 0)),
                  pl.BlockSpec((1, LANES), lambda i: (0, 0)),
                  pl.BlockSpec((1, LANES), lambda i: (0, 0))],
        out_specs=[pl.BlockSpec((tm, LANES), lambda i: (i, 0))] * 2,
        out_shape=[jax.ShapeDtypeStruct((t, LANES), F32)] * 2,
        compiler_params=_cparams(("parallel",)),
        name="rope_tables",
    )(pos, invf, sign)


def _rmsnorm_kernel(x_ref, g_ref, o_ref):
    xf = x_ref[...].astype(F32)
    ms = jnp.mean(xf * xf, axis=-1, keepdims=True)
    o_ref[...] = (xf * lax.rsqrt(ms + EPS) * g_ref[...]).astype(o_ref.dtype)


def _rmsnorm(x, g, *, tm=512):
    t, d = x.shape
    return pl.pallas_call(
        _rmsnorm_kernel,
        grid=(t // tm,),
        in_specs=[pl.BlockSpec((tm, d), lambda i: (i, 0)), pl.BlockSpec((1, d), lambda i: (0, 0))],
        out_specs=pl.BlockSpec((tm, d), lambda i: (i, 0)),
        out_shape=jax.ShapeDtypeStruct((t, d), BF16),
        compiler_params=_cparams(("parallel",)),
        name="attn_norm",
    )(x, g.reshape(1, d))


def _rope_epilogue(acc, cos_ref, sin_ref, o_ref, j, kinds, scales):
    def epilogue(tile_kinds, tile_scales):
        needs_rope = any(k != ROPE_NONE for k in tile_kinds)
        cos = cos_ref[...] if needs_rope else None
        sin = sin_ref[...] if needs_rope else None
        for c, (kind, scale) in enumerate(zip(tile_kinds, tile_scales)):
            blk = _rope_tile(acc[:, c * LANES:(c + 1) * LANES], cos, sin, kind)
            if scale != 1.0:
                blk = blk * scale
            o_ref[:, c * LANES:(c + 1) * LANES] = blk.astype(o_ref.dtype)

    if all(k == kinds[0] for k in kinds) and all(s == scales[0] for s in scales):
        epilogue(kinds[0], scales[0])
    else:
        for jj in range(len(kinds)):
            pl.when(j == jj)(functools.partial(epilogue, kinds[jj], scales[jj]))


def _per_tile(values, n_j, per):
    return tuple(tuple(values[jj * per:(jj + 1) * per]) for jj in range(n_j))


def _matmul_rope_kernel(x_ref, w_ref, cos_ref, sin_ref, o_ref, *, kinds, scales):
    acc = jnp.dot(x_ref[...], w_ref[...], preferred_element_type=F32)
    _rope_epilogue(acc, cos_ref, sin_ref, o_ref, pl.program_id(0), kinds, scales)


def _matmul_rope(x, w, cos, sin, *, tm, tn, kinds, scales, name):
    t, k = x.shape
    n = w.shape[1]
    n_j = n // tn
    return pl.pallas_call(
        functools.partial(_matmul_rope_kernel, kinds=_per_tile(kinds, n_j, tn // LANES),
                          scales=_per_tile(scales, n_j, tn // LANES)),
        grid=(n_j, t // tm),
        in_specs=[pl.BlockSpec((tm, k), lambda j, i: (i, 0)),
                  pl.BlockSpec((k, tn), lambda j, i: (0, j)),
                  pl.BlockSpec((tm, LANES), lambda j, i: (i, 0)),
                  pl.BlockSpec((tm, LANES), lambda j, i: (i, 0))],
        out_specs=pl.BlockSpec((tm, tn), lambda j, i: (i, j)),
        out_shape=jax.ShapeDtypeStruct((t, n), BF16),
        compiler_params=_cparams(("parallel", "parallel")),
        name=name,
    )(x, w, cos, sin)


def _norm_matmul_kernel(x_ref, g_ref, w_ref, cos_ref, sin_ref, o_ref, n_scr, *, kinds, scales):
    j = pl.program_id(1)

    @pl.when(j == 0)
    def _():
        xf = x_ref[...].astype(F32)
        ms = jnp.mean(xf * xf, axis=-1, keepdims=True)
        n_scr[...] = (xf * lax.rsqrt(ms + EPS) * g_ref[...]).astype(BF16)

    acc = jnp.dot(n_scr[...], w_ref[...].astype(BF16), preferred_element_type=F32)
    _rope_epilogue(acc, cos_ref, sin_ref, o_ref, j, kinds, scales)


def _norm_matmul(x, x_col_block, k, g, w, cos, sin, *, tm, tn, kinds, scales, name):
    t = x.shape[0]
    n = w.shape[1]
    n_j = n // tn
    kinds = _per_tile(kinds, n_j, tn // LANES)
    scales = _per_tile(scales, n_j, tn // LANES)
    return pl.pallas_call(
        functools.partial(_norm_matmul_kernel, kinds=kinds, scales=scales),
        grid=(t // tm, n_j),
        in_specs=[pl.BlockSpec((tm, k), lambda i, j: (i, x_col_block)),
                  pl.BlockSpec((1, k), lambda i, j: (0, 0)),
                  pl.BlockSpec((k, tn), lambda i, j: (0, j)),
                  pl.BlockSpec((tm, LANES), lambda i, j: (i, 0)),
                  pl.BlockSpec((tm, LANES), lambda i, j: (i, 0))],
        out_specs=pl.BlockSpec((tm, tn), lambda i, j: (i, j)),
        out_shape=jax.ShapeDtypeStruct((t, n), BF16),
        scratch_shapes=[pltpu.VMEM((tm, k), BF16)],
        compiler_params=_cparams(("parallel", "arbitrary")),
        name=name,
    )(x, g.reshape(1, k), w, cos, sin)


def _mla_kernel(q_ref, kv_ref, kpe_ref, o_ref, kcat_scr, *, tq):
    s_len = q_ref.shape[0]
    kcat_scr[:, :MLA_NOPE_DIM] = kv_ref[:, :MLA_NOPE_DIM]
    kcat_scr[:, MLA_NOPE_DIM:] = kpe_ref[...]
    row = lax.broadcasted_iota(jnp.int32, (tq, tq), 0)
    col = lax.broadcasted_iota(jnp.int32, (tq, tq), 1)
    tri = col <= row
    nt = (((1,), (1,)), ((), ()))
    for qi in range(s_len // tq):
        lo, hi = qi * tq, (qi + 1) * tq
        q = q_ref[lo:hi, :]
        s_diag = lax.dot_general(q, kcat_scr[lo:hi, :], nt, preferred_element_type=F32)
        s_diag = jnp.where(tri, s_diag, NEG)
        m = jnp.max(s_diag, axis=-1, keepdims=True)
        if qi > 0:
            s_past = lax.dot_general(q, kcat_scr[:lo, :], nt, preferred_element_type=F32)
            m = jnp.maximum(m, jnp.max(s_past, axis=-1, keepdims=True))
            p_past = jnp.exp(s_past - m)
        p_diag = jnp.exp(s_diag - m)
        l = jnp.sum(p_diag, axis=-1, keepdims=True)
        o = jnp.dot(p_diag.astype(BF16), kv_ref[lo:hi, MLA_NOPE_DIM:], preferred_element_type=F32)
        if qi > 0:
            l = l + jnp.sum(p_past, axis=-1, keepdims=True)
            o = o + jnp.dot(p_past.astype(BF16), kv_ref[:lo, MLA_NOPE_DIM:], preferred_element_type=F32)
        o_ref[lo:hi, :] = (o / l).astype(o_ref.dtype)


def _mla_attention(q, kv, z, batch, seq):
    t = batch * seq
    kpe_block = Z_OFF_KPE // LANES
    return pl.pallas_call(
        functools.partial(_mla_kernel, tq=256),
        grid=(batch, MLA_HEADS),
        in_specs=[pl.BlockSpec((seq, MLA_HEAD_PAD), lambda b, h: (b, h)),
                  pl.BlockSpec((seq, MLA_NOPE_DIM + MLA_V_DIM), lambda b, h: (b, h)),
                  pl.BlockSpec((seq, LANES), lambda b, h: (b, kpe_block))],
        out_specs=pl.BlockSpec((seq, MLA_V_DIM), lambda b, h: (b, h)),
        out_shape=jax.ShapeDtypeStruct((t, MLA_HEADS * MLA_V_DIM), BF16),
        scratch_shapes=[pltpu.VMEM((seq, MLA_HEAD_PAD), BF16)],
        compiler_params=_cparams(("parallel", "arbitrary")),
        name="mla_attention",
    )(q, kv, z)


def _swa_kernel(sinks_ref, q_ref, kv_ref, o_ref, klo, khi, vlo, vhi):
    h = pl.program_id(1)
    s_len = q_ref.shape[0]
    w = SWA_WINDOW
    pairs = SWA_GROUP // 2
    kv = kv_ref[...].astype(F32)
    lane = lax.broadcasted_iota(jnp.int32, kv.shape, 1)
    low = lane < SWA_HEAD_DIM
    rolled = pltpu.roll(kv, SWA_HEAD_DIM, 1)
    pad = jnp.zeros((w, LANES), BF16)
    for scr, val in ((klo, jnp.where(low, kv, 0.0)), (khi, jnp.where(low, 0.0, rolled)),
                     (vlo, jnp.where(low, rolled, 0.0)), (vhi, jnp.where(low, 0.0, kv))):
        scr[:w, :] = pad
        scr[w:, :] = val.astype(BF16)

    rows = pairs * w
    r = lax.broadcasted_iota(jnp.int32, (rows, 2 * w), 0) % w
    c = lax.broadcasted_iota(jnp.int32, (rows, 2 * w), 1)
    band = (c > r) & (c <= r + w)
    pair_of_row = lax.broadcasted_iota(jnp.int32, (rows, 1), 0) // w
    out_low = lax.broadcasted_iota(jnp.int32, (rows, LANES), 1) < SWA_HEAD_DIM

    def sink_column(parity):
        col = jnp.zeros((rows, 1), F32)
        for j in range(pairs):
            col = jnp.where(pair_of_row == j, sinks_ref[h * SWA_GROUP + 2 * j + parity], col)
        return col

    sink_even, sink_odd = sink_column(0), sink_column(1)
    nt = (((1,), (1,)), ((), ()))

    def block(n, mask):
        r0 = n * w if isinstance(n, int) else pl.multiple_of(n * w, w)
        qs = jnp.concatenate([q_ref[pl.ds(r0, w), j * LANES:(j + 1) * LANES] for j in range(pairs)], axis=0)

        def head(k_scr, v_scr, sink):
            s = lax.dot_general(qs, k_scr[pl.ds(r0, 2 * w), :], nt, preferred_element_type=F32)
            s = jnp.where(mask, s, NEG)
            m = jnp.maximum(jnp.max(s, axis=-1, keepdims=True), sink)
            p = jnp.exp(s - m)
            l = jnp.sum(p, axis=-1, keepdims=True) + jnp.exp(sink - m)
            o = jnp.dot(p.astype(BF16), v_scr[pl.ds(r0, 2 * w), :], preferred_element_type=F32)
            return o, l

        o_even, l_even = head(klo, vlo, sink_even)
        o_odd, l_odd = head(khi, vhi, sink_odd)
        o = (o_even + o_odd) / jnp.where(out_low, l_even, l_odd)
        for j in range(pairs):
            o_ref[pl.ds(r0, w), j * LANES:(j + 1) * LANES] = o[j * w:(j + 1) * w].astype(o_ref.dtype)

    block(0, band & (c >= w))

    def later_block(n, carry):
        block(n, band)
        return carry

    lax.fori_loop(1, s_len // w, later_block, 0, unroll=3)


def _swa_attention(z, sinks, batch, seq):
    t = batch * seq
    gw = SWA_GROUP * SWA_HEAD_DIM
    q_block0 = Z_OFF_SQ // gw
    kv_block0 = Z_OFF_SKV // LANES
    return pl.pallas_call(
        _swa_kernel,
        grid_spec=pltpu.PrefetchScalarGridSpec(
            num_scalar_prefetch=1,
            grid=(batch, SWA_KV_HEADS),
            in_specs=[pl.BlockSpec((seq, gw), lambda b, h, s: (b, q_block0 + h)),
                      pl.BlockSpec((seq, LANES), lambda b, h, s: (b, kv_block0 + h))],
            out_specs=pl.BlockSpec((seq, gw), lambda b, h, s: (b, h)),
            scratch_shapes=[pltpu.VMEM((seq + SWA_WINDOW, LANES), BF16)] * 4),
        out_shape=jax.ShapeDtypeStruct((t, SWA_HEADS * SWA_HEAD_DIM), BF16),
        compiler_params=_cparams(("parallel", "arbitrary")),
        name="swa_attention",
    )(sinks.astype(F32), z, z)


def _out_proj_kernel(x_ref, a_ref, s_ref, wt_ref, wb_ref, h_ref):
    h_ref[...] = (x_ref[...] + jnp.dot(a_ref[...], wt_ref[...].astype(BF16), preferred_element_type=F32)
                  + jnp.dot(s_ref[...], wb_ref[...].astype(BF16), preferred_element_type=F32))


def _out_proj(x2, mla_o, swa_o, w_out, *, tm=1024, tn=512):
    t, d = x2.shape
    half = mla_o.shape[1]
    return pl.pallas_call(
        _out_proj_kernel,
        grid=(t // tm, d // tn),
        in_specs=[pl.BlockSpec((tm, tn), lambda i, j: (i, j)),
                  pl.BlockSpec((tm, half), lambda i, j: (i, 0)),
                  pl.BlockSpec((tm, half), lambda i, j: (i, 0)),
                  pl.BlockSpec((half, tn), lambda i, j: (0, j)),
                  pl.BlockSpec((half, tn), lambda i, j: (1, j))],
        out_specs=pl.BlockSpec((tm, tn), lambda i, j: (i, j)),
        out_shape=jax.ShapeDtypeStruct((t, d), F32),
        compiler_params=_cparams(("parallel", "parallel")),
        name="out_proj",
    )(x2, mla_o, swa_o, w_out, w_out)


ROW_CHUNKS = D_MODEL // (2 * LANES)
ROW_PITCH = 24
U32 = jnp.uint32
HIGH_HALF = 0xFFFF0000


def _pack_pair(lo, hi):
    lo_bits = lax.bitcast_convert_type(lo.astype(BF16).astype(F32), U32)
    hi_bits = lax.bitcast_convert_type(hi.astype(BF16).astype(F32), U32)
    return (lo_bits >> 16) | (hi_bits & U32(HIGH_HALF))


def _unpack_pair(words):
    return (lax.bitcast_convert_type(words << 16, F32),
            lax.bitcast_convert_type(words & U32(HIGH_HALF), F32))


def _chunk_cols(c):
    return (slice(2 * c * LANES, (2 * c + 1) * LANES), slice((2 * c + 1) * LANES, (2 * c + 2) * LANES))


def _split_bf16(v):
    hi = v.astype(BF16)
    return hi, (v - hi.astype(F32)).astype(BF16)


def _ffn_norm_router_kernel(h_ref, g_ref, wr_ref, br_ref, o_ref, lg_ref):
    tm = h_ref.shape[0]
    h = h_ref[...]
    inv = lax.rsqrt(jnp.mean(h * h, axis=-1, keepdims=True) + EPS)
    hg_hi, hg_lo = _split_bf16(h * g_ref[...])
    wr_hi, wr_lo = _split_bf16(wr_ref[...])
    acc = (jnp.dot(hg_hi, wr_hi, preferred_element_type=F32)
           + (jnp.dot(hg_hi, wr_lo, preferred_element_type=F32)
              + jnp.dot(hg_lo, wr_hi, preferred_element_type=F32)))
    lg_ref[...] = acc * inv + br_ref[...]
    for c in range(ROW_CHUNKS):
        lo, hi = _chunk_cols(c)
        o_ref[pl.ds(c, tm, stride=ROW_PITCH), :] = _pack_pair(h_ref[:, lo] * inv * g_ref[:, lo],
                                                              h_ref[:, hi] * inv * g_ref[:, hi])
    for c in range(ROW_CHUNKS, ROW_PITCH):
        o_ref[pl.ds(c, tm, stride=ROW_PITCH), :] = jnp.zeros((tm, LANES), o_ref.dtype)


def _ffn_norm_router(h, g, w_r, b_r, *, tm=256):
    t, d = h.shape
    return pl.pallas_call(
        _ffn_norm_router_kernel,
        grid=(t // tm,),
        in_specs=[pl.BlockSpec((tm, d), lambda i: (i, 0)),
                  pl.BlockSpec((1, d), lambda i: (0, 0)),
                  pl.BlockSpec((d, LANES), lambda i: (0, 0)),
                  pl.BlockSpec((1, LANES), lambda i: (0, 0))],
        out_specs=[pl.BlockSpec((tm * ROW_PITCH, LANES), lambda i: (i, 0)),
                   pl.BlockSpec((tm, LANES), lambda i: (i, 0))],
        out_shape=[jax.ShapeDtypeStruct((t * ROW_PITCH, LANES), U32),
                   jax.ShapeDtypeStruct((t, LANES), F32)],
        compiler_params=_cparams(("parallel",)),
        name="ffn_norm_router",
    )(h, g.reshape(1, d), w_r, b_r)


MOE_UP_STEPS = D_EXPERT // MOE_CHUNK
MOE_DOWN_STEPS = 2
MOE_STEPS = MOE_UP_STEPS + MOE_DOWN_STEPS
MOE_DOWN_TILE = D_MODEL // MOE_DOWN_STEPS


LANES_LOG2 = LANES.bit_length() - 1
IDX_TILE_ROWS = 8
IDX_TILE = IDX_TILE_ROWS * LANES
IDX_TILE_LOG2 = IDX_TILE.bit_length() - 1
IDX_WINDOW_ROWS = 2 * IDX_TILE_ROWS
assert MOE_ROWS <= IDX_TILE
ROW_UNROLL_LOG2 = 3
ROW_UNROLL = 1 << ROW_UNROLL_LOG2


def _moe_kernel(item_e, item_cnt, item_row0, dstk_ref, hn_ref, wg_ref, wu_ref, wd_ref, pairs_ref,
                idx_win, x_stage, x_bf, hid, out_stage, sem_idx, sem_in, sem_out, *, n_tokens):
    del item_e
    w = pl.program_id(0)
    s = pl.program_id(1)
    n_items = pl.num_programs(0)
    cnt = item_cnt[w]
    nsub = (cnt + MOE_SUB - 1) // MOE_SUB
    nxt_item = jnp.minimum(w + 1, n_items - 1)
    has_next = (w + 1 < n_items) & (item_cnt[nxt_item] > 0)
    cur = w & 1
    nxt = 1 - cur

    def idx_copy(item, buf):
        first = (item_row0[item] >> IDX_TILE_LOG2) * IDX_TILE_ROWS
        return pltpu.make_async_copy(dstk_ref.at[pl.ds(pl.multiple_of(first, IDX_TILE_ROWS), IDX_WINDOW_ROWS)],
                                     idx_win.at[buf], sem_idx)

    def unrolled_rows(n, one_row):
        groups = n >> ROW_UNROLL_LOG2

        def group(g, carry):
            for u in range(ROW_UNROLL):
                one_row(g * ROW_UNROLL + u)
            return carry

        def single(r, carry):
            one_row(r)
            return carry

        lax.fori_loop(0, groups, group, 0)
        lax.fori_loop(groups * ROW_UNROLL, n, single, 0)

    def for_rows(item, buf, fn):
        base = item_row0[item] & (IDX_TILE - 1)

        def one_row(r):
            off = base + r
            fn(idx_win[buf, off >> LANES_LOG2, off & (LANES - 1)], r)

        unrolled_rows(item_cnt[item], one_row)

    def row_in(tok, r):
        return pltpu.make_async_copy(hn_ref.at[pl.ds(pl.multiple_of(tok * ROW_PITCH, 8), ROW_PITCH)],
                                     x_stage.at[pl.ds(pl.multiple_of(r * ROW_PITCH, 8), ROW_PITCH)], sem_in)

    def row_out(slot, r):
        return pltpu.make_async_copy(out_stage.at[pl.ds(pl.multiple_of(r * ROW_PITCH, 8), ROW_PITCH)],
                                     pairs_ref.at[pl.ds(pl.multiple_of(slot * ROW_PITCH, 8), ROW_PITCH)], sem_out)

    def gather_start(item, buf):
        for_rows(item, buf, lambda v, r: row_in(v & (n_tokens - 1), r).start())

    def scatter_start(item, buf):
        for_rows(item, buf, lambda v, r: row_out(v, r).start())

    def wait_rows(n, copy):
        unrolled_rows(n, lambda r: copy(0, r).wait())

    @pl.when((w == 0) & (s == 0))
    def _():
        x_stage[...] = jnp.zeros(x_stage.shape, x_stage.dtype)
        out_stage[...] = jnp.zeros(out_stage.shape, out_stage.dtype)

        @pl.when(cnt > 0)
        def _():
            first_window = idx_copy(0, 0)
            first_window.start()
            first_window.wait()
            gather_start(0, 0)

    @pl.when((s == 0) & (cnt > 0))
    def _():
        wait_rows(cnt, row_in)

    @pl.when((s == 0) & has_next)
    def _():
        idx_copy(nxt_item, nxt).start()

    @pl.when((s == MOE_UP_STEPS) & (w > 0) & (cnt > 0))
    def _():
        wait_rows(item_cnt[jnp.maximum(w - 1, 0)], row_out)

    def compute(m):
        @pl.when(s == 0)
        def _():
            for c in range(ROW_CHUNKS):
                lo, hi = _chunk_cols(c)
                x_lo, x_hi = _unpack_pair(x_stage[pl.ds(c, m, stride=ROW_PITCH), :])
                x_bf[:m, lo] = x_lo.astype(BF16)
                x_bf[:m, hi] = x_hi.astype(BF16)

        @pl.when(s < MOE_UP_STEPS)
        def _():
            x = x_bf[:m, :]
            g = jnp.dot(x, wg_ref[...].astype(BF16), preferred_element_type=F32)
            u = jnp.dot(x, wu_ref[...].astype(BF16), preferred_element_type=F32)
            hid[s, :m, :] = ((g * jax.nn.sigmoid(g)) * u).astype(BF16)

        @pl.when(s >= MOE_UP_STEPS)
        def _():
            hidden = jnp.concatenate([hid[c, :m, :] for c in range(MOE_UP_STEPS)], axis=1)
            y = jnp.dot(hidden, wd_ref[...].astype(BF16), preferred_element_type=F32)
            tile_chunks = MOE_DOWN_TILE // (2 * LANES)
            base = (s - MOE_UP_STEPS) * tile_chunks
            for c in range(tile_chunks):
                lo, hi = _chunk_cols(c)
                out_stage[pl.ds(base + c, m, stride=ROW_PITCH), :] = _pack_pair(y[:, lo], y[:, hi])

    for k in range(1, MOE_ROWS // MOE_SUB + 1):
        pl.when(nsub == k)(functools.partial(compute, k * MOE_SUB))

    @pl.when((s == 1) & has_next)
    def _():
        idx_copy(nxt_item, nxt).wait()
        gather_start(nxt_item, nxt)

    @pl.when((s == MOE_STEPS - 1) & (cnt > 0))
    def _():
        scatter_start(w, cur)

        @pl.when(jnp.logical_not(has_next))
        def _():
            wait_rows(cnt, row_out)


def _moe_experts(hn_lin, w_gate, w_up, w_down, item_e, item_cnt, item_row0, dstk, n_tokens):
    n_items = item_e.shape[0]
    d = D_MODEL
    assert n_tokens & (n_tokens - 1) == 0, "token ids are unpacked from k * n_tokens + token with a mask"

    def up_chunk(s, cnt_ref, w):
        return jnp.where(cnt_ref[w] > 0, jnp.minimum(s, MOE_UP_STEPS - 1), MOE_UP_STEPS - 1)

    def down_block(w, s, e, cn):
        own = (cn[w] > 0) & (s >= MOE_UP_STEPS)
        return (jnp.where(own, e[w], e[jnp.maximum(w - 1, 0)]), 0,
                jnp.where(own, s - MOE_UP_STEPS, MOE_DOWN_STEPS - 1))

    return pl.pallas_call(
        functools.partial(_moe_kernel, n_tokens=n_tokens),
        grid_spec=pltpu.PrefetchScalarGridSpec(
            num_scalar_prefetch=3,
            grid=(n_items, MOE_STEPS),
            in_specs=[pl.BlockSpec(memory_space=pl.ANY),
                      pl.BlockSpec(memory_space=pl.ANY),
                      pl.BlockSpec((None, d, MOE_CHUNK), lambda w, s, e, cn, r0: (e[w], 0, up_chunk(s, cn, w))),
                      pl.BlockSpec((None, d, MOE_CHUNK), lambda w, s, e, cn, r0: (e[w], 0, up_chunk(s, cn, w))),
                      pl.BlockSpec((None, D_EXPERT, MOE_DOWN_TILE), lambda w, s, e, cn, r0: down_block(w, s, e, cn))],
            out_specs=pl.BlockSpec(memory_space=pl.ANY),
            scratch_shapes=[pltpu.SMEM((2, IDX_WINDOW_ROWS, LANES), jnp.int32),
                            pltpu.VMEM((MOE_ROWS * ROW_PITCH, LANES), U32),
                            pltpu.VMEM((MOE_ROWS, d), BF16),
                            pltpu.VMEM((MOE_UP_STEPS, MOE_ROWS, MOE_CHUNK), BF16),
                            pltpu.VMEM((MOE_ROWS * ROW_PITCH, LANES), U32),
                            pltpu.SemaphoreType.DMA(()),
                            pltpu.SemaphoreType.DMA(()),
                            pltpu.SemaphoreType.DMA(())]),
        out_shape=jax.ShapeDtypeStruct((TOP_K * n_tokens * ROW_PITCH, LANES), U32),
        compiler_params=_cparams(("arbitrary", "arbitrary")),
        name="moe_experts",
    )(item_e, item_cnt, item_row0, dstk, hn_lin, w_gate, w_up, w_down)


def _combine_kernel(h_ref, p0_ref, p1_ref, gate_ref, g_ref, o_ref):
    tm, d = h_ref.shape
    g0 = gate_ref[:, 0:1]
    g1 = gate_ref[:, 1:2]
    ss = jnp.zeros((tm, 1), F32)
    for c in range(ROW_CHUNKS):
        a = _unpack_pair(p0_ref[pl.ds(c, tm, stride=ROW_PITCH), :])
        b = _unpack_pair(p1_ref[pl.ds(c, tm, stride=ROW_PITCH), :])
        for sl, a_half, b_half in zip(_chunk_cols(c), a, b):
            y = h_ref[:, sl] + (a_half * g0 + b_half * g1)
            o_ref[:, sl] = y
            ss = ss + jnp.sum(y * y, axis=-1, keepdims=True)
    inv = lax.rsqrt(ss / d + EPS)
    o_ref[...] = o_ref[...] * inv * g_ref[...]


def _combine(h, pairs, gates, g_final, *, tm=256):
    t, d = h.shape
    k1 = t // tm
    return pl.pallas_call(
        _combine_kernel,
        grid=(t // tm,),
        in_specs=[pl.BlockSpec((tm, d), lambda i: (i, 0)),
                  pl.BlockSpec((tm * ROW_PITCH, LANES), lambda i: (i, 0)),
                  pl.BlockSpec((tm * ROW_PITCH, LANES), lambda i: (k1 + i, 0)),
                  pl.BlockSpec((tm, TOP_K), lambda i: (i, 0)),
                  pl.BlockSpec((1, d), lambda i: (0, 0))],
        out_specs=pl.BlockSpec((tm, d), lambda i: (i, 0)),
        out_shape=jax.ShapeDtypeStruct((t, d), F32),
        compiler_params=_cparams(("parallel",)),
        name="combine_final_norm",
    )(h, pairs, pairs, gates, g_final.reshape(1, d))


def _route(logits, t):
    group_prob = jax.nn.softmax(logits[:, :N_GROUPS], axis=-1)
    g_sel = jnp.argmax(group_prob, axis=-1)
    g_p = jnp.max(group_prob, axis=-1, keepdims=True)
    e_logits = logits[:, N_GROUPS:N_GROUPS + N_EXPERTS]
    in_group = (jnp.arange(N_EXPERTS) // EXPERTS_PER_GROUP)[None, :] == g_sel[:, None]
    e_prob = jax.nn.softmax(jnp.where(in_group, e_logits, -jnp.inf), axis=-1)
    key = jnp.where(in_group, e_prob, -1.0)
    i1 = jnp.argmax(key, axis=-1)
    key2 = jnp.where(jnp.arange(N_EXPERTS)[None, :] == i1[:, None], -2.0, key)
    i2 = jnp.argmax(key2, axis=-1)
    top_p = jnp.stack([jnp.max(key, axis=-1), jnp.max(key2, axis=-1)], axis=-1)
    expert_idx = jnp.stack([i1, i2], axis=-1)
    assert TOP_K == 2
    gates = g_p * top_p / jnp.sum(top_p, axis=-1, keepdims=True)

    tk = t * TOP_K
    n_items_max = tk // MOE_ROWS + N_EXPERTS
    flat_e = expert_idx.reshape(tk).astype(jnp.int32)
    order = jnp.argsort(flat_e).astype(jnp.int32)
    dstk = (order % TOP_K) * t + order // TOP_K
    dstk = jnp.concatenate([dstk, jnp.zeros((2 * IDX_TILE,), jnp.int32)]).reshape(-1, LANES)
    counts = jnp.sum((flat_e[:, None] == jnp.arange(N_EXPERTS, dtype=jnp.int32)[None, :]).astype(jnp.int32), axis=0)
    start = jnp.cumsum(counts) - counts
    items_of = (counts + MOE_ROWS - 1) // MOE_ROWS
    item_end = jnp.cumsum(items_of)
    wi = jnp.arange(n_items_max, dtype=jnp.int32)
    e_w = jnp.minimum(jnp.sum((item_end[None, :] <= wi[:, None]).astype(jnp.int32), axis=1), N_EXPERTS - 1)
    part = wi - (item_end[e_w] - items_of[e_w])
    active = wi < item_end[-1]
    item_cnt = jnp.where(active, jnp.clip(counts[e_w] - part * MOE_ROWS, 0, MOE_ROWS), 0).astype(jnp.int32)
    item_row0 = jnp.where(active, start[e_w] + part * MOE_ROWS, 0).astype(jnp.int32)
    item_e = jnp.where(active, e_w, e_w[jnp.maximum(item_end[-1] - 1, 0)]).astype(jnp.int32)
    return gates.astype(F32), item_e, item_cnt, item_row0, dstk


def _column_layout_kernel(w_ref, o_ref, *, pieces):
    w = w_ref[...]
    cols = [jnp.zeros((w.shape[0], width), w.dtype) if start is None else w[:, start:start + width]
            for start, width in pieces]
    o_ref[...] = jnp.concatenate(cols, axis=1).astype(o_ref.dtype)


def _column_layout_from_transpose_kernel(wt_ref, o_ref, *, pieces):
    wt = wt_ref[...]
    rows = [jnp.zeros((width, wt.shape[1]), wt.dtype) if start is None else wt[start:start + width, :]
            for start, width in pieces]
    o_ref[...] = jnp.concatenate(rows, axis=0).T.astype(o_ref.dtype)


def _column_layout(w, layer, pieces, *, tk=256, name, read_transposed=False):
    _, d, n = w.shape
    n_out = sum(width for _, width in pieces)
    if read_transposed:
        body = functools.partial(_column_layout_from_transpose_kernel, pieces=tuple(pieces))
        w = jnp.swapaxes(w, 1, 2)
        in_spec = pl.BlockSpec((None, n, tk), lambda i: (layer, 0, i))
    else:
        body = functools.partial(_column_layout_kernel, pieces=tuple(pieces))
        in_spec = pl.BlockSpec((None, tk, n), lambda i: (layer, i, 0))
    return pl.pallas_call(
        body,
        grid=(d // tk,),
        in_specs=[in_spec],
        out_specs=pl.BlockSpec((tk, n_out), lambda i: (i, 0)),
        out_shape=jax.ShapeDtypeStruct((d, n_out), BF16),
        compiler_params=_cparams(("parallel",)),
        name=name,
    )(w)


def _prep_in_weight(w_in, layer):
    off_kv = MLA_Q_RANK + MLA_KV_RANK
    off_kpe = off_kv + MLA_ROPE_DIM
    off_sq = off_kpe + SWA_HEADS * SWA_HEAD_DIM
    off_sk = off_sq + SWA_KV_HEADS * SWA_HEAD_DIM
    pieces = [(0, off_kv), (off_kpe, off_sq - off_kpe)]
    for h in range(SWA_KV_HEADS):
        pieces += [(off_sq + h * SWA_HEAD_DIM, SWA_HEAD_DIM), (off_sk + h * SWA_HEAD_DIM, SWA_HEAD_DIM)]
    pieces += [(off_kv, MLA_ROPE_DIM), (None, LANES - MLA_ROPE_DIM)]
    return _column_layout(w_in, layer, pieces, name="in_weight_layout", read_transposed=True)


def _prep_q_weight(w_q_up, layer):
    pieces = []
    for h in range(MLA_HEADS):
        pieces += [(h * MLA_QK_DIM, MLA_QK_DIM), (None, MLA_HEAD_PAD - MLA_QK_DIM)]
    return _column_layout(w_q_up, layer, pieces, name="q_weight_layout")


def kernel(x, positions, g_attn_norm, w_in, g_q_norm, w_q_up, g_kv_norm, w_kv_up, sinks, w_out,
           g_ffn_norm, w_router_group, b_router_group, w_router_expert, b_router_expert,
           w_expert_gate, w_expert_up, w_expert_down, g_final_norm):
    batch, seq, d = x.shape
    t = batch * seq
    assert g_attn_norm.shape[0] == 1, "the final rmsnorm is fused into the (single) layer's combine"
    h = x.reshape(t, d)
    cos, sin = _rope_tables(positions)
    for l in range(g_attn_norm.shape[0]):
        z_kinds = ([ROPE_NONE] * (Z_OFF_SQ // LANES) + [ROPE_FULL] * ((Z_OFF_SKV - Z_OFF_SQ) // LANES)
                   + [ROPE_HALF] * ((Z_WIDTH - Z_OFF_SKV) // LANES))
        z_scales = [SWA_HEAD_DIM ** -0.5 if k == ROPE_FULL else 1.0 for k in z_kinds]
        z = _matmul_rope(_rmsnorm(h, g_attn_norm[l]), _prep_in_weight(w_in, l), cos, sin,
                         tm=512, tn=Z_WIDTH // 3, kinds=z_kinds, scales=z_scales, name="in_proj")
        q_kinds = [ROPE_NONE, ROPE_HALF] * MLA_HEADS
        q = _norm_matmul(z, Z_OFF_CQ // MLA_Q_RANK, MLA_Q_RANK, g_q_norm[l], _prep_q_weight(w_q_up, l), cos, sin,
                         tm=1024, tn=1024, kinds=q_kinds, scales=[MLA_QK_DIM ** -0.5] * len(q_kinds), name="q_up")
        kv_kinds = [ROPE_NONE] * (w_kv_up.shape[2] // LANES)
        kv = _norm_matmul(z, Z_OFF_CKV // MLA_KV_RANK, MLA_KV_RANK, g_kv_norm[l], w_kv_up[l], cos, sin,
                          tm=1024, tn=1024, kinds=kv_kinds, scales=[1.0] * len(kv_kinds), name="kv_up")
        mla_o = _mla_attention(q, kv, z, batch, seq)
        swa_o = _swa_attention(z, sinks[l], batch, seq)
        w_r = jnp.concatenate([w_router_group[l], w_router_expert[l],
                               jnp.zeros((d, LANES - N_GROUPS - N_EXPERTS), F32)], axis=1)
        b_r = jnp.concatenate([b_router_group[l], b_router_expert[l],
                               jnp.zeros((LANES - N_GROUPS - N_EXPERTS,), F32)]).reshape(1, LANES)
        h = _out_proj(h, mla_o, swa_o, w_out[l])
        hn, logits = _ffn_norm_router(h, g_ffn_norm[l], w_r, b_r)
        gates, item_e, item_cnt, item_row0, dstk = _route(logits, t)
        pairs = _moe_experts(hn, w_expert_gate[l], w_expert_up[l], w_expert_down[l],
                             item_e, item_cnt, item_row0, dstk, t)
        out = _combine(h, pairs, gates, g_final_norm)
    return out.reshape(batch, seq, d)
```

```python
import functools

import jax
import jax.numpy as jnp
from jax import lax
from jax.experimental import pallas as pl
from jax.experimental.pallas import tpu as pltpu

F32 = jnp.float32
BF16 = jnp.bfloat16

D_MODEL = 4096
EPS = 1e-6
ROPE_THETA = 10000.0
NEG = -1e30
MLA_HEADS = 16
MLA_Q_RANK = 1024
MLA_KV_RANK = 512
MLA_NOPE_DIM = 128
MLA_ROPE_DIM = 64
MLA_V_DIM = 128
MLA_QK_DIM = MLA_NOPE_DIM + MLA_ROPE_DIM
MLA_HEAD_PAD = 256
SWA_HEADS = 32
SWA_KV_HEADS = 4
SWA_GROUP = SWA_HEADS // SWA_KV_HEADS
SWA_HEAD_DIM = 64
SWA_WINDOW = 128
N_GROUPS = 8
EXPERTS_PER_GROUP = 8
N_EXPERTS = N_GROUPS * EXPERTS_PER_GROUP
TOP_K = 2
D_EXPERT = 1024

LANES = 128
KV_KPE_BLOCK = SWA_KV_HEADS
ROPE_NONE, ROPE_FULL, ROPE_HALF = 0, 1, 2

MOE_ROWS = 512
MOE_SUB = 128
MOE_CHUNK = 256
VMEM_LIMIT = 56 * 1024 * 1024


def _cparams(sem, vmem=VMEM_LIMIT):
    return pltpu.CompilerParams(dimension_semantics=sem, vmem_limit_bytes=vmem)


def _swap32(x):
    lane = lax.broadcasted_iota(jnp.int32, x.shape, 1)
    fwd = pltpu.roll(x, LANES - 32, 1)
    bwd = pltpu.roll(x, 32, 1)
    return jnp.where((lane % 64) < 32, fwd, bwd)


def _rope_tile(x, cos, sin, kind):
    if kind == ROPE_NONE:
        return x
    if kind == ROPE_HALF:
        lane = lax.broadcasted_iota(jnp.int32, x.shape, 1)
        cos = jnp.where(lane < 64, cos, 1.0)
        sin = jnp.where(lane < 64, sin, 0.0)
    return x * cos + _swap32(x) * sin


def _rope_table_kernel(pos_ref, invf_ref, sign_ref, cos_ref, sin_ref):
    ang = pos_ref[...].astype(F32) * invf_ref[...]
    cos_ref[...] = jnp.cos(ang)
    sin_ref[...] = jnp.sin(ang) * sign_ref[...]


def _rope_tables(positions):
    t = positions.size
    pos = positions.reshape(t, 1)
    half = SWA_HEAD_DIM // 2
    inv_freq = ROPE_THETA ** (-jnp.arange(0, SWA_HEAD_DIM, 2, dtype=F32) / SWA_HEAD_DIM)
    lane = jnp.arange(LANES)
    invf = inv_freq[lane % half].reshape(1, LANES)
    sign = jnp.where((lane % SWA_HEAD_DIM) < half, -1.0, 1.0).astype(F32).reshape(1, LANES)
    tm = 1024
    return pl.pallas_call(
        _rope_table_kernel,
        grid=(t // tm,),
        in_specs=[pl.BlockSpec((tm, 1), lambda i: (i, 0)),
                  pl.BlockSpec((1, LANES), lambda i: (0, 0)),
                  pl.BlockSpec((1, LANES), lambda i: (0, 0))],
        out_specs=[pl.BlockSpec((tm, LANES), lambda i: (i, 0))] * 2,
        out_shape=[jax.ShapeDtypeStruct((t, LANES), F32)] * 2,
        compiler_params=_cparams(("parallel",)),
        name="rope_tables",
    )(pos, invf, sign)


def _rmsnorm_kernel(x_ref, g_ref, o_ref):
    xf = x_ref[...].astype(F32)
    ms = jnp.mean(xf * xf, axis=-1, keepdims=True)
    o_ref[...] = (xf * lax.rsqrt(ms + EPS) * g_ref[...]).astype(o_ref.dtype)


def _rmsnorm(x, g, *, tm=512):
    t, d = x.shape
    return pl.pallas_call(
        _rmsnorm_kernel,
        grid=(t // tm,),
        in_specs=[pl.BlockSpec((tm, d), lambda i: (i, 0)), pl.BlockSpec((1, d), lambda i: (0, 0))],
        out_specs=pl.BlockSpec((tm, d), lambda i: (i, 0)),
        out_shape=jax.ShapeDtypeStruct((t, d), BF16),
        compiler_params=_cparams(("parallel",)),
        name="attn_norm",
    )(x, g.reshape(1, d))


def _rope_epilogue(acc, cos_ref, sin_ref, o_ref, j, kinds, scales):
    def epilogue(tile_kinds, tile_scales):
        needs_rope = any(k != ROPE_NONE for k in tile_kinds)
        cos = cos_ref[...] if needs_rope else None
        sin = sin_ref[...] if needs_rope else None
        for c, (kind, scale) in enumerate(zip(tile_kinds, tile_scales)):
            blk = _rope_tile(acc[:, c * LANES:(c + 1) * LANES], cos, sin, kind)
            if scale != 1.0:
                blk = blk * scale
            o_ref[:, c * LANES:(c + 1) * LANES] = blk.astype(o_ref.dtype)

    if all(k == kinds[0] for k in kinds) and all(s == scales[0] for s in scales):
        epilogue(kinds[0], scales[0])
    else:
        for jj in range(len(kinds)):
            pl.when(j == jj)(functools.partial(epilogue, kinds[jj], scales[jj]))


def _per_tile(values, n_j, per):
    return tuple(tuple(values[jj * per:(jj + 1) * per]) for jj in range(n_j))


def _matmul_rope_kernel(x_ref, w_ref, cos_ref, sin_ref, o_ref, *, kinds, scales):
    acc = jnp.dot(x_ref[...], w_ref[...], preferred_element_type=F32)
    _rope_epilogue(acc, cos_ref, sin_ref, o_ref, 0, (kinds,), (scales,))


def _matmul_rope(x, w, cos, sin, *, tm, kinds, scales, name):
    t, k = x.shape
    n = w.shape[1]
    return pl.pallas_call(
        functools.partial(_matmul_rope_kernel, kinds=tuple(kinds), scales=tuple(scales)),
        grid=(t // tm,),
        in_specs=[pl.BlockSpec((tm, k), lambda i: (i, 0)),
                  pl.BlockSpec((k, n), lambda i: (0, 0), pipeline_mode=pl.Buffered(1)),
                  pl.BlockSpec((tm, LANES), lambda i: (i, 0)),
                  pl.BlockSpec((tm, LANES), lambda i: (i, 0))],
        out_specs=pl.BlockSpec((tm, n), lambda i: (i, 0)),
        out_shape=jax.ShapeDtypeStruct((t, n), BF16),
        compiler_params=_cparams(("parallel",)),
        name=name,
    )(x, w, cos, sin)


def _norm_matmul_kernel(x_ref, g_ref, w_ref, cos_ref, sin_ref, o_ref, n_scr, *, kinds, scales):
    j = pl.program_id(1)

    @pl.when(j == 0)
    def _():
        xf = x_ref[...].astype(F32)
        ms = jnp.mean(xf * xf, axis=-1, keepdims=True)
        n_scr[...] = (xf * lax.rsqrt(ms + EPS) * g_ref[...]).astype(BF16)

    acc = jnp.dot(n_scr[...], w_ref[...].astype(BF16), preferred_element_type=F32)
    _rope_epilogue(acc, cos_ref, sin_ref, o_ref, j, kinds, scales)


def _norm_matmul(x, x_col_block, k, g, w, cos, sin, *, tm, tn, kinds, scales, name):
    t = x.shape[0]
    n = w.shape[1]
    n_j = n // tn
    kinds = _per_tile(kinds, n_j, tn // LANES)
    scales = _per_tile(scales, n_j, tn // LANES)
    return pl.pallas_call(
        functools.partial(_norm_matmul_kernel, kinds=kinds, scales=scales),
        grid=(t // tm, n_j),
        in_specs=[pl.BlockSpec((tm, k), lambda i, j: (i, x_col_block)),
                  pl.BlockSpec((1, k), lambda i, j: (0, 0)),
                  pl.BlockSpec((k, tn), lambda i, j: (0, j)),
                  pl.BlockSpec((tm, LANES), lambda i, j: (i, 0)),
                  pl.BlockSpec((tm, LANES), lambda i, j: (i, 0))],
        out_specs=pl.BlockSpec((tm, tn), lambda i, j: (i, j)),
        out_shape=jax.ShapeDtypeStruct((t, n), BF16),
        scratch_shapes=[pltpu.VMEM((tm, k), BF16)],
        compiler_params=_cparams(("parallel", "arbitrary")),
        name=name,
    )(x, g.reshape(1, k), w, cos, sin)


def _mla_kernel(q_ref, kv_ref, kpe_ref, o_ref, kcat_scr, *, tq):
    s_len = q_ref.shape[0]
    kcat_scr[:, :MLA_NOPE_DIM] = kv_ref[:, :MLA_NOPE_DIM]
    kcat_scr[:, MLA_NOPE_DIM:] = kpe_ref[...]
    row = lax.broadcasted_iota(jnp.int32, (tq, tq), 0)
    col = lax.broadcasted_iota(jnp.int32, (tq, tq), 1)
    tri = col <= row
    nt = (((1,), (1,)), ((), ()))
    for qi in range(s_len // tq):
        lo, hi = qi * tq, (qi + 1) * tq
        q = q_ref[lo:hi, :]
        s_diag = lax.dot_general(q, kcat_scr[lo:hi, :], nt, preferred_element_type=F32)
        s_diag = jnp.where(tri, s_diag, NEG)
        m = jnp.max(s_diag, axis=-1, keepdims=True)
        if qi > 0:
            s_past = lax.dot_general(q, kcat_scr[:lo, :], nt, preferred_element_type=F32)
            m = jnp.maximum(m, jnp.max(s_past, axis=-1, keepdims=True))
            p_past = jnp.exp(s_past - m)
        p_diag = jnp.exp(s_diag - m)
        l = jnp.sum(p_diag, axis=-1, keepdims=True)
        o = jnp.dot(p_diag.astype(BF16), kv_ref[lo:hi, MLA_NOPE_DIM:], preferred_element_type=F32)
        if qi > 0:
            l = l + jnp.sum(p_past, axis=-1, keepdims=True)
            o = o + jnp.dot(p_past.astype(BF16), kv_ref[:lo, MLA_NOPE_DIM:], preferred_element_type=F32)
        o_ref[lo:hi, :] = (o / l).astype(o_ref.dtype)


def _mla_attention(q, kv, z_kv, batch, seq):
    t = batch * seq
    kpe_block = KV_KPE_BLOCK
    return pl.pallas_call(
        functools.partial(_mla_kernel, tq=256),
        grid=(batch, MLA_HEADS),
        in_specs=[pl.BlockSpec((seq, MLA_HEAD_PAD), lambda b, h: (b, h)),
                  pl.BlockSpec((seq, MLA_NOPE_DIM + MLA_V_DIM), lambda b, h: (b, h)),
                  pl.BlockSpec((seq, LANES), lambda b, h: (b, kpe_block))],
        out_specs=pl.BlockSpec((seq, MLA_V_DIM), lambda b, h: (b, h)),
        out_shape=jax.ShapeDtypeStruct((t, MLA_HEADS * MLA_V_DIM), BF16),
        scratch_shapes=[pltpu.VMEM((seq, MLA_HEAD_PAD), BF16)],
        compiler_params=_cparams(("parallel", "arbitrary")),
        name="mla_attention",
    )(q, kv, z_kv)


def _swa_kernel(sinks_ref, q_ref, kv_ref, o_ref, klo, khi, vlo, vhi):
    h = pl.program_id(1)
    s_len = q_ref.shape[0]
    w = SWA_WINDOW
    pairs = SWA_GROUP // 2
    kv = kv_ref[...].astype(F32)
    lane = lax.broadcasted_iota(jnp.int32, kv.shape, 1)
    low = lane < SWA_HEAD_DIM
    rolled = pltpu.roll(kv, SWA_HEAD_DIM, 1)
    pad = jnp.zeros((w, LANES), BF16)
    for scr, val in ((klo, jnp.where(low, kv, 0.0)), (khi, jnp.where(low, 0.0, rolled)),
                     (vlo, jnp.where(low, rolled, 0.0)), (vhi, jnp.where(low, 0.0, kv))):
        scr[:w, :] = pad
        scr[w:, :] = val.astype(BF16)

    rows = pairs * w
    r = lax.broadcasted_iota(jnp.int32, (rows, 2 * w), 0) % w
    c = lax.broadcasted_iota(jnp.int32, (rows, 2 * w), 1)
    band = (c > r) & (c <= r + w)
    pair_of_row = lax.broadcasted_iota(jnp.int32, (rows, 1), 0) // w
    out_low = lax.broadcasted_iota(jnp.int32, (rows, LANES), 1) < SWA_HEAD_DIM

    def sink_column(parity):
        col = jnp.zeros((rows, 1), F32)
        for j in range(pairs):
            col = jnp.where(pair_of_row == j, sinks_ref[h * SWA_GROUP + 2 * j + parity], col)
        return col

    sink_even, sink_odd = sink_column(0), sink_column(1)
    nt = (((1,), (1,)), ((), ()))

    def block(n, mask):
        r0 = n * w if isinstance(n, int) else pl.multiple_of(n * w, w)
        qs = jnp.concatenate([q_ref[pl.ds(r0, w), j * LANES:(j + 1) * LANES] for j in range(pairs)], axis=0)

        def head(k_scr, v_scr, sink):
            s = lax.dot_general(qs, k_scr[pl.ds(r0, 2 * w), :], nt, preferred_element_type=F32)
            s = jnp.where(mask, s, NEG)
            m = jnp.maximum(jnp.max(s, axis=-1, keepdims=True), sink)
            p = jnp.exp(s - m)
            l = jnp.sum(p, axis=-1, keepdims=True) + jnp.exp(sink - m)
            o = jnp.dot(p.astype(BF16), v_scr[pl.ds(r0, 2 * w), :], preferred_element_type=F32)
            return o, l

        o_even, l_even = head(klo, vlo, sink_even)
        o_odd, l_odd = head(khi, vhi, sink_odd)
        o = (o_even + o_odd) / jnp.where(out_low, l_even, l_odd)
        for j in range(pairs):
            o_ref[pl.ds(r0, w), j * LANES:(j + 1) * LANES] = o[j * w:(j + 1) * w].astype(o_ref.dtype)

    block(0, band & (c >= w))

    def later_block(n, carry):
        block(n, band)
        return carry

    lax.fori_loop(1, s_len // w, later_block, 0, unroll=3)


def _swa_attention(z_q, z_kv, sinks, batch, seq):
    t = batch * seq
    gw = SWA_GROUP * SWA_HEAD_DIM
    return pl.pallas_call(
        _swa_kernel,
        grid_spec=pltpu.PrefetchScalarGridSpec(
            num_scalar_prefetch=1,
            grid=(batch, SWA_KV_HEADS),
            in_specs=[pl.BlockSpec((seq, gw), lambda b, h, s: (b, h)),
                      pl.BlockSpec((seq, LANES), lambda b, h, s: (b, h))],
            out_specs=pl.BlockSpec((seq, gw), lambda b, h, s: (b, h)),
            scratch_shapes=[pltpu.VMEM((seq + SWA_WINDOW, LANES), BF16)] * 4),
        out_shape=jax.ShapeDtypeStruct((t, SWA_HEADS * SWA_HEAD_DIM), BF16),
        compiler_params=_cparams(("parallel", "arbitrary")),
        name="swa_attention",
    )(sinks.astype(F32), z_q, z_kv)


def _out_proj_kernel(x_ref, a_ref, s_ref, wt_ref, wb_ref, h_ref):
    h_ref[...] = (x_ref[...] + jnp.dot(a_ref[...], wt_ref[...].astype(BF16), preferred_element_type=F32)
                  + jnp.dot(s_ref[...], wb_ref[...].astype(BF16), preferred_element_type=F32))


def _out_proj(x2, mla_o, swa_o, w_out, *, tm=1024, tn=512):
    t, d = x2.shape
    half = mla_o.shape[1]
    return pl.pallas_call(
        _out_proj_kernel,
        grid=(t // tm, d // tn),
        in_specs=[pl.BlockSpec((tm, tn), lambda i, j: (i, j)),
                  pl.BlockSpec((tm, half), lambda i, j: (i, 0)),
                  pl.BlockSpec((tm, half), lambda i, j: (i, 0)),
                  pl.BlockSpec((half, tn), lambda i, j: (0, j)),
                  pl.BlockSpec((half, tn), lambda i, j: (1, j))],
        out_specs=pl.BlockSpec((tm, tn), lambda i, j: (i, j)),
        out_shape=jax.ShapeDtypeStruct((t, d), F32),
        compiler_params=_cparams(("parallel", "parallel")),
        name="out_proj",
    )(x2, mla_o, swa_o, w_out, w_out)


ROW_CHUNKS = D_MODEL // (2 * LANES)
ROW_PITCH = 24
U32 = jnp.uint32
HIGH_HALF = 0xFFFF0000


def _pack_pair(lo, hi):
    lo_bits = lax.bitcast_convert_type(lo.astype(BF16).astype(F32), U32)
    hi_bits = lax.bitcast_convert_type(hi.astype(BF16).astype(F32), U32)
    return (lo_bits >> 16) | (hi_bits & U32(HIGH_HALF))


def _unpack_pair(words):
    return (lax.bitcast_convert_type(words << 16, F32),
            lax.bitcast_convert_type(words & U32(HIGH_HALF), F32))


def _chunk_cols(c):
    return (slice(2 * c * LANES, (2 * c + 1) * LANES), slice((2 * c + 1) * LANES, (2 * c + 2) * LANES))


def _split_bf16(v):
    hi = v.astype(BF16)
    return hi, (v - hi.astype(F32)).astype(BF16)


def _ffn_norm_router_kernel(h_ref, g_ref, wr_ref, br_ref, o_ref, lg_ref):
    tm = h_ref.shape[0]
    h = h_ref[...]
    inv = lax.rsqrt(jnp.mean(h * h, axis=-1, keepdims=True) + EPS)
    hg_hi, hg_lo = _split_bf16(h * g_ref[...])
    wr_hi, wr_lo = _split_bf16(wr_ref[...])
    acc = (jnp.dot(hg_hi, wr_hi, preferred_element_type=F32)
           + (jnp.dot(hg_hi, wr_lo, preferred_element_type=F32)
              + jnp.dot(hg_lo, wr_hi, preferred_element_type=F32)))
    lg_ref[...] = acc * inv + br_ref[...]
    for c in range(ROW_CHUNKS):
        lo, hi = _chunk_cols(c)
        o_ref[pl.ds(c, tm, stride=ROW_PITCH), :] = _pack_pair(h_ref[:, lo] * inv * g_ref[:, lo],
                                                              h_ref[:, hi] * inv * g_ref[:, hi])
    for c in range(ROW_CHUNKS, ROW_PITCH):
        o_ref[pl.ds(c, tm, stride=ROW_PITCH), :] = jnp.zeros((tm, LANES), o_ref.dtype)


def _ffn_norm_router(h, g, w_r, b_r, *, tm=256):
    t, d = h.shape
    return pl.pallas_call(
        _ffn_norm_router_kernel,
        grid=(t // tm,),
        in_specs=[pl.BlockSpec((tm, d), lambda i: (i, 0)),
                  pl.BlockSpec((1, d), lambda i: (0, 0)),
                  pl.BlockSpec((d, LANES), lambda i: (0, 0)),
                  pl.BlockSpec((1, LANES), lambda i: (0, 0))],
        out_specs=[pl.BlockSpec((tm * ROW_PITCH, LANES), lambda i: (i, 0)),
                   pl.BlockSpec((tm, LANES), lambda i: (i, 0))],
        out_shape=[jax.ShapeDtypeStruct((t * ROW_PITCH, LANES), U32),
                   jax.ShapeDtypeStruct((t, LANES), F32)],
        compiler_params=_cparams(("parallel",)),
        name="ffn_norm_router",
    )(h, g.reshape(1, d), w_r, b_r)


MOE_UP_STEPS = D_EXPERT // MOE_CHUNK
MOE_DOWN_STEPS = 2
MOE_STEPS = MOE_UP_STEPS + MOE_DOWN_STEPS
MOE_DOWN_TILE = D_MODEL // MOE_DOWN_STEPS


LANES_LOG2 = LANES.bit_length() - 1
IDX_TILE_ROWS = 8
IDX_TILE = IDX_TILE_ROWS * LANES
IDX_TILE_LOG2 = IDX_TILE.bit_length() - 1
IDX_WINDOW_ROWS = 2 * IDX_TILE_ROWS
assert MOE_ROWS <= IDX_TILE
ROW_UNROLL_LOG2 = 3
ROW_UNROLL = 1 << ROW_UNROLL_LOG2


def _moe_kernel(item_e, item_cnt, item_row0, dstk_ref, hn_ref, wg_ref, wu_ref, wd_ref, pairs_ref,
                idx_win, x_stage, x_bf, hid, out_stage, sem_idx, sem_in, sem_out, *, n_tokens):
    del item_e
    w = pl.program_id(0)
    s = pl.program_id(1)
    n_items = pl.num_programs(0)
    cnt = item_cnt[w]
    nsub = (cnt + MOE_SUB - 1) // MOE_SUB
    nxt_item = jnp.minimum(w + 1, n_items - 1)
    has_next = (w + 1 < n_items) & (item_cnt[nxt_item] > 0)
    cur = w & 1
    nxt = 1 - cur

    def idx_copy(item, buf):
        first = (item_row0[item] >> IDX_TILE_LOG2) * IDX_TILE_ROWS
        return pltpu.make_async_copy(dstk_ref.at[pl.ds(pl.multiple_of(first, IDX_TILE_ROWS), IDX_WINDOW_ROWS)],
                                     idx_win.at[buf], sem_idx)

    def unrolled_rows(n, one_row):
        groups = n >> ROW_UNROLL_LOG2

        def group(g, carry):
            for u in range(ROW_UNROLL):
                one_row(g * ROW_UNROLL + u)
            return carry

        def single(r, carry):
            one_row(r)
            return carry

        lax.fori_loop(0, groups, group, 0)
        lax.fori_loop(groups * ROW_UNROLL, n, single, 0)

    def for_rows(item, buf, fn):
        base = item_row0[item] & (IDX_TILE - 1)

        def one_row(r):
            off = base + r
            fn(idx_win[buf, off >> LANES_LOG2, off & (LANES - 1)], r)

        unrolled_rows(item_cnt[item], one_row)

    def row_in(tok, r):
        return pltpu.make_async_copy(hn_ref.at[pl.ds(pl.multiple_of(tok * ROW_PITCH, 8), ROW_PITCH)],
                                     x_stage.at[pl.ds(pl.multiple_of(r * ROW_PITCH, 8), ROW_PITCH)], sem_in)

    def row_out(slot, r):
        return pltpu.make_async_copy(out_stage.at[pl.ds(pl.multiple_of(r * ROW_PITCH, 8), ROW_PITCH)],
                                     pairs_ref.at[pl.ds(pl.multiple_of(slot * ROW_PITCH, 8), ROW_PITCH)], sem_out)

    def gather_start(item, buf):
        for_rows(item, buf, lambda v, r: row_in(v & (n_tokens - 1), r).start())

    def scatter_start(item, buf):
        for_rows(item, buf, lambda v, r: row_out(v, r).start())

    def wait_rows(n, copy):
        unrolled_rows(n, lambda r: copy(0, r).wait())

    @pl.when((w == 0) & (s == 0))
    def _():
        x_stage[...] = jnp.zeros(x_stage.shape, x_stage.dtype)
        out_stage[...] = jnp.zeros(out_stage.shape, out_stage.dtype)

        @pl.when(cnt > 0)
        def _():
            first_window = idx_copy(0, 0)
            first_window.start()
            first_window.wait()
            gather_start(0, 0)

    @pl.when((s == 0) & (cnt > 0))
    def _():
        wait_rows(cnt, row_in)

    @pl.when((s == 0) & has_next)
    def _():
        idx_copy(nxt_item, nxt).start()

    @pl.when((s == MOE_UP_STEPS) & (w > 0) & (cnt > 0))
    def _():
        wait_rows(item_cnt[jnp.maximum(w - 1, 0)], row_out)

    def compute(m):
        @pl.when(s == 0)
        def _():
            for c in range(ROW_CHUNKS):
                lo, hi = _chunk_cols(c)
                x_lo, x_hi = _unpack_pair(x_stage[pl.ds(c, m, stride=ROW_PITCH), :])
                x_bf[:m, lo] = x_lo.astype(BF16)
                x_bf[:m, hi] = x_hi.astype(BF16)

        @pl.when(s < MOE_UP_STEPS)
        def _():
            x = x_bf[:m, :]
            g = jnp.dot(x, wg_ref[...].astype(BF16), preferred_element_type=F32)
            u = jnp.dot(x, wu_ref[...].astype(BF16), preferred_element_type=F32)
            hid[s, :m, :] = ((g * jax.nn.sigmoid(g)) * u).astype(BF16)

        @pl.when(s >= MOE_UP_STEPS)
        def _():
            hidden = jnp.concatenate([hid[c, :m, :] for c in range(MOE_UP_STEPS)], axis=1)
            y = jnp.dot(hidden, wd_ref[...].astype(BF16), preferred_element_type=F32)
            tile_chunks = MOE_DOWN_TILE // (2 * LANES)
            base = (s - MOE_UP_STEPS) * tile_chunks
            for c in range(tile_chunks):
                lo, hi = _chunk_cols(c)
                out_stage[pl.ds(base + c, m, stride=ROW_PITCH), :] = _pack_pair(y[:, lo], y[:, hi])

    for k in range(1, MOE_ROWS // MOE_SUB + 1):
        pl.when(nsub == k)(functools.partial(compute, k * MOE_SUB))

    @pl.when((s == 1) & has_next)
    def _():
        idx_copy(nxt_item, nxt).wait()
        gather_start(nxt_item, nxt)

    @pl.when((s == MOE_STEPS - 1) & (cnt > 0))
    def _():
        scatter_start(w, cur)

        @pl.when(jnp.logical_not(has_next))
        def _():
            wait_rows(cnt, row_out)


def _moe_experts(hn_lin, w_gate, w_up, w_down, item_e, item_cnt, item_row0, dstk, n_tokens):
    n_items = item_e.shape[0]
    d = D_MODEL
    assert n_tokens & (n_tokens - 1) == 0, "token ids are unpacked from k * n_tokens + token with a mask"

    def up_chunk(s, cnt_ref, w):
        return jnp.where(cnt_ref[w] > 0, jnp.minimum(s, MOE_UP_STEPS - 1), MOE_UP_STEPS - 1)

    def down_block(w, s, e, cn):
        own = (cn[w] > 0) & (s >= MOE_UP_STEPS)
        return (jnp.where(own, e[w], e[jnp.maximum(w - 1, 0)]), 0,
                jnp.where(own, s - MOE_UP_STEPS, MOE_DOWN_STEPS - 1))

    return pl.pallas_call(
        functools.partial(_moe_kernel, n_tokens=n_tokens),
        grid_spec=pltpu.PrefetchScalarGridSpec(
            num_scalar_prefetch=3,
            grid=(n_items, MOE_STEPS),
            in_specs=[pl.BlockSpec(memory_space=pl.ANY),
                      pl.BlockSpec(memory_space=pl.ANY),
                      pl.BlockSpec((None, d, MOE_CHUNK), lambda w, s, e, cn, r0: (e[w], 0, up_chunk(s, cn, w))),
                      pl.BlockSpec((None, d, MOE_CHUNK), lambda w, s, e, cn, r0: (e[w], 0, up_chunk(s, cn, w))),
                      pl.BlockSpec((None, D_EXPERT, MOE_DOWN_TILE), lambda w, s, e, cn, r0: down_block(w, s, e, cn))],
            out_specs=pl.BlockSpec(memory_space=pl.ANY),
            scratch_shapes=[pltpu.SMEM((2, IDX_WINDOW_ROWS, LANES), jnp.int32),
                            pltpu.VMEM((MOE_ROWS * ROW_PITCH, LANES), U32),
                            pltpu.VMEM((MOE_ROWS, d), BF16),
                            pltpu.VMEM((MOE_UP_STEPS, MOE_ROWS, MOE_CHUNK), BF16),
                            pltpu.VMEM((MOE_ROWS * ROW_PITCH, LANES), U32),
                            pltpu.SemaphoreType.DMA(()),
                            pltpu.SemaphoreType.DMA(()),
                            pltpu.SemaphoreType.DMA(())]),
        out_shape=jax.ShapeDtypeStruct((TOP_K * n_tokens * ROW_PITCH, LANES), U32),
        compiler_params=_cparams(("arbitrary", "arbitrary")),
        name="moe_experts",
    )(item_e, item_cnt, item_row0, dstk, hn_lin, w_gate, w_up, w_down)


def _combine_kernel(h_ref, p0_ref, p1_ref, gate_ref, g_ref, o_ref):
    tm, d = h_ref.shape
    g0 = gate_ref[:, 0:1]
    g1 = gate_ref[:, 1:2]
    ss = jnp.zeros((tm, 1), F32)
    for c in range(ROW_CHUNKS):
        a = _unpack_pair(p0_ref[pl.ds(c, tm, stride=ROW_PITCH), :])
        b = _unpack_pair(p1_ref[pl.ds(c, tm, stride=ROW_PITCH), :])
        for sl, a_half, b_half in zip(_chunk_cols(c), a, b):
            y = h_ref[:, sl] + (a_half * g0 + b_half * g1)
            o_ref[:, sl] = y
            ss = ss + jnp.sum(y * y, axis=-1, keepdims=True)
    inv = lax.rsqrt(ss / d + EPS)
    o_ref[...] = o_ref[...] * inv * g_ref[...]


def _combine(h, pairs, gates, g_final, *, tm=256):
    t, d = h.shape
    k1 = t // tm
    return pl.pallas_call(
        _combine_kernel,
        grid=(t // tm,),
        in_specs=[pl.BlockSpec((tm, d), lambda i: (i, 0)),
                  pl.BlockSpec((tm * ROW_PITCH, LANES), lambda i: (i, 0)),
                  pl.BlockSpec((tm * ROW_PITCH, LANES), lambda i: (k1 + i, 0)),
                  pl.BlockSpec((tm, TOP_K), lambda i: (i, 0)),
                  pl.BlockSpec((1, d), lambda i: (0, 0))],
        out_specs=pl.BlockSpec((tm, d), lambda i: (i, 0)),
        out_shape=jax.ShapeDtypeStruct((t, d), F32),
        compiler_params=_cparams(("parallel",)),
        name="combine_final_norm",
    )(h, pairs, pairs, gates, g_final.reshape(1, d))


def _route(logits, t):
    group_prob = jax.nn.softmax(logits[:, :N_GROUPS], axis=-1)
    g_sel = jnp.argmax(group_prob, axis=-1)
    g_p = jnp.max(group_prob, axis=-1, keepdims=True)
    e_logits = logits[:, N_GROUPS:N_GROUPS + N_EXPERTS]
    in_group = (jnp.arange(N_EXPERTS) // EXPERTS_PER_GROUP)[None, :] == g_sel[:, None]
    e_prob = jax.nn.softmax(jnp.where(in_group, e_logits, -jnp.inf), axis=-1)
    key = jnp.where(in_group, e_prob, -1.0)
    i1 = jnp.argmax(key, axis=-1)
    key2 = jnp.where(jnp.arange(N_EXPERTS)[None, :] == i1[:, None], -2.0, key)
    i2 = jnp.argmax(key2, axis=-1)
    top_p = jnp.stack([jnp.max(key, axis=-1), jnp.max(key2, axis=-1)], axis=-1)
    expert_idx = jnp.stack([i1, i2], axis=-1)
    assert TOP_K == 2
    gates = g_p * top_p / jnp.sum(top_p, axis=-1, keepdims=True)

    tk = t * TOP_K
    n_items_max = tk // MOE_ROWS + N_EXPERTS
    flat_e = expert_idx.reshape(tk).astype(jnp.int32)
    order = jnp.argsort(flat_e).astype(jnp.int32)
    dstk = (order % TOP_K) * t + order // TOP_K
    dstk = jnp.concatenate([dstk, jnp.zeros((2 * IDX_TILE,), jnp.int32)]).reshape(-1, LANES)
    counts = jnp.sum((flat_e[:, None] == jnp.arange(N_EXPERTS, dtype=jnp.int32)[None, :]).astype(jnp.int32), axis=0)
    start = jnp.cumsum(counts) - counts
    items_of = (counts + MOE_ROWS - 1) // MOE_ROWS
    item_end = jnp.cumsum(items_of)
    wi = jnp.arange(n_items_max, dtype=jnp.int32)
    e_w = jnp.minimum(jnp.sum((item_end[None, :] <= wi[:, None]).astype(jnp.int32), axis=1), N_EXPERTS - 1)
    part = wi - (item_end[e_w] - items_of[e_w])
    active = wi < item_end[-1]
    item_cnt = jnp.where(active, jnp.clip(counts[e_w] - part * MOE_ROWS, 0, MOE_ROWS), 0).astype(jnp.int32)
    item_row0 = jnp.where(active, start[e_w] + part * MOE_ROWS, 0).astype(jnp.int32)
    item_e = jnp.where(active, e_w, e_w[jnp.maximum(item_end[-1] - 1, 0)]).astype(jnp.int32)
    return gates.astype(F32), item_e, item_cnt, item_row0, dstk


def _take_pieces(w, pieces, axis):
    parts = []
    for start, width in pieces:
        if start is None:
            shape = (w.shape[0], width) if axis == 1 else (width, w.shape[1])
            parts.append(jnp.zeros(shape, w.dtype))
        else:
            parts.append(w[:, start:start + width] if axis == 1 else w[start:start + width, :])
    return jnp.concatenate(parts, axis=axis)


def _column_layout_kernel(w_ref, *o_refs, groups):
    w = w_ref[...]
    for o_ref, pieces in zip(o_refs, groups):
        o_ref[...] = _take_pieces(w, pieces, 1).astype(o_ref.dtype)


def _column_layout_from_transpose_kernel(wt_ref, *o_refs, groups):
    wt = wt_ref[...]
    for o_ref, pieces in zip(o_refs, groups):
        o_ref[...] = _take_pieces(wt, pieces, 0).T.astype(o_ref.dtype)


def _column_layout(w, layer, groups, *, tk=256, name, read_transposed=False):
    _, d, n = w.shape
    groups = tuple(tuple(g) for g in groups)
    widths = [sum(width for _, width in g) for g in groups]
    if read_transposed:
        body = functools.partial(_column_layout_from_transpose_kernel, groups=groups)
        w = jnp.swapaxes(w, 1, 2)
        in_spec = pl.BlockSpec((None, n, tk), lambda i: (layer, 0, i))
    else:
        body = functools.partial(_column_layout_kernel, groups=groups)
        in_spec = pl.BlockSpec((None, tk, n), lambda i: (layer, i, 0))
    return pl.pallas_call(
        body,
        grid=(d // tk,),
        in_specs=[in_spec],
        out_specs=[pl.BlockSpec((tk, width), lambda i: (i, 0)) for width in widths],
        out_shape=[jax.ShapeDtypeStruct((d, width), BF16) for width in widths],
        compiler_params=_cparams(("parallel",)),
        name=name,
    )(w)


def _prep_in_weight(w_in, layer):
    off_kv = MLA_Q_RANK + MLA_KV_RANK
    off_kpe = off_kv + MLA_ROPE_DIM
    off_sq = off_kpe + SWA_HEADS * SWA_HEAD_DIM
    off_sk = off_sq + SWA_KV_HEADS * SWA_HEAD_DIM
    kv_group = []
    for h in range(SWA_KV_HEADS):
        kv_group += [(off_sq + h * SWA_HEAD_DIM, SWA_HEAD_DIM), (off_sk + h * SWA_HEAD_DIM, SWA_HEAD_DIM)]
    kv_group += [(off_kv, MLA_ROPE_DIM), (None, LANES - MLA_ROPE_DIM)]
    groups = [[(0, off_kv)], [(off_kpe, off_sq - off_kpe)], kv_group]
    return _column_layout(w_in, layer, groups, name="in_weight_layout", read_transposed=True)


def _prep_q_weight(w_q_up, layer):
    pieces = []
    for h in range(MLA_HEADS):
        pieces += [(h * MLA_QK_DIM, MLA_QK_DIM), (None, MLA_HEAD_PAD - MLA_QK_DIM)]
    return _column_layout(w_q_up, layer, [pieces], name="q_weight_layout")[0]


def kernel(x, positions, g_attn_norm, w_in, g_q_norm, w_q_up, g_kv_norm, w_kv_up, sinks, w_out,
           g_ffn_norm, w_router_group, b_router_group, w_router_expert, b_router_expert,
           w_expert_gate, w_expert_up, w_expert_down, g_final_norm):
    batch, seq, d = x.shape
    t = batch * seq
    assert g_attn_norm.shape[0] == 1, "the final rmsnorm is fused into the (single) layer's combine"
    h = x.reshape(t, d)
    cos, sin = _rope_tables(positions)
    for l in range(g_attn_norm.shape[0]):
        n = _rmsnorm(h, g_attn_norm[l])
        w_lat, w_sq, w_skv = _prep_in_weight(w_in, l)

        def in_proj(w, kind, scale, name):
            blocks = w.shape[1] // LANES
            return _matmul_rope(n, w, cos, sin, tm=512, kinds=[kind] * blocks, scales=[scale] * blocks, name=name)

        z_lat = in_proj(w_lat, ROPE_NONE, 1.0, "in_proj_latents")
        z_sq = in_proj(w_sq, ROPE_FULL, SWA_HEAD_DIM ** -0.5, "in_proj_swa_q")
        z_skv = in_proj(w_skv, ROPE_HALF, 1.0, "in_proj_swa_kv")
        q_kinds = [ROPE_NONE, ROPE_HALF] * MLA_HEADS
        q = _norm_matmul(z_lat, 0, MLA_Q_RANK, g_q_norm[l], _prep_q_weight(w_q_up, l), cos, sin,
                         tm=1024, tn=2048, kinds=q_kinds, scales=[MLA_QK_DIM ** -0.5] * len(q_kinds), name="q_up")
        kv_kinds = [ROPE_NONE] * (w_kv_up.shape[2] // LANES)
        kv = _norm_matmul(z_lat, MLA_Q_RANK // MLA_KV_RANK, MLA_KV_RANK, g_kv_norm[l], w_kv_up[l], cos, sin,
                          tm=1024, tn=2048, kinds=kv_kinds, scales=[1.0] * len(kv_kinds), name="kv_up")
        mla_o = _mla_attention(q, kv, z_skv, batch, seq)
        swa_o = _swa_attention(z_sq, z_skv, sinks[l], batch, seq)
        w_r = jnp.concatenate([w_router_group[l], w_router_expert[l],
                               jnp.zeros((d, LANES - N_GROUPS - N_EXPERTS), F32)], axis=1)
        b_r = jnp.concatenate([b_router_group[l], b_router_expert[l],
                               jnp.zeros((LANES - N_GROUPS - N_EXPERTS,), F32)]).reshape(1, LANES)
        h = _out_proj(h, mla_o, swa_o, w_out[l])
        hn, logits = _ffn_norm_router(h, g_ffn_norm[l], w_r, b_r)
        gates, item_e, item_cnt, item_row0, dstk = _route(logits, t)
        pairs = _moe_experts(hn, w_expert_gate[l], w_expert_up[l], w_expert_down[l],
                             item_e, item_cnt, item_row0, dstk, t)
        out = _combine(h, pairs, gates, g_final_norm)
    return out.reshape(batch, seq, d)
```

```python
import functools

import jax
import jax.numpy as jnp
from jax import lax
from jax.experimental import pallas as pl
from jax.experimental.pallas import tpu as pltpu

F32 = jnp.float32
BF16 = jnp.bfloat16

D_MODEL = 4096
EPS = 1e-6
ROPE_THETA = 10000.0
NEG = -1e30
MLA_HEADS = 16
MLA_Q_RANK = 1024
MLA_KV_RANK = 512
MLA_NOPE_DIM = 128
MLA_ROPE_DIM = 64
MLA_V_DIM = 128
MLA_QK_DIM = MLA_NOPE_DIM + MLA_ROPE_DIM
MLA_HEAD_PAD = 256
SWA_HEADS = 32
SWA_KV_HEADS = 4
SWA_GROUP = SWA_HEADS // SWA_KV_HEADS
SWA_HEAD_DIM = 64
SWA_WINDOW = 128
N_GROUPS = 8
EXPERTS_PER_GROUP = 8
N_EXPERTS = N_GROUPS * EXPERTS_PER_GROUP
TOP_K = 2
D_EXPERT = 1024

LANES = 128
KV_KPE_BLOCK = SWA_KV_HEADS
ROPE_NONE, ROPE_FULL, ROPE_HALF = 0, 1, 2

MOE_ROWS = 512
MOE_ROW_VARIANTS = (128, 256, 320, 384, MOE_ROWS)
MOE_CHUNK = 256
VMEM_LIMIT = 56 * 1024 * 1024


def _cparams(sem, vmem=VMEM_LIMIT):
    return pltpu.CompilerParams(dimension_semantics=sem, vmem_limit_bytes=vmem)


def _swap32(x):
    lane = lax.broadcasted_iota(jnp.int32, x.shape, 1)
    fwd = pltpu.roll(x, LANES - 32, 1)
    bwd = pltpu.roll(x, 32, 1)
    return jnp.where((lane % 64) < 32, fwd, bwd)


def _rope_tile(x, cos, sin, kind):
    if kind == ROPE_NONE:
        return x
    if kind == ROPE_HALF:
        lane = lax.broadcasted_iota(jnp.int32, x.shape, 1)
        cos = jnp.where(lane < 64, cos, 1.0)
        sin = jnp.where(lane < 64, sin, 0.0)
    return x * cos + _swap32(x) * sin


def _rope_table_kernel(pos_ref, invf_ref, sign_ref, cos_ref, sin_ref):
    ang = pos_ref[...].astype(F32) * invf_ref[...]
    cos_ref[...] = jnp.cos(ang)
    sin_ref[...] = jnp.sin(ang) * sign_ref[...]


def _rope_tables(positions):
    t = positions.size
    pos = positions.reshape(t, 1)
    half = SWA_HEAD_DIM // 2
    inv_freq = ROPE_THETA ** (-jnp.arange(0, SWA_HEAD_DIM, 2, dtype=F32) / SWA_HEAD_DIM)
    lane = jnp.arange(LANES)
    invf = inv_freq[lane % half].reshape(1, LANES)
    sign = jnp.where((lane % SWA_HEAD_DIM) < half, -1.0, 1.0).astype(F32).reshape(1, LANES)
    tm = 1024
    return pl.pallas_call(
        _rope_table_kernel,
        grid=(t // tm,),
        in_specs=[pl.BlockSpec((tm, 1), lambda i: (i, 0)),
                  pl.BlockSpec((1, LANES), lambda i: (0, 0)),
                  pl.BlockSpec((1, LANES), lambda i: (0, 0))],
        out_specs=[pl.BlockSpec((tm, LANES), lambda i: (i, 0))] * 2,
        out_shape=[jax.ShapeDtypeStruct((t, LANES), F32)] * 2,
        compiler_params=_cparams(("parallel",)),
        name="rope_tables",
    )(pos, invf, sign)


def _rmsnorm_matmul_kernel(x_ref, g_ref, w_ref, o_ref, n_ref):
    xf = x_ref[...].astype(F32)
    ms = jnp.mean(xf * xf, axis=-1, keepdims=True)
    n = (xf * lax.rsqrt(ms + EPS) * g_ref[...]).astype(n_ref.dtype)
    n_ref[...] = n
    o_ref[...] = jnp.dot(n, w_ref[...], preferred_element_type=F32).astype(o_ref.dtype)


def _rmsnorm_matmul(x, g, w, *, tm, name):
    t, d = x.shape
    n = w.shape[1]
    return pl.pallas_call(
        _rmsnorm_matmul_kernel,
        grid=(t // tm,),
        in_specs=[pl.BlockSpec((tm, d), lambda i: (i, 0)),
                  pl.BlockSpec((1, d), lambda i: (0, 0)),
                  pl.BlockSpec((d, n), lambda i: (0, 0), pipeline_mode=pl.Buffered(1))],
        out_specs=[pl.BlockSpec((tm, n), lambda i: (i, 0)), pl.BlockSpec((tm, d), lambda i: (i, 0))],
        out_shape=[jax.ShapeDtypeStruct((t, n), BF16), jax.ShapeDtypeStruct((t, d), BF16)],
        compiler_params=_cparams(("parallel",)),
        name=name,
    )(x, g.reshape(1, d), w)


def _rope_epilogue(acc, cos_ref, sin_ref, o_ref, j, kinds, scales):
    def epilogue(tile_kinds, tile_scales):
        needs_rope = any(k != ROPE_NONE for k in tile_kinds)
        cos = cos_ref[...] if needs_rope else None
        sin = sin_ref[...] if needs_rope else None
        for c, (kind, scale) in enumerate(zip(tile_kinds, tile_scales)):
            blk = _rope_tile(acc[:, c * LANES:(c + 1) * LANES], cos, sin, kind)
            if scale != 1.0:
                blk = blk * scale
            o_ref[:, c * LANES:(c + 1) * LANES] = blk.astype(o_ref.dtype)

    if all(k == kinds[0] for k in kinds) and all(s == scales[0] for s in scales):
        epilogue(kinds[0], scales[0])
    else:
        for jj in range(len(kinds)):
            pl.when(j == jj)(functools.partial(epilogue, kinds[jj], scales[jj]))


def _per_tile(values, n_j, per):
    return tuple(tuple(values[jj * per:(jj + 1) * per]) for jj in range(n_j))


def _matmul_rope_kernel(x_ref, w_ref, cos_ref, sin_ref, o_ref, *, kinds, scales):
    acc = jnp.dot(x_ref[...], w_ref[...], preferred_element_type=F32)
    _rope_epilogue(acc, cos_ref, sin_ref, o_ref, 0, (kinds,), (scales,))


def _matmul_rope(x, w, cos, sin, *, tm, kinds, scales, name):
    t, k = x.shape
    n = w.shape[1]
    return pl.pallas_call(
        functools.partial(_matmul_rope_kernel, kinds=tuple(kinds), scales=tuple(scales)),
        grid=(t // tm,),
        in_specs=[pl.BlockSpec((tm, k), lambda i: (i, 0)),
                  pl.BlockSpec((k, n), lambda i: (0, 0), pipeline_mode=pl.Buffered(1)),
                  pl.BlockSpec((tm, LANES), lambda i: (i, 0)),
                  pl.BlockSpec((tm, LANES), lambda i: (i, 0))],
        out_specs=pl.BlockSpec((tm, n), lambda i: (i, 0)),
        out_shape=jax.ShapeDtypeStruct((t, n), BF16),
        compiler_params=_cparams(("parallel",)),
        name=name,
    )(x, w, cos, sin)


def _norm_matmul_kernel(x_ref, g_ref, w_ref, cos_ref, sin_ref, o_ref, n_scr, *, kinds, scales):
    j = pl.program_id(1)

    @pl.when(j == 0)
    def _():
        xf = x_ref[...].astype(F32)
        ms = jnp.mean(xf * xf, axis=-1, keepdims=True)
        n_scr[...] = (xf * lax.rsqrt(ms + EPS) * g_ref[...]).astype(BF16)

    acc = jnp.dot(n_scr[...], w_ref[...].astype(BF16), preferred_element_type=F32)
    _rope_epilogue(acc, cos_ref, sin_ref, o_ref, j, kinds, scales)


def _norm_matmul(x, x_col_block, k, g, w, cos, sin, *, tm, tn, kinds, scales, name):
    t = x.shape[0]
    n = w.shape[1]
    n_j = n // tn
    kinds = _per_tile(kinds, n_j, tn // LANES)
    scales = _per_tile(scales, n_j, tn // LANES)
    return pl.pallas_call(
        functools.partial(_norm_matmul_kernel, kinds=kinds, scales=scales),
        grid=(t // tm, n_j),
        in_specs=[pl.BlockSpec((tm, k), lambda i, j: (i, x_col_block)),
                  pl.BlockSpec((1, k), lambda i, j: (0, 0)),
                  pl.BlockSpec((k, tn), lambda i, j: (0, j)),
                  pl.BlockSpec((tm, LANES), lambda i, j: (i, 0)),
                  pl.BlockSpec((tm, LANES), lambda i, j: (i, 0))],
        out_specs=pl.BlockSpec((tm, tn), lambda i, j: (i, j)),
        out_shape=jax.ShapeDtypeStruct((t, n), BF16),
        scratch_shapes=[pltpu.VMEM((tm, k), BF16)],
        compiler_params=_cparams(("parallel", "arbitrary")),
        name=name,
    )(x, g.reshape(1, k), w, cos, sin)


def _mla_kernel(q_ref, kv_ref, kpe_ref, o_ref, kcat_scr, *, tq):
    s_len = q_ref.shape[0]
    kcat_scr[:, :MLA_NOPE_DIM] = kv_ref[:, :MLA_NOPE_DIM]
    kcat_scr[:, MLA_NOPE_DIM:] = kpe_ref[...]
    row = lax.broadcasted_iota(jnp.int32, (tq, tq), 0)
    col = lax.broadcasted_iota(jnp.int32, (tq, tq), 1)
    tri = col <= row
    nt = (((1,), (1,)), ((), ()))
    for qi in range(s_len // tq):
        lo, hi = qi * tq, (qi + 1) * tq
        q = q_ref[lo:hi, :]
        s_diag = lax.dot_general(q, kcat_scr[lo:hi, :], nt, preferred_element_type=F32)
        s_diag = jnp.where(tri, s_diag, NEG)
        m = jnp.max(s_diag, axis=-1, keepdims=True)
        if qi > 0:
            s_past = lax.dot_general(q, kcat_scr[:lo, :], nt, preferred_element_type=F32)
            m = jnp.maximum(m, jnp.max(s_past, axis=-1, keepdims=True))
            p_past = jnp.exp(s_past - m)
        p_diag = jnp.exp(s_diag - m)
        l = jnp.sum(p_diag, axis=-1, keepdims=True)
        o = jnp.dot(p_diag.astype(BF16), kv_ref[lo:hi, MLA_NOPE_DIM:], preferred_element_type=F32)
        if qi > 0:
            l = l + jnp.sum(p_past, axis=-1, keepdims=True)
            o = o + jnp.dot(p_past.astype(BF16), kv_ref[:lo, MLA_NOPE_DIM:], preferred_element_type=F32)
        o_ref[lo:hi, :] = (o / l).astype(o_ref.dtype)


def _mla_attention(q, kv, z_kv, batch, seq):
    t = batch * seq
    kpe_block = KV_KPE_BLOCK
    return pl.pallas_call(
        functools.partial(_mla_kernel, tq=256),
        grid=(batch, MLA_HEADS),
        in_specs=[pl.BlockSpec((seq, MLA_HEAD_PAD), lambda b, h: (b, h)),
                  pl.BlockSpec((seq, MLA_NOPE_DIM + MLA_V_DIM), lambda b, h: (b, h)),
                  pl.BlockSpec((seq, LANES), lambda b, h: (b, kpe_block))],
        out_specs=pl.BlockSpec((seq, MLA_V_DIM), lambda b, h: (b, h)),
        out_shape=jax.ShapeDtypeStruct((t, MLA_HEADS * MLA_V_DIM), BF16),
        scratch_shapes=[pltpu.VMEM((seq, MLA_HEAD_PAD), BF16)],
        compiler_params=_cparams(("parallel", "arbitrary")),
        name="mla_attention",
    )(q, kv, z_kv)


def _swa_kernel(sinks_ref, q_ref, kv_ref, o_ref, klo, khi, vlo, vhi):
    h = pl.program_id(1)
    s_len = q_ref.shape[0]
    w = SWA_WINDOW
    pairs = SWA_GROUP // 2
    kv = kv_ref[...].astype(F32)
    lane = lax.broadcasted_iota(jnp.int32, kv.shape, 1)
    low = lane < SWA_HEAD_DIM
    rolled = pltpu.roll(kv, SWA_HEAD_DIM, 1)
    pad = jnp.zeros((w, LANES), BF16)
    for scr, val in ((klo, jnp.where(low, kv, 0.0)), (khi, jnp.where(low, 0.0, rolled)),
                     (vlo, jnp.where(low, rolled, 0.0)), (vhi, jnp.where(low, 0.0, kv))):
        scr[:w, :] = pad
        scr[w:, :] = val.astype(BF16)

    rows = pairs * w
    r = lax.broadcasted_iota(jnp.int32, (rows, 2 * w), 0) % w
    c = lax.broadcasted_iota(jnp.int32, (rows, 2 * w), 1)
    band = (c > r) & (c <= r + w)
    pair_of_row = lax.broadcasted_iota(jnp.int32, (rows, 1), 0) // w
    out_low = lax.broadcasted_iota(jnp.int32, (rows, LANES), 1) < SWA_HEAD_DIM

    def sink_column(parity):
        col = jnp.zeros((rows, 1), F32)
        for j in range(pairs):
            col = jnp.where(pair_of_row == j, sinks_ref[h * SWA_GROUP + 2 * j + parity], col)
        return col

    sink_even, sink_odd = sink_column(0), sink_column(1)
    nt = (((1,), (1,)), ((), ()))

    def block(n, mask):
        r0 = n * w if isinstance(n, int) else pl.multiple_of(n * w, w)
        qs = jnp.concatenate([q_ref[pl.ds(r0, w), j * LANES:(j + 1) * LANES] for j in range(pairs)], axis=0)

        def head(k_scr, v_scr, sink):
            s = lax.dot_general(qs, k_scr[pl.ds(r0, 2 * w), :], nt, preferred_element_type=F32)
            s = jnp.where(mask, s, NEG)
            m = jnp.maximum(jnp.max(s, axis=-1, keepdims=True), sink)
            p = jnp.exp(s - m)
            l = jnp.sum(p, axis=-1, keepdims=True) + jnp.exp(sink - m)
            o = jnp.dot(p.astype(BF16), v_scr[pl.ds(r0, 2 * w), :], preferred_element_type=F32)
            return o, l

        o_even, l_even = head(klo, vlo, sink_even)
        o_odd, l_odd = head(khi, vhi, sink_odd)
        o = (o_even + o_odd) / jnp.where(out_low, l_even, l_odd)
        for j in range(pairs):
            o_ref[pl.ds(r0, w), j * LANES:(j + 1) * LANES] = o[j * w:(j + 1) * w].astype(o_ref.dtype)

    block(0, band & (c >= w))

    def later_block(n, carry):
        block(n, band)
        return carry

    lax.fori_loop(1, s_len // w, later_block, 0, unroll=5)


def _swa_attention(z_q, z_kv, sinks, batch, seq):
    t = batch * seq
    gw = SWA_GROUP * SWA_HEAD_DIM
    return pl.pallas_call(
        _swa_kernel,
        grid_spec=pltpu.PrefetchScalarGridSpec(
            num_scalar_prefetch=1,
            grid=(batch, SWA_KV_HEADS),
            in_specs=[pl.BlockSpec((seq, gw), lambda b, h, s: (b, h)),
                      pl.BlockSpec((seq, LANES), lambda b, h, s: (b, h))],
            out_specs=pl.BlockSpec((seq, gw), lambda b, h, s: (b, h)),
            scratch_shapes=[pltpu.VMEM((seq + SWA_WINDOW, LANES), BF16)] * 4),
        out_shape=jax.ShapeDtypeStruct((t, SWA_HEADS * SWA_HEAD_DIM), BF16),
        compiler_params=_cparams(("parallel", "arbitrary")),
        name="swa_attention",
    )(sinks.astype(F32), z_q, z_kv)


def _out_proj_kernel(x_ref, a_ref, s_ref, wt_ref, wb_ref, h_ref):
    h_ref[...] = (x_ref[...] + jnp.dot(a_ref[...], wt_ref[...].astype(BF16), preferred_element_type=F32)
                  + jnp.dot(s_ref[...], wb_ref[...].astype(BF16), preferred_element_type=F32))


def _out_proj(x2, mla_o, swa_o, w_out, *, tm=1024, tn=512):
    t, d = x2.shape
    half = mla_o.shape[1]
    return pl.pallas_call(
        _out_proj_kernel,
        grid=(t // tm, d // tn),
        in_specs=[pl.BlockSpec((tm, tn), lambda i, j: (i, j)),
                  pl.BlockSpec((tm, half), lambda i, j: (i, 0)),
                  pl.BlockSpec((tm, half), lambda i, j: (i, 0)),
                  pl.BlockSpec((half, tn), lambda i, j: (0, j)),
                  pl.BlockSpec((half, tn), lambda i, j: (1, j))],
        out_specs=pl.BlockSpec((tm, tn), lambda i, j: (i, j)),
        out_shape=jax.ShapeDtypeStruct((t, d), F32),
        compiler_params=_cparams(("parallel", "parallel")),
        name="out_proj",
    )(x2, mla_o, swa_o, w_out, w_out)


ROW_CHUNKS = D_MODEL // (2 * LANES)
ROW_PITCH = 24
U32 = jnp.uint32
HIGH_HALF = 0xFFFF0000


def _pack_pair(lo, hi):
    lo_bits = lax.bitcast_convert_type(lo.astype(BF16).astype(F32), U32)
    hi_bits = lax.bitcast_convert_type(hi.astype(BF16).astype(F32), U32)
    return (lo_bits >> 16) | (hi_bits & U32(HIGH_HALF))


def _unpack_pair(words):
    return (lax.bitcast_convert_type(words << 16, F32),
            lax.bitcast_convert_type(words & U32(HIGH_HALF), F32))


def _chunk_cols(c):
    return (slice(2 * c * LANES, (2 * c + 1) * LANES), slice((2 * c + 1) * LANES, (2 * c + 2) * LANES))


def _split_bf16(v):
    hi = v.astype(BF16)
    return hi, (v - hi.astype(F32)).astype(BF16)


def _ffn_norm_router_kernel(h_ref, g_ref, wr_ref, br_ref, o_ref, lg_ref):
    tm = h_ref.shape[0]
    h = h_ref[...]
    inv = lax.rsqrt(jnp.mean(h * h, axis=-1, keepdims=True) + EPS)
    hg_hi, hg_lo = _split_bf16(h * g_ref[...])
    wr_hi, wr_lo = _split_bf16(wr_ref[...])
    acc = (jnp.dot(hg_hi, wr_hi, preferred_element_type=F32)
           + (jnp.dot(hg_hi, wr_lo, preferred_element_type=F32)
              + jnp.dot(hg_lo, wr_hi, preferred_element_type=F32)))
    lg_ref[...] = acc * inv + br_ref[...]
    for c in range(ROW_CHUNKS):
        lo, hi = _chunk_cols(c)
        o_ref[pl.ds(c, tm, stride=ROW_PITCH), :] = _pack_pair(h_ref[:, lo] * inv * g_ref[:, lo],
                                                              h_ref[:, hi] * inv * g_ref[:, hi])
    for c in range(ROW_CHUNKS, ROW_PITCH):
        o_ref[pl.ds(c, tm, stride=ROW_PITCH), :] = jnp.zeros((tm, LANES), o_ref.dtype)


def _ffn_norm_router(h, g, w_r, b_r, *, tm=256):
    t, d = h.shape
    return pl.pallas_call(
        _ffn_norm_router_kernel,
        grid=(t // tm,),
        in_specs=[pl.BlockSpec((tm, d), lambda i: (i, 0)),
                  pl.BlockSpec((1, d), lambda i: (0, 0)),
                  pl.BlockSpec((d, LANES), lambda i: (0, 0)),
                  pl.BlockSpec((1, LANES), lambda i: (0, 0))],
        out_specs=[pl.BlockSpec((tm * ROW_PITCH, LANES), lambda i: (i, 0)),
                   pl.BlockSpec((tm, LANES), lambda i: (i, 0))],
        out_shape=[jax.ShapeDtypeStruct((t * ROW_PITCH, LANES), U32),
                   jax.ShapeDtypeStruct((t, LANES), F32)],
        compiler_params=_cparams(("parallel",)),
        name="ffn_norm_router",
    )(h, g.reshape(1, d), w_r, b_r)


MOE_UP_STEPS = D_EXPERT // MOE_CHUNK
MOE_DOWN_STEPS = 2
MOE_STEPS = MOE_UP_STEPS + MOE_DOWN_STEPS
MOE_DOWN_TILE = D_MODEL // MOE_DOWN_STEPS


LANES_LOG2 = LANES.bit_length() - 1
IDX_TILE_ROWS = 8
IDX_TILE = IDX_TILE_ROWS * LANES
IDX_TILE_LOG2 = IDX_TILE.bit_length() - 1
IDX_WINDOW_ROWS = 2 * IDX_TILE_ROWS
assert MOE_ROWS <= IDX_TILE
ROW_UNROLL_LOG2 = 3
ROW_UNROLL = 1 << ROW_UNROLL_LOG2


def _moe_kernel(item_e, item_cnt, item_row0, dstk_ref, hn_ref, wg_ref, wu_ref, wd_ref, pairs_ref,
                idx_win, x_stage, x_bf, hid, out_stage, sem_idx, sem_in, sem_out, *, n_tokens):
    del item_e
    w = pl.program_id(0)
    s = pl.program_id(1)
    n_items = pl.num_programs(0)
    cnt = item_cnt[w]
    nxt_item = jnp.minimum(w + 1, n_items - 1)
    has_next = (w + 1 < n_items) & (item_cnt[nxt_item] > 0)
    cur = w & 1
    nxt = 1 - cur

    def idx_copy(item, buf):
        first = (item_row0[item] >> IDX_TILE_LOG2) * IDX_TILE_ROWS
        return pltpu.make_async_copy(dstk_ref.at[pl.ds(pl.multiple_of(first, IDX_TILE_ROWS), IDX_WINDOW_ROWS)],
                                     idx_win.at[buf], sem_idx)

    def unrolled_rows(n, one_row):
        groups = n >> ROW_UNROLL_LOG2

        def group(g, carry):
            for u in range(ROW_UNROLL):
                one_row(g * ROW_UNROLL + u)
            return carry

        def single(r, carry):
            one_row(r)
            return carry

        lax.fori_loop(0, groups, group, 0)
        lax.fori_loop(groups * ROW_UNROLL, n, single, 0)

    def for_rows(item, buf, fn):
        base = item_row0[item] & (IDX_TILE - 1)

        def one_row(r):
            off = base + r
            fn(idx_win[buf, off >> LANES_LOG2, off & (LANES - 1)], r)

        unrolled_rows(item_cnt[item], one_row)

    def row_in(tok, r):
        return pltpu.make_async_copy(hn_ref.at[pl.ds(pl.multiple_of(tok * ROW_PITCH, 8), ROW_PITCH)],
                                     x_stage.at[pl.ds(pl.multiple_of(r * ROW_PITCH, 8), ROW_PITCH)], sem_in)

    def row_out(slot, r):
        return pltpu.make_async_copy(out_stage.at[pl.ds(pl.multiple_of(r * ROW_PITCH, 8), ROW_PITCH)],
                                     pairs_ref.at[pl.ds(pl.multiple_of(slot * ROW_PITCH, 8), ROW_PITCH)], sem_out)

    def gather_start(item, buf):
        for_rows(item, buf, lambda v, r: row_in(v & (n_tokens - 1), r).start())

    def scatter_start(item, buf):
        for_rows(item, buf, lambda v, r: row_out(v, r).start())

    def wait_rows(n, copy):
        unrolled_rows(n, lambda r: copy(0, r).wait())

    @pl.when((w == 0) & (s == 0))
    def _():
        x_stage[...] = jnp.zeros(x_stage.shape, x_stage.dtype)
        out_stage[...] = jnp.zeros(out_stage.shape, out_stage.dtype)

        @pl.when(cnt > 0)
        def _():
            first_window = idx_copy(0, 0)
            first_window.start()
            first_window.wait()
            gather_start(0, 0)

    @pl.when((s == 0) & (cnt > 0))
    def _():
        wait_rows(cnt, row_in)

    @pl.when((s == 0) & has_next)
    def _():
        idx_copy(nxt_item, nxt).start()

    @pl.when((s == MOE_UP_STEPS) & (w > 0) & (cnt > 0))
    def _():
        wait_rows(item_cnt[jnp.maximum(w - 1, 0)], row_out)

    def compute(m):
        @pl.when(s == 0)
        def _():
            for c in range(ROW_CHUNKS):
                lo, hi = _chunk_cols(c)
                x_lo, x_hi = _unpack_pair(x_stage[pl.ds(c, m, stride=ROW_PITCH), :])
                x_bf[:m, lo] = x_lo.astype(BF16)
                x_bf[:m, hi] = x_hi.astype(BF16)

        @pl.when(s < MOE_UP_STEPS)
        def _():
            x = x_bf[:m, :]
            g = jnp.dot(x, wg_ref[...].astype(BF16), preferred_element_type=F32)
            u = jnp.dot(x, wu_ref[...].astype(BF16), preferred_element_type=F32)
            hid[s, :m, :] = ((g * jax.nn.sigmoid(g)) * u).astype(BF16)

        @pl.when(s >= MOE_UP_STEPS)
        def _():
            hidden = jnp.concatenate([hid[c, :m, :] for c in range(MOE_UP_STEPS)], axis=1)
            y = jnp.dot(hidden, wd_ref[...].astype(BF16), preferred_element_type=F32)
            tile_chunks = MOE_DOWN_TILE // (2 * LANES)
            base = (s - MOE_UP_STEPS) * tile_chunks
            for c in range(tile_chunks):
                lo, hi = _chunk_cols(c)
                out_stage[pl.ds(base + c, m, stride=ROW_PITCH), :] = _pack_pair(y[:, lo], y[:, hi])

    for below, m in zip((0,) + MOE_ROW_VARIANTS[:-1], MOE_ROW_VARIANTS):
        pl.when((cnt > below) & (cnt <= m))(functools.partial(compute, m))

    @pl.when((s == 1) & has_next)
    def _():
        idx_copy(nxt_item, nxt).wait()
        gather_start(nxt_item, nxt)

    @pl.when((s == MOE_STEPS - 1) & (cnt > 0))
    def _():
        scatter_start(w, cur)

        @pl.when(jnp.logical_not(has_next))
        def _():
            wait_rows(cnt, row_out)


def _moe_experts(hn_lin, w_gate, w_up, w_down, item_e, item_cnt, item_row0, dstk, n_tokens):
    n_items = item_e.shape[0]
    d = D_MODEL
    assert n_tokens & (n_tokens - 1) == 0, "token ids are unpacked from k * n_tokens + token with a mask"

    def up_chunk(s, cnt_ref, w):
        return jnp.where(cnt_ref[w] > 0, jnp.minimum(s, MOE_UP_STEPS - 1), MOE_UP_STEPS - 1)

    def down_block(w, s, e, cn):
        own = (cn[w] > 0) & (s >= MOE_UP_STEPS)
        return (jnp.where(own, e[w], e[jnp.maximum(w - 1, 0)]), 0,
                jnp.where(own, s - MOE_UP_STEPS, MOE_DOWN_STEPS - 1))

    return pl.pallas_call(
        functools.partial(_moe_kernel, n_tokens=n_tokens),
        grid_spec=pltpu.PrefetchScalarGridSpec(
            num_scalar_prefetch=3,
            grid=(n_items, MOE_STEPS),
            in_specs=[pl.BlockSpec(memory_space=pl.ANY),
                      pl.BlockSpec(memory_space=pl.ANY),
                      pl.BlockSpec((None, d, MOE_CHUNK), lambda w, s, e, cn, r0: (e[w], 0, up_chunk(s, cn, w))),
                      pl.BlockSpec((None, d, MOE_CHUNK), lambda w, s, e, cn, r0: (e[w], 0, up_chunk(s, cn, w))),
                      pl.BlockSpec((None, D_EXPERT, MOE_DOWN_TILE), lambda w, s, e, cn, r0: down_block(w, s, e, cn))],
            out_specs=pl.BlockSpec(memory_space=pl.ANY),
            scratch_shapes=[pltpu.SMEM((2, IDX_WINDOW_ROWS, LANES), jnp.int32),
                            pltpu.VMEM((MOE_ROWS * ROW_PITCH, LANES), U32),
                            pltpu.VMEM((MOE_ROWS, d), BF16),
                            pltpu.VMEM((MOE_UP_STEPS, MOE_ROWS, MOE_CHUNK), BF16),
                            pltpu.VMEM((MOE_ROWS * ROW_PITCH, LANES), U32),
                            pltpu.SemaphoreType.DMA(()),
                            pltpu.SemaphoreType.DMA(()),
                            pltpu.SemaphoreType.DMA(())]),
        out_shape=jax.ShapeDtypeStruct((TOP_K * n_tokens * ROW_PITCH, LANES), U32),
        compiler_params=_cparams(("arbitrary", "arbitrary")),
        name="moe_experts",
    )(item_e, item_cnt, item_row0, dstk, hn_lin, w_gate, w_up, w_down)


def _combine_kernel(h_ref, p0_ref, p1_ref, gate_ref, g_ref, o_ref):
    tm, d = h_ref.shape
    g0 = gate_ref[:, 0:1]
    g1 = gate_ref[:, 1:2]
    ss = jnp.zeros((tm, 1), F32)
    for c in range(ROW_CHUNKS):
        a = _unpack_pair(p0_ref[pl.ds(c, tm, stride=ROW_PITCH), :])
        b = _unpack_pair(p1_ref[pl.ds(c, tm, stride=ROW_PITCH), :])
        for sl, a_half, b_half in zip(_chunk_cols(c), a, b):
            y = h_ref[:, sl] + (a_half * g0 + b_half * g1)
            o_ref[:, sl] = y
            ss = ss + jnp.sum(y * y, axis=-1, keepdims=True)
    inv = lax.rsqrt(ss / d + EPS)
    o_ref[...] = o_ref[...] * inv * g_ref[...]


def _combine(h, pairs, gates, g_final, *, tm=256):
    t, d = h.shape
    k1 = t // tm
    return pl.pallas_call(
        _combine_kernel,
        grid=(t // tm,),
        in_specs=[pl.BlockSpec((tm, d), lambda i: (i, 0)),
                  pl.BlockSpec((tm * ROW_PITCH, LANES), lambda i: (i, 0)),
                  pl.BlockSpec((tm * ROW_PITCH, LANES), lambda i: (k1 + i, 0)),
                  pl.BlockSpec((tm, TOP_K), lambda i: (i, 0)),
                  pl.BlockSpec((1, d), lambda i: (0, 0))],
        out_specs=pl.BlockSpec((tm, d), lambda i: (i, 0)),
        out_shape=jax.ShapeDtypeStruct((t, d), F32),
        compiler_params=_cparams(("parallel",)),
        name="combine_final_norm",
    )(h, pairs, pairs, gates, g_final.reshape(1, d))


def _route(logits, t):
    group_prob = jax.nn.softmax(logits[:, :N_GROUPS], axis=-1)
    g_sel = jnp.argmax(group_prob, axis=-1)
    g_p = jnp.max(group_prob, axis=-1, keepdims=True)
    e_logits = logits[:, N_GROUPS:N_GROUPS + N_EXPERTS]
    in_group = (jnp.arange(N_EXPERTS) // EXPERTS_PER_GROUP)[None, :] == g_sel[:, None]
    e_prob = jax.nn.softmax(jnp.where(in_group, e_logits, -jnp.inf), axis=-1)
    key = jnp.where(in_group, e_prob, -1.0)
    i1 = jnp.argmax(key, axis=-1)
    key2 = jnp.where(jnp.arange(N_EXPERTS)[None, :] == i1[:, None], -2.0, key)
    i2 = jnp.argmax(key2, axis=-1)
    top_p = jnp.stack([jnp.max(key, axis=-1), jnp.max(key2, axis=-1)], axis=-1)
    expert_idx = jnp.stack([i1, i2], axis=-1)
    assert TOP_K == 2
    gates = g_p * top_p / jnp.sum(top_p, axis=-1, keepdims=True)

    tk = t * TOP_K
    n_items_max = tk // MOE_ROWS + N_EXPERTS
    flat_e = expert_idx.reshape(tk).astype(jnp.int32)
    order = jnp.argsort(flat_e).astype(jnp.int32)
    dstk = (order % TOP_K) * t + order // TOP_K
    dstk = jnp.concatenate([dstk, jnp.zeros((2 * IDX_TILE,), jnp.int32)]).reshape(-1, LANES)
    counts = jnp.sum((flat_e[:, None] == jnp.arange(N_EXPERTS, dtype=jnp.int32)[None, :]).astype(jnp.int32), axis=0)
    start = jnp.cumsum(counts) - counts
    items_of = (counts + MOE_ROWS - 1) // MOE_ROWS
    item_end = jnp.cumsum(items_of)
    wi = jnp.arange(n_items_max, dtype=jnp.int32)
    e_w = jnp.minimum(jnp.sum((item_end[None, :] <= wi[:, None]).astype(jnp.int32), axis=1), N_EXPERTS - 1)
    part = wi - (item_end[e_w] - items_of[e_w])
    active = wi < item_end[-1]
    item_cnt = jnp.where(active, jnp.clip(counts[e_w] - part * MOE_ROWS, 0, MOE_ROWS), 0).astype(jnp.int32)
    item_row0 = jnp.where(active, start[e_w] + part * MOE_ROWS, 0).astype(jnp.int32)
    item_e = jnp.where(active, e_w, e_w[jnp.maximum(item_end[-1] - 1, 0)]).astype(jnp.int32)
    return gates.astype(F32), item_e, item_cnt, item_row0, dstk


def _take_pieces(w, pieces, axis):
    parts = []
    for start, width in pieces:
        if start is None:
            shape = (w.shape[0], width) if axis == 1 else (width, w.shape[1])
            parts.append(jnp.zeros(shape, w.dtype))
        else:
            parts.append(w[:, start:start + width] if axis == 1 else w[start:start + width, :])
    return jnp.concatenate(parts, axis=axis)


def _column_layout_kernel(w_ref, *o_refs, groups):
    w = w_ref[...]
    for o_ref, pieces in zip(o_refs, groups):
        o_ref[...] = _take_pieces(w, pieces, 1).astype(o_ref.dtype)


def _column_layout_from_transpose_kernel(wt_ref, *o_refs, groups):
    wt = wt_ref[...]
    for o_ref, pieces in zip(o_refs, groups):
        o_ref[...] = _take_pieces(wt, pieces, 0).T.astype(o_ref.dtype)


def _column_layout(w, layer, groups, *, tk=256, name, read_transposed=False):
    _, d, n = w.shape
    groups = tuple(tuple(g) for g in groups)
    widths = [sum(width for _, width in g) for g in groups]
    if read_transposed:
        body = functools.partial(_column_layout_from_transpose_kernel, groups=groups)
        w = jnp.swapaxes(w, 1, 2)
        in_spec = pl.BlockSpec((None, n, tk), lambda i: (layer, 0, i))
    else:
        body = functools.partial(_column_layout_kernel, groups=groups)
        in_spec = pl.BlockSpec((None, tk, n), lambda i: (layer, i, 0))
    return pl.pallas_call(
        body,
        grid=(d // tk,),
        in_specs=[in_spec],
        out_specs=[pl.BlockSpec((tk, width), lambda i: (i, 0)) for width in widths],
        out_shape=[jax.ShapeDtypeStruct((d, width), BF16) for width in widths],
        compiler_params=_cparams(("parallel",)),
        name=name,
    )(w)


def _prep_in_weight(w_in, layer):
    off_kv = MLA_Q_RANK + MLA_KV_RANK
    off_kpe = off_kv + MLA_ROPE_DIM
    off_sq = off_kpe + SWA_HEADS * SWA_HEAD_DIM
    off_sk = off_sq + SWA_KV_HEADS * SWA_HEAD_DIM
    kv_group = []
    for h in range(SWA_KV_HEADS):
        kv_group += [(off_sq + h * SWA_HEAD_DIM, SWA_HEAD_DIM), (off_sk + h * SWA_HEAD_DIM, SWA_HEAD_DIM)]
    kv_group += [(off_kv, MLA_ROPE_DIM), (None, LANES - MLA_ROPE_DIM)]
    groups = [[(0, off_kv)], [(off_kpe, off_sq - off_kpe)], kv_group]
    return _column_layout(w_in, layer, groups, name="in_weight_layout", read_transposed=True)


def _prep_q_weight(w_q_up, layer):
    pieces = []
    for h in range(MLA_HEADS):
        pieces += [(h * MLA_QK_DIM, MLA_QK_DIM), (None, MLA_HEAD_PAD - MLA_QK_DIM)]
    return _column_layout(w_q_up, layer, [pieces], name="q_weight_layout")[0]


def kernel(x, positions, g_attn_norm, w_in, g_q_norm, w_q_up, g_kv_norm, w_kv_up, sinks, w_out,
           g_ffn_norm, w_router_group, b_router_group, w_router_expert, b_router_expert,
           w_expert_gate, w_expert_up, w_expert_down, g_final_norm):
    batch, seq, d = x.shape
    t = batch * seq
    assert g_attn_norm.shape[0] == 1, "the final rmsnorm is fused into the (single) layer's combine"
    h = x.reshape(t, d)
    cos, sin = _rope_tables(positions)
    for l in range(g_attn_norm.shape[0]):
        w_lat, w_sq, w_skv = _prep_in_weight(w_in, l)
        z_lat, n = _rmsnorm_matmul(h, g_attn_norm[l], w_lat, tm=512, name="attn_norm_in_proj_latents")

        def in_proj(w, kind, scale, name):
            blocks = w.shape[1] // LANES
            return _matmul_rope(n, w, cos, sin, tm=512, kinds=[kind] * blocks, scales=[scale] * blocks, name=name)

        z_sq = in_proj(w_sq, ROPE_FULL, SWA_HEAD_DIM ** -0.5, "in_proj_swa_q")
        z_skv = in_proj(w_skv, ROPE_HALF, 1.0, "in_proj_swa_kv")
        q_kinds = [ROPE_NONE, ROPE_HALF] * MLA_HEADS
        q = _norm_matmul(z_lat, 0, MLA_Q_RANK, g_q_norm[l], _prep_q_weight(w_q_up, l), cos, sin,
                         tm=1024, tn=2048, kinds=q_kinds, scales=[MLA_QK_DIM ** -0.5] * len(q_kinds), name="q_up")
        kv_kinds = [ROPE_NONE] * (w_kv_up.shape[2] // LANES)
        kv = _norm_matmul(z_lat, MLA_Q_RANK // MLA_KV_RANK, MLA_KV_RANK, g_kv_norm[l], w_kv_up[l], cos, sin,
                          tm=1024, tn=2048, kinds=kv_kinds, scales=[1.0] * len(kv_kinds), name="kv_up")
        mla_o = _mla_attention(q, kv, z_skv, batch, seq)
        swa_o = _swa_attention(z_sq, z_skv, sinks[l], batch, seq)
        w_r = jnp.concatenate([w_router_group[l], w_router_expert[l],
                               jnp.zeros((d, LANES - N_GROUPS - N_EXPERTS), F32)], axis=1)
        b_r = jnp.concatenate([b_router_group[l], b_router_expert[l],
                               jnp.zeros((LANES - N_GROUPS - N_EXPERTS,), F32)]).reshape(1, LANES)
        h = _out_proj(h, mla_o, swa_o, w_out[l])
        hn, logits = _ffn_norm_router(h, g_ffn_norm[l], w_r, b_r)
        gates, item_e, item_cnt, item_row0, dstk = _route(logits, t)
        pairs = _moe_experts(hn, w_expert_gate[l], w_expert_up[l], w_expert_down[l],
                             item_e, item_cnt, item_row0, dstk, t)
        out = _combine(h, pairs, gates, g_final_norm)
    return out.reshape(batch, seq, d)
```

```python
import functools

import jax
import jax.numpy as jnp
from jax import lax
from jax.experimental import pallas as pl
from jax.experimental.pallas import tpu as pltpu

F32 = jnp.float32
BF16 = jnp.bfloat16

D_MODEL = 4096
EPS = 1e-6
ROPE_THETA = 10000.0
NEG = -1e30
MLA_HEADS = 16
MLA_Q_RANK = 1024
MLA_KV_RANK = 512
MLA_NOPE_DIM = 128
MLA_ROPE_DIM = 64
MLA_V_DIM = 128
MLA_QK_DIM = MLA_NOPE_DIM + MLA_ROPE_DIM
MLA_HEAD_PAD = 256
SWA_HEADS = 32
SWA_KV_HEADS = 4
SWA_GROUP = SWA_HEADS // SWA_KV_HEADS
SWA_HEAD_DIM = 64
SWA_WINDOW = 128
N_GROUPS = 8
EXPERTS_PER_GROUP = 8
N_EXPERTS = N_GROUPS * EXPERTS_PER_GROUP
TOP_K = 2
D_EXPERT = 1024

LANES = 128
KV_KPE_BLOCK = SWA_KV_HEADS
ROPE_NONE, ROPE_FULL, ROPE_HALF = 0, 1, 2

MOE_ROWS = 512
MOE_ROW_VARIANTS = (128, 256, 320, 384, MOE_ROWS)
MOE_CHUNK = 256
VMEM_LIMIT = 56 * 1024 * 1024


def _cparams(sem, vmem=VMEM_LIMIT):
    return pltpu.CompilerParams(dimension_semantics=sem, vmem_limit_bytes=vmem)


def _swap32(x):
    lane = lax.broadcasted_iota(jnp.int32, x.shape, 1)
    fwd = pltpu.roll(x, LANES - 32, 1)
    bwd = pltpu.roll(x, 32, 1)
    return jnp.where((lane % 64) < 32, fwd, bwd)


def _rope_tile(x, cos, sin, kind):
    if kind == ROPE_NONE:
        return x
    if kind == ROPE_HALF:
        lane = lax.broadcasted_iota(jnp.int32, x.shape, 1)
        cos = jnp.where(lane < 64, cos, 1.0)
        sin = jnp.where(lane < 64, sin, 0.0)
    return x * cos + _swap32(x) * sin


def _rope_table_kernel(pos_ref, invf_ref, sign_ref, cos_ref, sin_ref):
    ang = pos_ref[...].astype(F32) * invf_ref[...]
    cos_ref[...] = jnp.cos(ang)
    sin_ref[...] = jnp.sin(ang) * sign_ref[...]


def _rope_tables(positions):
    t = positions.size
    pos = positions.reshape(t, 1)
    half = SWA_HEAD_DIM // 2
    inv_freq = ROPE_THETA ** (-jnp.arange(0, SWA_HEAD_DIM, 2, dtype=F32) / SWA_HEAD_DIM)
    lane = jnp.arange(LANES)
    invf = inv_freq[lane % half].reshape(1, LANES)
    sign = jnp.where((lane % SWA_HEAD_DIM) < half, -1.0, 1.0).astype(F32).reshape(1, LANES)
    tm = 1024
    return pl.pallas_call(
        _rope_table_kernel,
        grid=(t // tm,),
        in_specs=[pl.BlockSpec((tm, 1), lambda i: (i, 0)),
                  pl.BlockSpec((1, LANES), lambda i: (0, 0)),
                  pl.BlockSpec((1, LANES), lambda i: (0, 0))],
        out_specs=[pl.BlockSpec((tm, LANES), lambda i: (i, 0))] * 2,
        out_shape=[jax.ShapeDtypeStruct((t, LANES), F32)] * 2,
        compiler_params=_cparams(("parallel",)),
        name="rope_tables",
    )(pos, invf, sign)


def _rmsnorm_matmul_kernel(x_ref, g_ref, w_ref, o_ref, n_ref):
    xf = x_ref[...].astype(F32)
    ms = jnp.mean(xf * xf, axis=-1, keepdims=True)
    n = (xf * lax.rsqrt(ms + EPS) * g_ref[...]).astype(n_ref.dtype)
    n_ref[...] = n
    o_ref[...] = jnp.dot(n, w_ref[...], preferred_element_type=F32).astype(o_ref.dtype)


def _rmsnorm_matmul(x, g, w, *, tm, name):
    t, d = x.shape
    n = w.shape[1]
    return pl.pallas_call(
        _rmsnorm_matmul_kernel,
        grid=(t // tm,),
        in_specs=[pl.BlockSpec((tm, d), lambda i: (i, 0)),
                  pl.BlockSpec((1, d), lambda i: (0, 0)),
                  pl.BlockSpec((d, n), lambda i: (0, 0), pipeline_mode=pl.Buffered(1))],
        out_specs=[pl.BlockSpec((tm, n), lambda i: (i, 0)), pl.BlockSpec((tm, d), lambda i: (i, 0))],
        out_shape=[jax.ShapeDtypeStruct((t, n), BF16), jax.ShapeDtypeStruct((t, d), BF16)],
        compiler_params=_cparams(("parallel",)),
        name=name,
    )(x, g.reshape(1, d), w)


def _rope_epilogue(acc, cos_ref, sin_ref, o_ref, j, kinds, scales):
    def epilogue(tile_kinds, tile_scales):
        needs_rope = any(k != ROPE_NONE for k in tile_kinds)
        cos = cos_ref[...] if needs_rope else None
        sin = sin_ref[...] if needs_rope else None
        for c, (kind, scale) in enumerate(zip(tile_kinds, tile_scales)):
            blk = _rope_tile(acc[:, c * LANES:(c + 1) * LANES], cos, sin, kind)
            if scale != 1.0:
                blk = blk * scale
            o_ref[:, c * LANES:(c + 1) * LANES] = blk.astype(o_ref.dtype)

    if all(k == kinds[0] for k in kinds) and all(s == scales[0] for s in scales):
        epilogue(kinds[0], scales[0])
    else:
        for jj in range(len(kinds)):
            pl.when(j == jj)(functools.partial(epilogue, kinds[jj], scales[jj]))


def _per_tile(values, n_j, per):
    return tuple(tuple(values[jj * per:(jj + 1) * per]) for jj in range(n_j))


def _matmul_rope_kernel(x_ref, w_ref, cos_ref, sin_ref, o_ref, *, kinds, scales):
    acc = jnp.dot(x_ref[...], w_ref[...], preferred_element_type=F32)
    _rope_epilogue(acc, cos_ref, sin_ref, o_ref, 0, (kinds,), (scales,))


def _matmul_rope(x, w, cos, sin, *, tm, kinds, scales, name):
    t, k = x.shape
    n = w.shape[1]
    return pl.pallas_call(
        functools.partial(_matmul_rope_kernel, kinds=tuple(kinds), scales=tuple(scales)),
        grid=(t // tm,),
        in_specs=[pl.BlockSpec((tm, k), lambda i: (i, 0)),
                  pl.BlockSpec((k, n), lambda i: (0, 0), pipeline_mode=pl.Buffered(1)),
                  pl.BlockSpec((tm, LANES), lambda i: (i, 0)),
                  pl.BlockSpec((tm, LANES), lambda i: (i, 0))],
        out_specs=pl.BlockSpec((tm, n), lambda i: (i, 0)),
        out_shape=jax.ShapeDtypeStruct((t, n), BF16),
        compiler_params=_cparams(("parallel",)),
        name=name,
    )(x, w, cos, sin)


def _norm_matmul_kernel(x_ref, g_ref, w_ref, cos_ref, sin_ref, o_ref, n_scr, *, kinds, scales):
    j = pl.program_id(1)

    @pl.when(j == 0)
    def _():
        xf = x_ref[...].astype(F32)
        ms = jnp.mean(xf * xf, axis=-1, keepdims=True)
        n_scr[...] = (xf * lax.rsqrt(ms + EPS) * g_ref[...]).astype(BF16)

    acc = jnp.dot(n_scr[...], w_ref[...].astype(BF16), preferred_element_type=F32)
    _rope_epilogue(acc, cos_ref, sin_ref, o_ref, j, kinds, scales)


def _norm_matmul(x, x_col_block, k, g, w, cos, sin, *, tm, tn, kinds, scales, name):
    t = x.shape[0]
    n = w.shape[1]
    n_j = n // tn
    kinds = _per_tile(kinds, n_j, tn // LANES)
    scales = _per_tile(scales, n_j, tn // LANES)
    return pl.pallas_call(
        functools.partial(_norm_matmul_kernel, kinds=kinds, scales=scales),
        grid=(t // tm, n_j),
        in_specs=[pl.BlockSpec((tm, k), lambda i, j: (i, x_col_block)),
                  pl.BlockSpec((1, k), lambda i, j: (0, 0)),
                  pl.BlockSpec((k, tn), lambda i, j: (0, j)),
                  pl.BlockSpec((tm, LANES), lambda i, j: (i, 0)),
                  pl.BlockSpec((tm, LANES), lambda i, j: (i, 0))],
        out_specs=pl.BlockSpec((tm, tn), lambda i, j: (i, j)),
        out_shape=jax.ShapeDtypeStruct((t, n), BF16),
        scratch_shapes=[pltpu.VMEM((tm, k), BF16)],
        compiler_params=_cparams(("parallel", "arbitrary")),
        name=name,
    )(x, g.reshape(1, k), w, cos, sin)


MLA_SCORE_LOOKAHEAD = 2


def _mla_kernel(q_ref, kv_ref, kpe_ref, o_ref, kcat_scr, *, tq):
    s_len = q_ref.shape[0]
    kcat_scr[:, :MLA_NOPE_DIM] = kv_ref[:, :MLA_NOPE_DIM]
    kcat_scr[:, MLA_NOPE_DIM:] = kpe_ref[...]
    row = lax.broadcasted_iota(jnp.int32, (tq, tq), 0)
    col = lax.broadcasted_iota(jnp.int32, (tq, tq), 1)
    tri = col <= row
    nt = (((1,), (1,)), ((), ()))
    n_blocks = s_len // tq

    def scores(qi):
        lo, hi = qi * tq, (qi + 1) * tq
        q = q_ref[lo:hi, :]
        s_diag = lax.dot_general(q, kcat_scr[lo:hi, :], nt, preferred_element_type=F32)
        s_past = lax.dot_general(q, kcat_scr[:lo, :], nt, preferred_element_type=F32) if qi > 0 else None
        return s_diag, s_past

    queue = [scores(qi) for qi in range(min(MLA_SCORE_LOOKAHEAD, n_blocks))]
    for qi in range(n_blocks):
        lo, hi = qi * tq, (qi + 1) * tq
        s_diag, s_past = queue.pop(0)
        if qi + MLA_SCORE_LOOKAHEAD < n_blocks:
            queue.append(scores(qi + MLA_SCORE_LOOKAHEAD))
        s_diag = jnp.where(tri, s_diag, NEG)
        m = jnp.max(s_diag, axis=-1, keepdims=True)
        if qi > 0:
            m = jnp.maximum(m, jnp.max(s_past, axis=-1, keepdims=True))
            p_past = jnp.exp(s_past - m)
        p_diag = jnp.exp(s_diag - m)
        l = jnp.sum(p_diag, axis=-1, keepdims=True)
        o = jnp.dot(p_diag.astype(BF16), kv_ref[lo:hi, MLA_NOPE_DIM:], preferred_element_type=F32)
        if qi > 0:
            l = l + jnp.sum(p_past, axis=-1, keepdims=True)
            o = o + jnp.dot(p_past.astype(BF16), kv_ref[:lo, MLA_NOPE_DIM:], preferred_element_type=F32)
        o_ref[lo:hi, :] = (o / l).astype(o_ref.dtype)


def _mla_attention(q, kv, z_kv, batch, seq):
    t = batch * seq
    kpe_block = KV_KPE_BLOCK
    return pl.pallas_call(
        functools.partial(_mla_kernel, tq=256),
        grid=(batch, MLA_HEADS),
        in_specs=[pl.BlockSpec((seq, MLA_HEAD_PAD), lambda b, h: (b, h)),
                  pl.BlockSpec((seq, MLA_NOPE_DIM + MLA_V_DIM), lambda b, h: (b, h)),
                  pl.BlockSpec((seq, LANES), lambda b, h: (b, kpe_block))],
        out_specs=pl.BlockSpec((seq, MLA_V_DIM), lambda b, h: (b, h)),
        out_shape=jax.ShapeDtypeStruct((t, MLA_HEADS * MLA_V_DIM), BF16),
        scratch_shapes=[pltpu.VMEM((seq, MLA_HEAD_PAD), BF16)],
        compiler_params=_cparams(("parallel", "arbitrary")),
        name="mla_attention",
    )(q, kv, z_kv)


SWA_BLOCK_GROUP = 5


def _swa_kernel(sinks_ref, q_ref, kv_ref, o_ref, klo, khi, vlo, vhi):
    h = pl.program_id(1)
    s_len = q_ref.shape[0]
    w = SWA_WINDOW
    pairs = SWA_GROUP // 2
    kv = kv_ref[...].astype(F32)
    lane = lax.broadcasted_iota(jnp.int32, kv.shape, 1)
    low = lane < SWA_HEAD_DIM
    rolled = pltpu.roll(kv, SWA_HEAD_DIM, 1)
    pad = jnp.zeros((w, LANES), BF16)
    for scr, val in ((klo, jnp.where(low, kv, 0.0)), (khi, jnp.where(low, 0.0, rolled)),
                     (vlo, jnp.where(low, rolled, 0.0)), (vhi, jnp.where(low, 0.0, kv))):
        scr[:w, :] = pad
        scr[w:, :] = val.astype(BF16)

    rows = pairs * w
    r = lax.broadcasted_iota(jnp.int32, (rows, 2 * w), 0) % w
    c = lax.broadcasted_iota(jnp.int32, (rows, 2 * w), 1)
    band = (c > r) & (c <= r + w)
    pair_of_row = lax.broadcasted_iota(jnp.int32, (rows, 1), 0) // w
    out_low = lax.broadcasted_iota(jnp.int32, (rows, LANES), 1) < SWA_HEAD_DIM

    def sink_column(parity):
        col = jnp.zeros((rows, 1), F32)
        for j in range(pairs):
            col = jnp.where(pair_of_row == j, sinks_ref[h * SWA_GROUP + 2 * j + parity], col)
        return col

    sink_even, sink_odd = sink_column(0), sink_column(1)
    nt = (((1,), (1,)), ((), ()))

    def row0(n):
        return n * w if isinstance(n, int) else pl.multiple_of(n * w, w)

    def scores(n):
        r0 = row0(n)
        qs = jnp.concatenate([q_ref[pl.ds(r0, w), j * LANES:(j + 1) * LANES] for j in range(pairs)], axis=0)
        return (lax.dot_general(qs, klo[pl.ds(r0, 2 * w), :], nt, preferred_element_type=F32),
                lax.dot_general(qs, khi[pl.ds(r0, 2 * w), :], nt, preferred_element_type=F32))

    def finish(n, block_scores, mask):
        r0 = row0(n)

        def head(s, v_scr, sink):
            s = jnp.where(mask, s, NEG)
            m = jnp.maximum(jnp.max(s, axis=-1, keepdims=True), sink)
            p = jnp.exp(s - m)
            l = jnp.sum(p, axis=-1, keepdims=True) + jnp.exp(sink - m)
            o = jnp.dot(p.astype(BF16), v_scr[pl.ds(r0, 2 * w), :], preferred_element_type=F32)
            return o, l

        o_even, l_even = head(block_scores[0], vlo, sink_even)
        o_odd, l_odd = head(block_scores[1], vhi, sink_odd)
        o = (o_even + o_odd) / jnp.where(out_low, l_even, l_odd)
        for j in range(pairs):
            o_ref[pl.ds(r0, w), j * LANES:(j + 1) * LANES] = o[j * w:(j + 1) * w].astype(o_ref.dtype)

    finish(0, scores(0), band & (c >= w))

    def group(g, carry):
        n0 = 1 + g * SWA_BLOCK_GROUP
        pending = scores(n0)
        for u in range(SWA_BLOCK_GROUP):
            current = pending
            if u + 1 < SWA_BLOCK_GROUP:
                pending = scores(n0 + u + 1)
            finish(n0 + u, current, band)
        return carry

    assert (s_len // w - 1) % SWA_BLOCK_GROUP == 0
    lax.fori_loop(0, (s_len // w - 1) // SWA_BLOCK_GROUP, group, 0)


def _swa_attention(z_q, z_kv, sinks, batch, seq):
    t = batch * seq
    gw = SWA_GROUP * SWA_HEAD_DIM
    return pl.pallas_call(
        _swa_kernel,
        grid_spec=pltpu.PrefetchScalarGridSpec(
            num_scalar_prefetch=1,
            grid=(batch, SWA_KV_HEADS),
            in_specs=[pl.BlockSpec((seq, gw), lambda b, h, s: (b, h)),
                      pl.BlockSpec((seq, LANES), lambda b, h, s: (b, h))],
            out_specs=pl.BlockSpec((seq, gw), lambda b, h, s: (b, h)),
            scratch_shapes=[pltpu.VMEM((seq + SWA_WINDOW, LANES), BF16)] * 4),
        out_shape=jax.ShapeDtypeStruct((t, SWA_HEADS * SWA_HEAD_DIM), BF16),
        compiler_params=_cparams(("parallel", "arbitrary")),
        name="swa_attention",
    )(sinks.astype(F32), z_q, z_kv)


def _out_proj_kernel(x_ref, a_ref, s_ref, wt_ref, wb_ref, h_ref):
    h_ref[...] = (x_ref[...] + jnp.dot(a_ref[...], wt_ref[...].astype(BF16), preferred_element_type=F32)
                  + jnp.dot(s_ref[...], wb_ref[...].astype(BF16), preferred_element_type=F32))


def _out_proj(x2, mla_o, swa_o, w_out, *, tm=1024, tn=512):
    t, d = x2.shape
    half = mla_o.shape[1]
    return pl.pallas_call(
        _out_proj_kernel,
        grid=(t // tm, d // tn),
        in_specs=[pl.BlockSpec((tm, tn), lambda i, j: (i, j)),
                  pl.BlockSpec((tm, half), lambda i, j: (i, 0)),
                  pl.BlockSpec((tm, half), lambda i, j: (i, 0)),
                  pl.BlockSpec((half, tn), lambda i, j: (0, j)),
                  pl.BlockSpec((half, tn), lambda i, j: (1, j))],
        out_specs=pl.BlockSpec((tm, tn), lambda i, j: (i, j)),
        out_shape=jax.ShapeDtypeStruct((t, d), F32),
        compiler_params=_cparams(("parallel", "parallel")),
        name="out_proj",
    )(x2, mla_o, swa_o, w_out, w_out)


ROW_CHUNKS = D_MODEL // (2 * LANES)
ROW_PITCH = 24
U32 = jnp.uint32
HIGH_HALF = 0xFFFF0000


def _pack_pair(lo, hi):
    lo_bits = lax.bitcast_convert_type(lo.astype(BF16).astype(F32), U32)
    hi_bits = lax.bitcast_convert_type(hi.astype(BF16).astype(F32), U32)
    return (lo_bits >> 16) | (hi_bits & U32(HIGH_HALF))


def _unpack_pair(words):
    return (lax.bitcast_convert_type(words << 16, F32),
            lax.bitcast_convert_type(words & U32(HIGH_HALF), F32))


def _chunk_cols(c):
    return (slice(2 * c * LANES, (2 * c + 1) * LANES), slice((2 * c + 1) * LANES, (2 * c + 2) * LANES))


def _split_bf16(v):
    hi = v.astype(BF16)
    return hi, (v - hi.astype(F32)).astype(BF16)


def _ffn_norm_router_kernel(h_ref, g_ref, wr_ref, br_ref, o_ref, lg_ref):
    tm = h_ref.shape[0]
    h = h_ref[...]
    inv = lax.rsqrt(jnp.mean(h * h, axis=-1, keepdims=True) + EPS)
    hg_hi, hg_lo = _split_bf16(h * g_ref[...])
    wr_hi, wr_lo = _split_bf16(wr_ref[...])
    acc = (jnp.dot(hg_hi, wr_hi, preferred_element_type=F32)
           + (jnp.dot(hg_hi, wr_lo, preferred_element_type=F32)
              + jnp.dot(hg_lo, wr_hi, preferred_element_type=F32)))
    lg_ref[...] = acc * inv + br_ref[...]
    for c in range(ROW_CHUNKS):
        lo, hi = _chunk_cols(c)
        o_ref[pl.ds(c, tm, stride=ROW_PITCH), :] = _pack_pair(h_ref[:, lo] * inv * g_ref[:, lo],
                                                              h_ref[:, hi] * inv * g_ref[:, hi])
    for c in range(ROW_CHUNKS, ROW_PITCH):
        o_ref[pl.ds(c, tm, stride=ROW_PITCH), :] = jnp.zeros((tm, LANES), o_ref.dtype)


def _ffn_norm_router(h, g, w_r, b_r, *, tm=256):
    t, d = h.shape
    return pl.pallas_call(
        _ffn_norm_router_kernel,
        grid=(t // tm,),
        in_specs=[pl.BlockSpec((tm, d), lambda i: (i, 0)),
                  pl.BlockSpec((1, d), lambda i: (0, 0)),
                  pl.BlockSpec((d, LANES), lambda i: (0, 0)),
                  pl.BlockSpec((1, LANES), lambda i: (0, 0))],
        out_specs=[pl.BlockSpec((tm * ROW_PITCH, LANES), lambda i: (i, 0)),
                   pl.BlockSpec((tm, LANES), lambda i: (i, 0))],
        out_shape=[jax.ShapeDtypeStruct((t * ROW_PITCH, LANES), U32),
                   jax.ShapeDtypeStruct((t, LANES), F32)],
        compiler_params=_cparams(("parallel",)),
        name="ffn_norm_router",
    )(h, g.reshape(1, d), w_r, b_r)


MOE_UP_STEPS = D_EXPERT // MOE_CHUNK
MOE_DOWN_STEPS = 2
MOE_STEPS = MOE_UP_STEPS + MOE_DOWN_STEPS
MOE_DOWN_TILE = D_MODEL // MOE_DOWN_STEPS


LANES_LOG2 = LANES.bit_length() - 1
IDX_TILE_ROWS = 8
IDX_TILE = IDX_TILE_ROWS * LANES
IDX_TILE_LOG2 = IDX_TILE.bit_length() - 1
IDX_WINDOW_ROWS = 2 * IDX_TILE_ROWS
assert MOE_ROWS <= IDX_TILE
ROW_UNROLL_LOG2 = 3
ROW_UNROLL = 1 << ROW_UNROLL_LOG2


def _moe_kernel(item_e, item_cnt, item_row0, dstk_ref, hn_ref, wg_ref, wu_ref, wd_ref, pairs_ref,
                idx_win, x_stage, x_bf, hid, out_stage, sem_idx, sem_in, sem_out, *, n_tokens):
    del item_e
    w = pl.program_id(0)
    s = pl.program_id(1)
    n_items = pl.num_programs(0)
    cnt = item_cnt[w]
    nxt_item = jnp.minimum(w + 1, n_items - 1)
    has_next = (w + 1 < n_items) & (item_cnt[nxt_item] > 0)
    cur = w & 1
    nxt = 1 - cur

    def idx_copy(item, buf):
        first = (item_row0[item] >> IDX_TILE_LOG2) * IDX_TILE_ROWS
        return pltpu.make_async_copy(dstk_ref.at[pl.ds(pl.multiple_of(first, IDX_TILE_ROWS), IDX_WINDOW_ROWS)],
                                     idx_win.at[buf], sem_idx)

    def unrolled_rows(n, one_row):
        groups = n >> ROW_UNROLL_LOG2

        def group(g, carry):
            for u in range(ROW_UNROLL):
                one_row(g * ROW_UNROLL + u)
            return carry

        def single(r, carry):
            one_row(r)
            return carry

        lax.fori_loop(0, groups, group, 0)
        lax.fori_loop(groups * ROW_UNROLL, n, single, 0)

    def for_rows(item, buf, fn):
        base = item_row0[item] & (IDX_TILE - 1)

        def one_row(r):
            off = base + r
            fn(idx_win[buf, off >> LANES_LOG2, off & (LANES - 1)], r)

        unrolled_rows(item_cnt[item], one_row)

    def row_in(tok, r):
        return pltpu.make_async_copy(hn_ref.at[pl.ds(pl.multiple_of(tok * ROW_PITCH, 8), ROW_PITCH)],
                                     x_stage.at[pl.ds(pl.multiple_of(r * ROW_PITCH, 8), ROW_PITCH)], sem_in)

    def row_out(slot, r):
        return pltpu.make_async_copy(out_stage.at[pl.ds(pl.multiple_of(r * ROW_PITCH, 8), ROW_PITCH)],
                                     pairs_ref.at[pl.ds(pl.multiple_of(slot * ROW_PITCH, 8), ROW_PITCH)], sem_out)

    def gather_start(item, buf):
        for_rows(item, buf, lambda v, r: row_in(v & (n_tokens - 1), r).start())

    def scatter_start(item, buf):
        for_rows(item, buf, lambda v, r: row_out(v, r).start())

    def wait_rows(n, copy):
        unrolled_rows(n, lambda r: copy(0, r).wait())

    @pl.when((w == 0) & (s == 0))
    def _():
        x_stage[...] = jnp.zeros(x_stage.shape, x_stage.dtype)
        out_stage[...] = jnp.zeros(out_stage.shape, out_stage.dtype)

        @pl.when(cnt > 0)
        def _():
            first_window = idx_copy(0, 0)
            first_window.start()
            first_window.wait()
            gather_start(0, 0)

    @pl.when((s == 0) & (cnt > 0))
    def _():
        wait_rows(cnt, row_in)

    @pl.when((s == 0) & has_next)
    def _():
        idx_copy(nxt_item, nxt).start()

    @pl.when((s == MOE_UP_STEPS) & (w > 0) & (cnt > 0))
    def _():
        wait_rows(item_cnt[jnp.maximum(w - 1, 0)], row_out)

    def compute(m):
        @pl.when(s == 0)
        def _():
            for c in range(ROW_CHUNKS):
                lo, hi = _chunk_cols(c)
                x_lo, x_hi = _unpack_pair(x_stage[pl.ds(c, m, stride=ROW_PITCH), :])
                x_bf[:m, lo] = x_lo.astype(BF16)
                x_bf[:m, hi] = x_hi.astype(BF16)

        @pl.when(s < MOE_UP_STEPS)
        def _():
            x = x_bf[:m, :]
            g = jnp.dot(x, wg_ref[...].astype(BF16), preferred_element_type=F32)
            u = jnp.dot(x, wu_ref[...].astype(BF16), preferred_element_type=F32)
            hid[s, :m, :] = ((g * jax.nn.sigmoid(g)) * u).astype(BF16)

        @pl.when(s >= MOE_UP_STEPS)
        def _():
            hidden = jnp.concatenate([hid[c, :m, :] for c in range(MOE_UP_STEPS)], axis=1)
            y = jnp.dot(hidden, wd_ref[...].astype(BF16), preferred_element_type=F32)
            tile_chunks = MOE_DOWN_TILE // (2 * LANES)
            base = (s - MOE_UP_STEPS) * tile_chunks
            for c in range(tile_chunks):
                lo, hi = _chunk_cols(c)
                out_stage[pl.ds(base + c, m, stride=ROW_PITCH), :] = _pack_pair(y[:, lo], y[:, hi])

    for below, m in zip((0,) + MOE_ROW_VARIANTS[:-1], MOE_ROW_VARIANTS):
        pl.when((cnt > below) & (cnt <= m))(functools.partial(compute, m))

    @pl.when((s == 1) & has_next)
    def _():
        idx_copy(nxt_item, nxt).wait()
        gather_start(nxt_item, nxt)

    @pl.when((s == MOE_STEPS - 1) & (cnt > 0))
    def _():
        scatter_start(w, cur)

        @pl.when(jnp.logical_not(has_next))
        def _():
            wait_rows(cnt, row_out)


def _moe_experts(hn_lin, w_gate, w_up, w_down, item_e, item_cnt, item_row0, dstk, n_tokens):
    n_items = item_e.shape[0]
    d = D_MODEL
    assert n_tokens & (n_tokens - 1) == 0, "token ids are unpacked from k * n_tokens + token with a mask"

    def up_chunk(s, cnt_ref, w):
        return jnp.where(cnt_ref[w] > 0, jnp.minimum(s, MOE_UP_STEPS - 1), MOE_UP_STEPS - 1)

    def down_block(w, s, e, cn):
        own = (cn[w] > 0) & (s >= MOE_UP_STEPS)
        return (jnp.where(own, e[w], e[jnp.maximum(w - 1, 0)]), 0,
                jnp.where(own, s - MOE_UP_STEPS, MOE_DOWN_STEPS - 1))

    return pl.pallas_call(
        functools.partial(_moe_kernel, n_tokens=n_tokens),
        grid_spec=pltpu.PrefetchScalarGridSpec(
            num_scalar_prefetch=3,
            grid=(n_items, MOE_STEPS),
            in_specs=[pl.BlockSpec(memory_space=pl.ANY),
                      pl.BlockSpec(memory_space=pl.ANY),
                      pl.BlockSpec((None, d, MOE_CHUNK), lambda w, s, e, cn, r0: (e[w], 0, up_chunk(s, cn, w))),
                      pl.BlockSpec((None, d, MOE_CHUNK), lambda w, s, e, cn, r0: (e[w], 0, up_chunk(s, cn, w))),
                      pl.BlockSpec((None, D_EXPERT, MOE_DOWN_TILE), lambda w, s, e, cn, r0: down_block(w, s, e, cn))],
            out_specs=pl.BlockSpec(memory_space=pl.ANY),
            scratch_shapes=[pltpu.SMEM((2, IDX_WINDOW_ROWS, LANES), jnp.int32),
                            pltpu.VMEM((MOE_ROWS * ROW_PITCH, LANES), U32),
                            pltpu.VMEM((MOE_ROWS, d), BF16),
                            pltpu.VMEM((MOE_UP_STEPS, MOE_ROWS, MOE_CHUNK), BF16),
                            pltpu.VMEM((MOE_ROWS * ROW_PITCH, LANES), U32),
                            pltpu.SemaphoreType.DMA(()),
                            pltpu.SemaphoreType.DMA(()),
                            pltpu.SemaphoreType.DMA(())]),
        out_shape=jax.ShapeDtypeStruct((TOP_K * n_tokens * ROW_PITCH, LANES), U32),
        compiler_params=_cparams(("arbitrary", "arbitrary")),
        name="moe_experts",
    )(item_e, item_cnt, item_row0, dstk, hn_lin, w_gate, w_up, w_down)


def _combine_kernel(h_ref, p0_ref, p1_ref, gate_ref, g_ref, o_ref):
    tm, d = h_ref.shape
    g0 = gate_ref[:, 0:1]
    g1 = gate_ref[:, 1:2]
    ss = jnp.zeros((tm, 1), F32)
    for c in range(ROW_CHUNKS):
        a = _unpack_pair(p0_ref[pl.ds(c, tm, stride=ROW_PITCH), :])
        b = _unpack_pair(p1_ref[pl.ds(c, tm, stride=ROW_PITCH), :])
        for sl, a_half, b_half in zip(_chunk_cols(c), a, b):
            y = h_ref[:, sl] + (a_half * g0 + b_half * g1)
            o_ref[:, sl] = y
            ss = ss + jnp.sum(y * y, axis=-1, keepdims=True)
    inv = lax.rsqrt(ss / d + EPS)
    o_ref[...] = o_ref[...] * inv * g_ref[...]


def _combine(h, pairs, gates, g_final, *, tm=256):
    t, d = h.shape
    k1 = t // tm
    return pl.pallas_call(
        _combine_kernel,
        grid=(t // tm,),
        in_specs=[pl.BlockSpec((tm, d), lambda i: (i, 0)),
                  pl.BlockSpec((tm * ROW_PITCH, LANES), lambda i: (i, 0)),
                  pl.BlockSpec((tm * ROW_PITCH, LANES), lambda i: (k1 + i, 0)),
                  pl.BlockSpec((tm, TOP_K), lambda i: (i, 0)),
                  pl.BlockSpec((1, d), lambda i: (0, 0))],
        out_specs=pl.BlockSpec((tm, d), lambda i: (i, 0)),
        out_shape=jax.ShapeDtypeStruct((t, d), F32),
        compiler_params=_cparams(("parallel",)),
        name="combine_final_norm",
    )(h, pairs, pairs, gates, g_final.reshape(1, d))


def _route(logits, t):
    group_prob = jax.nn.softmax(logits[:, :N_GROUPS], axis=-1)
    g_sel = jnp.argmax(group_prob, axis=-1)
    g_p = jnp.max(group_prob, axis=-1, keepdims=True)
    e_logits = logits[:, N_GROUPS:N_GROUPS + N_EXPERTS]
    in_group = (jnp.arange(N_EXPERTS) // EXPERTS_PER_GROUP)[None, :] == g_sel[:, None]
    e_prob = jax.nn.softmax(jnp.where(in_group, e_logits, -jnp.inf), axis=-1)
    key = jnp.where(in_group, e_prob, -1.0)
    i1 = jnp.argmax(key, axis=-1)
    key2 = jnp.where(jnp.arange(N_EXPERTS)[None, :] == i1[:, None], -2.0, key)
    i2 = jnp.argmax(key2, axis=-1)
    top_p = jnp.stack([jnp.max(key, axis=-1), jnp.max(key2, axis=-1)], axis=-1)
    expert_idx = jnp.stack([i1, i2], axis=-1)
    assert TOP_K == 2
    gates = g_p * top_p / jnp.sum(top_p, axis=-1, keepdims=True)

    tk = t * TOP_K
    n_items_max = tk // MOE_ROWS + N_EXPERTS
    flat_e = expert_idx.reshape(tk).astype(jnp.int32)
    order = jnp.argsort(flat_e).astype(jnp.int32)
    dstk = (order % TOP_K) * t + order // TOP_K
    dstk = jnp.concatenate([dstk, jnp.zeros((2 * IDX_TILE,), jnp.int32)]).reshape(-1, LANES)
    counts = jnp.sum((flat_e[:, None] == jnp.arange(N_EXPERTS, dtype=jnp.int32)[None, :]).astype(jnp.int32), axis=0)
    start = jnp.cumsum(counts) - counts
    items_of = (counts + MOE_ROWS - 1) // MOE_ROWS
    item_end = jnp.cumsum(items_of)
    wi = jnp.arange(n_items_max, dtype=jnp.int32)
    e_w = jnp.minimum(jnp.sum((item_end[None, :] <= wi[:, None]).astype(jnp.int32), axis=1), N_EXPERTS - 1)
    part = wi - (item_end[e_w] - items_of[e_w])
    active = wi < item_end[-1]
    item_cnt = jnp.where(active, jnp.clip(counts[e_w] - part * MOE_ROWS, 0, MOE_ROWS), 0).astype(jnp.int32)
    item_row0 = jnp.where(active, start[e_w] + part * MOE_ROWS, 0).astype(jnp.int32)
    item_e = jnp.where(active, e_w, e_w[jnp.maximum(item_end[-1] - 1, 0)]).astype(jnp.int32)
    return gates.astype(F32), item_e, item_cnt, item_row0, dstk


def _take_pieces(w, pieces, axis):
    parts = []
    for start, width in pieces:
        if start is None:
            shape = (w.shape[0], width) if axis == 1 else (width, w.shape[1])
            parts.append(jnp.zeros(shape, w.dtype))
        else:
            parts.append(w[:, start:start + width] if axis == 1 else w[start:start + width, :])
    return jnp.concatenate(parts, axis=axis)


def _column_layout_kernel(w_ref, *o_refs, groups):
    w = w_ref[...]
    for o_ref, pieces in zip(o_refs, groups):
        o_ref[...] = _take_pieces(w, pieces, 1).astype(o_ref.dtype)


def _column_layout_from_transpose_kernel(wt_ref, *o_refs, groups):
    wt = wt_ref[...]
    for o_ref, pieces in zip(o_refs, groups):
        o_ref[...] = _take_pieces(wt, pieces, 0).T.astype(o_ref.dtype)


def _column_layout(w, layer, groups, *, tk=256, name, read_transposed=False):
    _, d, n = w.shape
    groups = tuple(tuple(g) for g in groups)
    widths = [sum(width for _, width in g) for g in groups]
    if read_transposed:
        body = functools.partial(_column_layout_from_transpose_kernel, groups=groups)
        w = jnp.swapaxes(w, 1, 2)
        in_spec = pl.BlockSpec((None, n, tk), lambda i: (layer, 0, i))
    else:
        body = functools.partial(_column_layout_kernel, groups=groups)
        in_spec = pl.BlockSpec((None, tk, n), lambda i: (layer, i, 0))
    return pl.pallas_call(
        body,
        grid=(d // tk,),
        in_specs=[in_spec],
        out_specs=[pl.BlockSpec((tk, width), lambda i: (i, 0)) for width in widths],
        out_shape=[jax.ShapeDtypeStruct((d, width), BF16) for width in widths],
        compiler_params=_cparams(("parallel",)),
        name=name,
    )(w)


def _prep_in_weight(w_in, layer):
    off_kv = MLA_Q_RANK + MLA_KV_RANK
    off_kpe = off_kv + MLA_ROPE_DIM
    off_sq = off_kpe + SWA_HEADS * SWA_HEAD_DIM
    off_sk = off_sq + SWA_KV_HEADS * SWA_HEAD_DIM
    kv_group = []
    for h in range(SWA_KV_HEADS):
        kv_group += [(off_sq + h * SWA_HEAD_DIM, SWA_HEAD_DIM), (off_sk + h * SWA_HEAD_DIM, SWA_HEAD_DIM)]
    kv_group += [(off_kv, MLA_ROPE_DIM), (None, LANES - MLA_ROPE_DIM)]
    groups = [[(0, off_kv)], [(off_kpe, off_sq - off_kpe)], kv_group]
    return _column_layout(w_in, layer, groups, name="in_weight_layout", read_transposed=True)


def _prep_q_weight(w_q_up, layer):
    pieces = []
    for h in range(MLA_HEADS):
        pieces += [(h * MLA_QK_DIM, MLA_QK_DIM), (None, MLA_HEAD_PAD - MLA_QK_DIM)]
    return _column_layout(w_q_up, layer, [pieces], name="q_weight_layout")[0]


def kernel(x, positions, g_attn_norm, w_in, g_q_norm, w_q_up, g_kv_norm, w_kv_up, sinks, w_out,
           g_ffn_norm, w_router_group, b_router_group, w_router_expert, b_router_expert,
           w_expert_gate, w_expert_up, w_expert_down, g_final_norm):
    batch, seq, d = x.shape
    t = batch * seq
    assert g_attn_norm.shape[0] == 1, "the final rmsnorm is fused into the (single) layer's combine"
    h = x.reshape(t, d)
    cos, sin = _rope_tables(positions)
    for l in range(g_attn_norm.shape[0]):
        w_lat, w_sq, w_skv = _prep_in_weight(w_in, l)
        z_lat, n = _rmsnorm_matmul(h, g_attn_norm[l], w_lat, tm=512, name="attn_norm_in_proj_latents")

        def in_proj(w, kind, scale, name):
            blocks = w.shape[1] // LANES
            return _matmul_rope(n, w, cos, sin, tm=512, kinds=[kind] * blocks, scales=[scale] * blocks, name=name)

        z_sq = in_proj(w_sq, ROPE_FULL, SWA_HEAD_DIM ** -0.5, "in_proj_swa_q")
        z_skv = in_proj(w_skv, ROPE_HALF, 1.0, "in_proj_swa_kv")
        q_kinds = [ROPE_NONE, ROPE_HALF] * MLA_HEADS
        q = _norm_matmul(z_lat, 0, MLA_Q_RANK, g_q_norm[l], _prep_q_weight(w_q_up, l), cos, sin,
                         tm=1024, tn=2048, kinds=q_kinds, scales=[MLA_QK_DIM ** -0.5] * len(q_kinds), name="q_up")
        kv_kinds = [ROPE_NONE] * (w_kv_up.shape[2] // LANES)
        kv = _norm_matmul(z_lat, MLA_Q_RANK // MLA_KV_RANK, MLA_KV_RANK, g_kv_norm[l], w_kv_up[l], cos, sin,
                          tm=1024, tn=2048, kinds=kv_kinds, scales=[1.0] * len(kv_kinds), name="kv_up")
        mla_o = _mla_attention(q, kv, z_skv, batch, seq)
        swa_o = _swa_attention(z_sq, z_skv, sinks[l], batch, seq)
        w_r = jnp.concatenate([w_router_group[l], w_router_expert[l],
                               jnp.zeros((d, LANES - N_GROUPS - N_EXPERTS), F32)], axis=1)
        b_r = jnp.concatenate([b_router_group[l], b_router_expert[l],
                               jnp.zeros((LANES - N_GROUPS - N_EXPERTS,), F32)]).reshape(1, LANES)
        h = _out_proj(h, mla_o, swa_o, w_out[l])
        hn, logits = _ffn_norm_router(h, g_ffn_norm[l], w_r, b_r)
        gates, item_e, item_cnt, item_row0, dstk = _route(logits, t)
        pairs = _moe_experts(hn, w_expert_gate[l], w_expert_up[l], w_expert_down[l],
                             item_e, item_cnt, item_row0, dstk, t)
        out = _combine(h, pairs, gates, g_final_norm)
    return out.reshape(batch, seq, d)
```

```python
import functools

import jax
import jax.numpy as jnp
from jax import lax
from jax.experimental import pallas as pl
from jax.experimental.pallas import tpu as pltpu

F32 = jnp.float32
BF16 = jnp.bfloat16

D_MODEL = 4096
EPS = 1e-6
ROPE_THETA = 10000.0
NEG = -1e30
MLA_HEADS = 16
MLA_Q_RANK = 1024
MLA_KV_RANK = 512
MLA_NOPE_DIM = 128
MLA_ROPE_DIM = 64
MLA_V_DIM = 128
MLA_QK_DIM = MLA_NOPE_DIM + MLA_ROPE_DIM
MLA_HEAD_PAD = 256
SWA_HEADS = 32
SWA_KV_HEADS = 4
SWA_GROUP = SWA_HEADS // SWA_KV_HEADS
SWA_HEAD_DIM = 64
SWA_WINDOW = 128
N_GROUPS = 8
EXPERTS_PER_GROUP = 8
N_EXPERTS = N_GROUPS * EXPERTS_PER_GROUP
TOP_K = 2
D_EXPERT = 1024

LANES = 128
KV_KPE_BLOCK = SWA_KV_HEADS
ROPE_NONE, ROPE_FULL, ROPE_HALF = 0, 1, 2

MOE_ROWS = 512
MOE_ROW_VARIANTS = (256, 320, 384, MOE_ROWS)
MOE_CHUNK = 256
VMEM_LIMIT = 56 * 1024 * 1024


def _cparams(sem, vmem=VMEM_LIMIT):
    return pltpu.CompilerParams(dimension_semantics=sem, vmem_limit_bytes=vmem)


def _swap32(x):
    lane = lax.broadcasted_iota(jnp.int32, x.shape, 1)
    fwd = pltpu.roll(x, LANES - 32, 1)
    bwd = pltpu.roll(x, 32, 1)
    return jnp.where((lane % 64) < 32, fwd, bwd)


def _rope_tile(x, cos, sin, kind):
    if kind == ROPE_NONE:
        return x
    if kind == ROPE_HALF:
        lane = lax.broadcasted_iota(jnp.int32, x.shape, 1)
        cos = jnp.where(lane < 64, cos, 1.0)
        sin = jnp.where(lane < 64, sin, 0.0)
    return x * cos + _swap32(x) * sin


def _rope_table_kernel(pos_ref, invf_ref, sign_ref, cos_ref, sin_ref):
    ang = pos_ref[...].astype(F32) * invf_ref[...]
    cos_ref[...] = jnp.cos(ang)
    sin_ref[...] = jnp.sin(ang) * sign_ref[...]


def _rope_tables(positions):
    t = positions.size
    pos = positions.reshape(t, 1)
    half = SWA_HEAD_DIM // 2
    inv_freq = ROPE_THETA ** (-jnp.arange(0, SWA_HEAD_DIM, 2, dtype=F32) / SWA_HEAD_DIM)
    lane = jnp.arange(LANES)
    invf = inv_freq[lane % half].reshape(1, LANES)
    sign = jnp.where((lane % SWA_HEAD_DIM) < half, -1.0, 1.0).astype(F32).reshape(1, LANES)
    tm = 1024
    return pl.pallas_call(
        _rope_table_kernel,
        grid=(t // tm,),
        in_specs=[pl.BlockSpec((tm, 1), lambda i: (i, 0)),
                  pl.BlockSpec((1, LANES), lambda i: (0, 0)),
                  pl.BlockSpec((1, LANES), lambda i: (0, 0))],
        out_specs=[pl.BlockSpec((tm, LANES), lambda i: (i, 0))] * 2,
        out_shape=[jax.ShapeDtypeStruct((t, LANES), F32)] * 2,
        compiler_params=_cparams(("parallel",)),
        name="rope_tables",
    )(pos, invf, sign)


def _rmsnorm_matmul_kernel(x_ref, g_ref, w_ref, o_ref, n_ref):
    xf = x_ref[...].astype(F32)
    ms = jnp.mean(xf * xf, axis=-1, keepdims=True)
    n = (xf * lax.rsqrt(ms + EPS) * g_ref[...]).astype(n_ref.dtype)
    n_ref[...] = n
    o_ref[...] = jnp.dot(n, w_ref[...], preferred_element_type=F32).astype(o_ref.dtype)


def _rmsnorm_matmul(x, g, w, *, tm, name):
    t, d = x.shape
    n = w.shape[1]
    return pl.pallas_call(
        _rmsnorm_matmul_kernel,
        grid=(t // tm,),
        in_specs=[pl.BlockSpec((tm, d), lambda i: (i, 0)),
                  pl.BlockSpec((1, d), lambda i: (0, 0)),
                  pl.BlockSpec((d, n), lambda i: (0, 0), pipeline_mode=pl.Buffered(1))],
        out_specs=[pl.BlockSpec((tm, n), lambda i: (i, 0)), pl.BlockSpec((tm, d), lambda i: (i, 0))],
        out_shape=[jax.ShapeDtypeStruct((t, n), BF16), jax.ShapeDtypeStruct((t, d), BF16)],
        compiler_params=_cparams(("parallel",)),
        name=name,
    )(x, g.reshape(1, d), w)


def _rope_epilogue(acc, cos_ref, sin_ref, o_ref, j, kinds, scales):
    def epilogue(tile_kinds, tile_scales):
        needs_rope = any(k != ROPE_NONE for k in tile_kinds)
        cos = cos_ref[...] if needs_rope else None
        sin = sin_ref[...] if needs_rope else None
        for c, (kind, scale) in enumerate(zip(tile_kinds, tile_scales)):
            blk = _rope_tile(acc[:, c * LANES:(c + 1) * LANES], cos, sin, kind)
            if scale != 1.0:
                blk = blk * scale
            o_ref[:, c * LANES:(c + 1) * LANES] = blk.astype(o_ref.dtype)

    if all(k == kinds[0] for k in kinds) and all(s == scales[0] for s in scales):
        epilogue(kinds[0], scales[0])
    else:
        for jj in range(len(kinds)):
            pl.when(j == jj)(functools.partial(epilogue, kinds[jj], scales[jj]))


def _per_tile(values, n_j, per):
    return tuple(tuple(values[jj * per:(jj + 1) * per]) for jj in range(n_j))


def _matmul_rope_kernel(x_ref, w_ref, cos_ref, sin_ref, o_ref, *, kinds, scales):
    acc = jnp.dot(x_ref[...], w_ref[...], preferred_element_type=F32)
    _rope_epilogue(acc, cos_ref, sin_ref, o_ref, 0, (kinds,), (scales,))


def _matmul_rope(x, w, cos, sin, *, tm, kinds, scales, name):
    t, k = x.shape
    n = w.shape[1]
    return pl.pallas_call(
        functools.partial(_matmul_rope_kernel, kinds=tuple(kinds), scales=tuple(scales)),
        grid=(t // tm,),
        in_specs=[pl.BlockSpec((tm, k), lambda i: (i, 0)),
                  pl.BlockSpec((k, n), lambda i: (0, 0), pipeline_mode=pl.Buffered(1)),
                  pl.BlockSpec((tm, LANES), lambda i: (i, 0)),
                  pl.BlockSpec((tm, LANES), lambda i: (i, 0))],
        out_specs=pl.BlockSpec((tm, n), lambda i: (i, 0)),
        out_shape=jax.ShapeDtypeStruct((t, n), BF16),
        compiler_params=_cparams(("parallel",)),
        name=name,
    )(x, w, cos, sin)


def _norm_matmul_kernel(x_ref, g_ref, w_ref, cos_ref, sin_ref, o_ref, n_scr, *, kinds, scales):
    j = pl.program_id(1)

    @pl.when(j == 0)
    def _():
        xf = x_ref[...].astype(F32)
        ms = jnp.mean(xf * xf, axis=-1, keepdims=True)
        n_scr[...] = (xf * lax.rsqrt(ms + EPS) * g_ref[...]).astype(BF16)

    acc = jnp.dot(n_scr[...], w_ref[...].astype(BF16), preferred_element_type=F32)
    _rope_epilogue(acc, cos_ref, sin_ref, o_ref, j, kinds, scales)


def _norm_matmul(x, x_col_block, k, g, w, cos, sin, *, tm, tn, kinds, scales, name):
    t = x.shape[0]
    n = w.shape[1]
    n_j = n // tn
    kinds = _per_tile(kinds, n_j, tn // LANES)
    scales = _per_tile(scales, n_j, tn // LANES)
    return pl.pallas_call(
        functools.partial(_norm_matmul_kernel, kinds=kinds, scales=scales),
        grid=(t // tm, n_j),
        in_specs=[pl.BlockSpec((tm, k), lambda i, j: (i, x_col_block)),
                  pl.BlockSpec((1, k), lambda i, j: (0, 0)),
                  pl.BlockSpec((k, tn), lambda i, j: (0, j)),
                  pl.BlockSpec((tm, LANES), lambda i, j: (i, 0)),
                  pl.BlockSpec((tm, LANES), lambda i, j: (i, 0))],
        out_specs=pl.BlockSpec((tm, tn), lambda i, j: (i, j)),
        out_shape=jax.ShapeDtypeStruct((t, n), BF16),
        scratch_shapes=[pltpu.VMEM((tm, k), BF16)],
        compiler_params=_cparams(("parallel", "arbitrary")),
        name=name,
    )(x, g.reshape(1, k), w, cos, sin)


MLA_SCORE_LOOKAHEAD = 2


def _mla_kernel(q_ref, kv_ref, kpe_ref, o_ref, kcat_scr, *, tq):
    s_len = q_ref.shape[0]
    kcat_scr[:, :MLA_NOPE_DIM] = kv_ref[:, :MLA_NOPE_DIM]
    kcat_scr[:, MLA_NOPE_DIM:] = kpe_ref[...]
    row = lax.broadcasted_iota(jnp.int32, (tq, tq), 0)
    col = lax.broadcasted_iota(jnp.int32, (tq, tq), 1)
    tri = col <= row
    nt = (((1,), (1,)), ((), ()))
    n_blocks = s_len // tq

    def scores(qi):
        lo, hi = qi * tq, (qi + 1) * tq
        q = q_ref[lo:hi, :]
        s_diag = lax.dot_general(q, kcat_scr[lo:hi, :], nt, preferred_element_type=F32)
        s_past = lax.dot_general(q, kcat_scr[:lo, :], nt, preferred_element_type=F32) if qi > 0 else None
        return s_diag, s_past

    queue = [scores(qi) for qi in range(min(MLA_SCORE_LOOKAHEAD, n_blocks))]
    for qi in range(n_blocks):
        lo, hi = qi * tq, (qi + 1) * tq
        s_diag, s_past = queue.pop(0)
        if qi + MLA_SCORE_LOOKAHEAD < n_blocks:
            queue.append(scores(qi + MLA_SCORE_LOOKAHEAD))
        s_diag = jnp.where(tri, s_diag, NEG)
        m = jnp.max(s_diag, axis=-1, keepdims=True)
        if qi > 0:
            m = jnp.maximum(m, jnp.max(s_past, axis=-1, keepdims=True))
            p_past = jnp.exp(s_past - m)
        p_diag = jnp.exp(s_diag - m)
        l = jnp.sum(p_diag, axis=-1, keepdims=True)
        o = jnp.dot(p_diag.astype(BF16), kv_ref[lo:hi, MLA_NOPE_DIM:], preferred_element_type=F32)
        if qi > 0:
            l = l + jnp.sum(p_past, axis=-1, keepdims=True)
            o = o + jnp.dot(p_past.astype(BF16), kv_ref[:lo, MLA_NOPE_DIM:], preferred_element_type=F32)
        o_ref[lo:hi, :] = (o / l).astype(o_ref.dtype)


def _mla_attention(q, kv, z_kv, batch, seq):
    t = batch * seq
    kpe_block = KV_KPE_BLOCK
    return pl.pallas_call(
        functools.partial(_mla_kernel, tq=256),
        grid=(batch, MLA_HEADS),
        in_specs=[pl.BlockSpec((seq, MLA_HEAD_PAD), lambda b, h: (b, h)),
                  pl.BlockSpec((seq, MLA_NOPE_DIM + MLA_V_DIM), lambda b, h: (b, h)),
                  pl.BlockSpec((seq, LANES), lambda b, h: (b, kpe_block))],
        out_specs=pl.BlockSpec((seq, MLA_V_DIM), lambda b, h: (b, h)),
        out_shape=jax.ShapeDtypeStruct((t, MLA_HEADS * MLA_V_DIM), BF16),
        scratch_shapes=[pltpu.VMEM((seq, MLA_HEAD_PAD), BF16)],
        compiler_params=_cparams(("parallel", "arbitrary")),
        name="mla_attention",
    )(q, kv, z_kv)


SWA_BLOCK_GROUP = 5


def _swa_kernel(sinks_ref, q_ref, kv_ref, o_ref, klo, khi, vlo, vhi):
    h = pl.program_id(1)
    s_len = q_ref.shape[0]
    w = SWA_WINDOW
    pairs = SWA_GROUP // 2
    kv = kv_ref[...].astype(F32)
    lane = lax.broadcasted_iota(jnp.int32, kv.shape, 1)
    low = lane < SWA_HEAD_DIM
    rolled = pltpu.roll(kv, SWA_HEAD_DIM, 1)
    pad = jnp.zeros((w, LANES), BF16)
    for scr, val in ((klo, jnp.where(low, kv, 0.0)), (khi, jnp.where(low, 0.0, rolled)),
                     (vlo, jnp.where(low, rolled, 0.0)), (vhi, jnp.where(low, 0.0, kv))):
        scr[:w, :] = pad
        scr[w:, :] = val.astype(BF16)

    rows = pairs * w
    r = lax.broadcasted_iota(jnp.int32, (rows, 2 * w), 0) % w
    c = lax.broadcasted_iota(jnp.int32, (rows, 2 * w), 1)
    band = (c > r) & (c <= r + w)
    pair_of_row = lax.broadcasted_iota(jnp.int32, (rows, 1), 0) // w
    out_low = lax.broadcasted_iota(jnp.int32, (rows, LANES), 1) < SWA_HEAD_DIM

    def sink_column(parity):
        col = jnp.zeros((rows, 1), F32)
        for j in range(pairs):
            col = jnp.where(pair_of_row == j, sinks_ref[h * SWA_GROUP + 2 * j + parity], col)
        return col

    sink_even, sink_odd = sink_column(0), sink_column(1)
    nt = (((1,), (1,)), ((), ()))

    def row0(n):
        return n * w if isinstance(n, int) else pl.multiple_of(n * w, w)

    def scores(n):
        r0 = row0(n)
        qs = jnp.concatenate([q_ref[pl.ds(r0, w), j * LANES:(j + 1) * LANES] for j in range(pairs)], axis=0)
        return (lax.dot_general(qs, klo[pl.ds(r0, 2 * w), :], nt, preferred_element_type=F32),
                lax.dot_general(qs, khi[pl.ds(r0, 2 * w), :], nt, preferred_element_type=F32))

    def finish(n, block_scores, mask):
        r0 = row0(n)

        def head(s, v_scr, sink):
            s = jnp.where(mask, s, NEG)
            m = jnp.maximum(jnp.max(s, axis=-1, keepdims=True), sink)
            p = jnp.exp(s - m)
            l = jnp.sum(p, axis=-1, keepdims=True) + jnp.exp(sink - m)
            o = jnp.dot(p.astype(BF16), v_scr[pl.ds(r0, 2 * w), :], preferred_element_type=F32)
            return o, l

        o_even, l_even = head(block_scores[0], vlo, sink_even)
        o_odd, l_odd = head(block_scores[1], vhi, sink_odd)
        o = (o_even + o_odd) / jnp.where(out_low, l_even, l_odd)
        for j in range(pairs):
            o_ref[pl.ds(r0, w), j * LANES:(j + 1) * LANES] = o[j * w:(j + 1) * w].astype(o_ref.dtype)

    finish(0, scores(0), band & (c >= w))

    def group(g, carry):
        n0 = 1 + g * SWA_BLOCK_GROUP
        pending = scores(n0)
        for u in range(SWA_BLOCK_GROUP):
            current = pending
            if u + 1 < SWA_BLOCK_GROUP:
                pending = scores(n0 + u + 1)
            finish(n0 + u, current, band)
        return carry

    assert (s_len // w - 1) % SWA_BLOCK_GROUP == 0
    lax.fori_loop(0, (s_len // w - 1) // SWA_BLOCK_GROUP, group, 0)


def _swa_attention(z_q, z_kv, sinks, batch, seq):
    t = batch * seq
    gw = SWA_GROUP * SWA_HEAD_DIM
    return pl.pallas_call(
        _swa_kernel,
        grid_spec=pltpu.PrefetchScalarGridSpec(
            num_scalar_prefetch=1,
            grid=(batch, SWA_KV_HEADS),
            in_specs=[pl.BlockSpec((seq, gw), lambda b, h, s: (b, h)),
                      pl.BlockSpec((seq, LANES), lambda b, h, s: (b, h))],
            out_specs=pl.BlockSpec((seq, gw), lambda b, h, s: (b, h)),
            scratch_shapes=[pltpu.VMEM((seq + SWA_WINDOW, LANES), BF16)] * 4),
        out_shape=jax.ShapeDtypeStruct((t, SWA_HEADS * SWA_HEAD_DIM), BF16),
        compiler_params=_cparams(("parallel", "arbitrary")),
        name="swa_attention",
    )(sinks.astype(F32), z_q, z_kv)


def _out_proj_kernel(x_ref, a_ref, s_ref, wt_ref, wb_ref, h_ref):
    h_ref[...] = (x_ref[...] + jnp.dot(a_ref[...], wt_ref[...].astype(BF16), preferred_element_type=F32)
                  + jnp.dot(s_ref[...], wb_ref[...].astype(BF16), preferred_element_type=F32))


def _out_proj(x2, mla_o, swa_o, w_out, *, tm=1024, tn=512):
    t, d = x2.shape
    half = mla_o.shape[1]
    return pl.pallas_call(
        _out_proj_kernel,
        grid=(t // tm, d // tn),
        in_specs=[pl.BlockSpec((tm, tn), lambda i, j: (i, j)),
                  pl.BlockSpec((tm, half), lambda i, j: (i, 0)),
                  pl.BlockSpec((tm, half), lambda i, j: (i, 0)),
                  pl.BlockSpec((half, tn), lambda i, j: (0, j)),
                  pl.BlockSpec((half, tn), lambda i, j: (1, j))],
        out_specs=pl.BlockSpec((tm, tn), lambda i, j: (i, j)),
        out_shape=jax.ShapeDtypeStruct((t, d), F32),
        compiler_params=_cparams(("parallel", "parallel")),
        name="out_proj",
    )(x2, mla_o, swa_o, w_out, w_out)


ROW_CHUNKS = D_MODEL // (2 * LANES)
ROW_PITCH = 24
U32 = jnp.uint32
HIGH_HALF = 0xFFFF0000


def _pack_pair(lo, hi):
    lo_bits = lax.bitcast_convert_type(lo.astype(BF16).astype(F32), U32)
    hi_bits = lax.bitcast_convert_type(hi.astype(BF16).astype(F32), U32)
    return (lo_bits >> 16) | (hi_bits & U32(HIGH_HALF))


def _unpack_pair(words):
    return (lax.bitcast_convert_type(words << 16, F32),
            lax.bitcast_convert_type(words & U32(HIGH_HALF), F32))


def _chunk_cols(c):
    return (slice(2 * c * LANES, (2 * c + 1) * LANES), slice((2 * c + 1) * LANES, (2 * c + 2) * LANES))


def _split_bf16(v):
    hi = v.astype(BF16)
    return hi, (v - hi.astype(F32)).astype(BF16)


def _ffn_norm_router_kernel(h_ref, g_ref, wr_ref, br_ref, o_ref, lg_ref):
    tm = h_ref.shape[0]
    h = h_ref[...]
    inv = lax.rsqrt(jnp.mean(h * h, axis=-1, keepdims=True) + EPS)
    hg_hi, hg_lo = _split_bf16(h * g_ref[...])
    wr_hi, wr_lo = _split_bf16(wr_ref[...])
    acc = (jnp.dot(hg_hi, wr_hi, preferred_element_type=F32)
           + (jnp.dot(hg_hi, wr_lo, preferred_element_type=F32)
              + jnp.dot(hg_lo, wr_hi, preferred_element_type=F32)))
    lg_ref[...] = acc * inv + br_ref[...]
    for c in range(ROW_CHUNKS):
        lo, hi = _chunk_cols(c)
        o_ref[pl.ds(c, tm, stride=ROW_PITCH), :] = _pack_pair(h_ref[:, lo] * inv * g_ref[:, lo],
                                                              h_ref[:, hi] * inv * g_ref[:, hi])
    for c in range(ROW_CHUNKS, ROW_PITCH):
        o_ref[pl.ds(c, tm, stride=ROW_PITCH), :] = jnp.zeros((tm, LANES), o_ref.dtype)


def _ffn_norm_router(h, g, w_r, b_r, *, tm=256):
    t, d = h.shape
    return pl.pallas_call(
        _ffn_norm_router_kernel,
        grid=(t // tm,),
        in_specs=[pl.BlockSpec((tm, d), lambda i: (i, 0)),
                  pl.BlockSpec((1, d), lambda i: (0, 0)),
                  pl.BlockSpec((d, LANES), lambda i: (0, 0)),
                  pl.BlockSpec((1, LANES), lambda i: (0, 0))],
        out_specs=[pl.BlockSpec((tm * ROW_PITCH, LANES), lambda i: (i, 0)),
                   pl.BlockSpec((tm, LANES), lambda i: (i, 0))],
        out_shape=[jax.ShapeDtypeStruct((t * ROW_PITCH, LANES), U32),
                   jax.ShapeDtypeStruct((t, LANES), F32)],
        compiler_params=_cparams(("parallel",)),
        name="ffn_norm_router",
    )(h, g.reshape(1, d), w_r, b_r)


MOE_UNPACK_CHUNKS = 4
MOE_UP_STEPS = D_EXPERT // MOE_CHUNK
MOE_DOWN_STEPS = 2
MOE_STEPS = MOE_UP_STEPS + MOE_DOWN_STEPS
MOE_DOWN_TILE = D_MODEL // MOE_DOWN_STEPS


LANES_LOG2 = LANES.bit_length() - 1
IDX_TILE_ROWS = 8
IDX_TILE = IDX_TILE_ROWS * LANES
IDX_TILE_LOG2 = IDX_TILE.bit_length() - 1
IDX_WINDOW_ROWS = 2 * IDX_TILE_ROWS
assert MOE_ROWS <= IDX_TILE
ROW_UNROLL_LOG2 = 3
ROW_UNROLL = 1 << ROW_UNROLL_LOG2


def _moe_kernel(item_e, item_cnt, item_row0, dstk_ref, hn_ref, wg_ref, wu_ref, wd_ref, pairs_ref,
                idx_win, x_stage, x_bf, hid, out_stage, sem_idx, sem_in, sem_out, *, n_tokens):
    del item_e
    w = pl.program_id(0)
    s = pl.program_id(1)
    n_items = pl.num_programs(0)
    cnt = item_cnt[w]
    nxt_item = jnp.minimum(w + 1, n_items - 1)
    has_next = (w + 1 < n_items) & (item_cnt[nxt_item] > 0)
    cur = w & 1
    nxt = 1 - cur

    def idx_copy(item, buf):
        first = (item_row0[item] >> IDX_TILE_LOG2) * IDX_TILE_ROWS
        return pltpu.make_async_copy(dstk_ref.at[pl.ds(pl.multiple_of(first, IDX_TILE_ROWS), IDX_WINDOW_ROWS)],
                                     idx_win.at[buf], sem_idx)

    def unrolled_rows(n, one_row):
        groups = n >> ROW_UNROLL_LOG2

        def group(g, carry):
            for u in range(ROW_UNROLL):
                one_row(g * ROW_UNROLL + u)
            return carry

        def single(r, carry):
            one_row(r)
            return carry

        lax.fori_loop(0, groups, group, 0)
        lax.fori_loop(groups * ROW_UNROLL, n, single, 0)

    def for_rows(item, buf, fn):
        base = item_row0[item] & (IDX_TILE - 1)

        def one_row(r):
            off = base + r
            fn(idx_win[buf, off >> LANES_LOG2, off & (LANES - 1)], r)

        unrolled_rows(item_cnt[item], one_row)

    def row_in(tok, r):
        return pltpu.make_async_copy(hn_ref.at[pl.ds(pl.multiple_of(tok * ROW_PITCH, 8), ROW_PITCH)],
                                     x_stage.at[pl.ds(pl.multiple_of(r * ROW_PITCH, 8), ROW_PITCH)], sem_in)

    def row_out(slot, r):
        return pltpu.make_async_copy(out_stage.at[pl.ds(pl.multiple_of(r * ROW_PITCH, 8), ROW_PITCH)],
                                     pairs_ref.at[pl.ds(pl.multiple_of(slot * ROW_PITCH, 8), ROW_PITCH)], sem_out)

    def gather_start(item, buf):
        for_rows(item, buf, lambda v, r: row_in(v & (n_tokens - 1), r).start())

    def scatter_start(item, buf):
        for_rows(item, buf, lambda v, r: row_out(v, r).start())

    def wait_rows(n, copy):
        unrolled_rows(n, lambda r: copy(0, r).wait())

    @pl.when((w == 0) & (s == 0))
    def _():
        x_stage[...] = jnp.zeros(x_stage.shape, x_stage.dtype)
        out_stage[...] = jnp.zeros(out_stage.shape, out_stage.dtype)

        @pl.when(cnt > 0)
        def _():
            first_window = idx_copy(0, 0)
            first_window.start()
            first_window.wait()
            gather_start(0, 0)

    @pl.when((s == 0) & (cnt > 0))
    def _():
        wait_rows(cnt, row_in)

    @pl.when((s == 0) & has_next)
    def _():
        idx_copy(nxt_item, nxt).start()

    @pl.when((s == MOE_UP_STEPS) & (w > 0) & (cnt > 0))
    def _():
        wait_rows(item_cnt[jnp.maximum(w - 1, 0)], row_out)

    def compute(m):
        @pl.when(s == 0)
        def _():
            g = jnp.zeros((m, MOE_CHUNK), F32)
            u = jnp.zeros((m, MOE_CHUNK), F32)
            for piece in range(ROW_CHUNKS // MOE_UNPACK_CHUNKS):
                tiles = []
                for c in range(piece * MOE_UNPACK_CHUNKS, (piece + 1) * MOE_UNPACK_CHUNKS):
                    lo, hi = _chunk_cols(c)
                    x_lo, x_hi = _unpack_pair(x_stage[pl.ds(c, m, stride=ROW_PITCH), :])
                    tiles += [x_lo.astype(BF16), x_hi.astype(BF16)]
                    x_bf[:m, lo] = tiles[-2]
                    x_bf[:m, hi] = tiles[-1]
                x_piece = jnp.concatenate(tiles, axis=1)
                k0 = piece * MOE_UNPACK_CHUNKS * 2 * LANES
                k1 = k0 + MOE_UNPACK_CHUNKS * 2 * LANES
                g = g + jnp.dot(x_piece, wg_ref[k0:k1, :].astype(BF16), preferred_element_type=F32)
                u = u + jnp.dot(x_piece, wu_ref[k0:k1, :].astype(BF16), preferred_element_type=F32)
            hid[0, :m, :] = ((g * jax.nn.sigmoid(g)) * u).astype(BF16)

        @pl.when((s > 0) & (s < MOE_UP_STEPS))
        def _():
            x = x_bf[:m, :]
            g = jnp.dot(x, wg_ref[...].astype(BF16), preferred_element_type=F32)
            u = jnp.dot(x, wu_ref[...].astype(BF16), preferred_element_type=F32)
            hid[s, :m, :] = ((g * jax.nn.sigmoid(g)) * u).astype(BF16)

        @pl.when(s >= MOE_UP_STEPS)
        def _():
            hidden = jnp.concatenate([hid[c, :m, :] for c in range(MOE_UP_STEPS)], axis=1)
            y = jnp.dot(hidden, wd_ref[...].astype(BF16), preferred_element_type=F32)
            tile_chunks = MOE_DOWN_TILE // (2 * LANES)
            base = (s - MOE_UP_STEPS) * tile_chunks
            for c in range(tile_chunks):
                lo, hi = _chunk_cols(c)
                out_stage[pl.ds(base + c, m, stride=ROW_PITCH), :] = _pack_pair(y[:, lo], y[:, hi])

    for below, m in zip((0,) + MOE_ROW_VARIANTS[:-1], MOE_ROW_VARIANTS):
        pl.when((cnt > below) & (cnt <= m))(functools.partial(compute, m))

    @pl.when((s == 1) & has_next)
    def _():
        idx_copy(nxt_item, nxt).wait()
        gather_start(nxt_item, nxt)

    @pl.when((s == MOE_STEPS - 1) & (cnt > 0))
    def _():
        scatter_start(w, cur)

        @pl.when(jnp.logical_not(has_next))
        def _():
            wait_rows(cnt, row_out)


def _moe_experts(hn_lin, w_gate, w_up, w_down, item_e, item_cnt, item_row0, dstk, n_tokens):
    n_items = item_e.shape[0]
    d = D_MODEL
    assert n_tokens & (n_tokens - 1) == 0, "token ids are unpacked from k * n_tokens + token with a mask"

    def up_chunk(s, cnt_ref, w):
        return jnp.where(cnt_ref[w] > 0, jnp.minimum(s, MOE_UP_STEPS - 1), MOE_UP_STEPS - 1)

    def down_block(w, s, e, cn):
        own = (cn[w] > 0) & (s >= MOE_UP_STEPS)
        return (jnp.where(own, e[w], e[jnp.maximum(w - 1, 0)]), 0,
                jnp.where(own, s - MOE_UP_STEPS, MOE_DOWN_STEPS - 1))

    return pl.pallas_call(
        functools.partial(_moe_kernel, n_tokens=n_tokens),
        grid_spec=pltpu.PrefetchScalarGridSpec(
            num_scalar_prefetch=3,
            grid=(n_items, MOE_STEPS),
            in_specs=[pl.BlockSpec(memory_space=pl.ANY),
                      pl.BlockSpec(memory_space=pl.ANY),
                      pl.BlockSpec((None, d, MOE_CHUNK), lambda w, s, e, cn, r0: (e[w], 0, up_chunk(s, cn, w))),
                      pl.BlockSpec((None, d, MOE_CHUNK), lambda w, s, e, cn, r0: (e[w], 0, up_chunk(s, cn, w))),
                      pl.BlockSpec((None, D_EXPERT, MOE_DOWN_TILE), lambda w, s, e, cn, r0: down_block(w, s, e, cn))],
            out_specs=pl.BlockSpec(memory_space=pl.ANY),
            scratch_shapes=[pltpu.SMEM((2, IDX_WINDOW_ROWS, LANES), jnp.int32),
                            pltpu.VMEM((MOE_ROWS * ROW_PITCH, LANES), U32),
                            pltpu.VMEM((MOE_ROWS, d), BF16),
                            pltpu.VMEM((MOE_UP_STEPS, MOE_ROWS, MOE_CHUNK), BF16),
                            pltpu.VMEM((MOE_ROWS * ROW_PITCH, LANES), U32),
                            pltpu.SemaphoreType.DMA(()),
                            pltpu.SemaphoreType.DMA(()),
                            pltpu.SemaphoreType.DMA(())]),
        out_shape=jax.ShapeDtypeStruct((TOP_K * n_tokens * ROW_PITCH, LANES), U32),
        compiler_params=_cparams(("arbitrary", "arbitrary")),
        name="moe_experts",
    )(item_e, item_cnt, item_row0, dstk, hn_lin, w_gate, w_up, w_down)


def _combine_kernel(h_ref, p0_ref, p1_ref, gate_ref, g_ref, o_ref):
    tm, d = h_ref.shape
    g0 = gate_ref[:, 0:1]
    g1 = gate_ref[:, 1:2]
    ss = jnp.zeros((tm, 1), F32)
    for c in range(ROW_CHUNKS):
        a = _unpack_pair(p0_ref[pl.ds(c, tm, stride=ROW_PITCH), :])
        b = _unpack_pair(p1_ref[pl.ds(c, tm, stride=ROW_PITCH), :])
        for sl, a_half, b_half in zip(_chunk_cols(c), a, b):
            y = h_ref[:, sl] + (a_half * g0 + b_half * g1)
            o_ref[:, sl] = y
            ss = ss + jnp.sum(y * y, axis=-1, keepdims=True)
    inv = lax.rsqrt(ss / d + EPS)
    o_ref[...] = o_ref[...] * inv * g_ref[...]


def _combine(h, pairs, gates, g_final, *, tm=256):
    t, d = h.shape
    k1 = t // tm
    return pl.pallas_call(
        _combine_kernel,
        grid=(t // tm,),
        in_specs=[pl.BlockSpec((tm, d), lambda i: (i, 0)),
                  pl.BlockSpec((tm * ROW_PITCH, LANES), lambda i: (i, 0)),
                  pl.BlockSpec((tm * ROW_PITCH, LANES), lambda i: (k1 + i, 0)),
                  pl.BlockSpec((tm, TOP_K), lambda i: (i, 0)),
                  pl.BlockSpec((1, d), lambda i: (0, 0))],
        out_specs=pl.BlockSpec((tm, d), lambda i: (i, 0)),
        out_shape=jax.ShapeDtypeStruct((t, d), F32),
        compiler_params=_cparams(("parallel",)),
        name="combine_final_norm",
    )(h, pairs, pairs, gates, g_final.reshape(1, d))


def _route(logits, t):
    group_prob = jax.nn.softmax(logits[:, :N_GROUPS], axis=-1)
    g_sel = jnp.argmax(group_prob, axis=-1)
    g_p = jnp.max(group_prob, axis=-1, keepdims=True)
    e_logits = logits[:, N_GROUPS:N_GROUPS + N_EXPERTS]
    in_group = (jnp.arange(N_EXPERTS) // EXPERTS_PER_GROUP)[None, :] == g_sel[:, None]
    e_prob = jax.nn.softmax(jnp.where(in_group, e_logits, -jnp.inf), axis=-1)
    key = jnp.where(in_group, e_prob, -1.0)
    i1 = jnp.argmax(key, axis=-1)
    key2 = jnp.where(jnp.arange(N_EXPERTS)[None, :] == i1[:, None], -2.0, key)
    i2 = jnp.argmax(key2, axis=-1)
    top_p = jnp.stack([jnp.max(key, axis=-1), jnp.max(key2, axis=-1)], axis=-1)
    expert_idx = jnp.stack([i1, i2], axis=-1)
    assert TOP_K == 2
    gates = g_p * top_p / jnp.sum(top_p, axis=-1, keepdims=True)

    tk = t * TOP_K
    n_items_max = tk // MOE_ROWS + N_EXPERTS
    flat_e = expert_idx.reshape(tk).astype(jnp.int32)
    order = jnp.argsort(flat_e).astype(jnp.int32)
    dstk = (order % TOP_K) * t + order // TOP_K
    dstk = jnp.concatenate([dstk, jnp.zeros((2 * IDX_TILE,), jnp.int32)]).reshape(-1, LANES)
    counts = jnp.sum((flat_e[:, None] == jnp.arange(N_EXPERTS, dtype=jnp.int32)[None, :]).astype(jnp.int32), axis=0)
    start = jnp.cumsum(counts) - counts
    items_of = (counts + MOE_ROWS - 1) // MOE_ROWS
    item_end = jnp.cumsum(items_of)
    wi = jnp.arange(n_items_max, dtype=jnp.int32)
    e_w = jnp.minimum(jnp.sum((item_end[None, :] <= wi[:, None]).astype(jnp.int32), axis=1), N_EXPERTS - 1)
    part = wi - (item_end[e_w] - items_of[e_w])
    active = wi < item_end[-1]
    item_cnt = jnp.where(active, jnp.clip(counts[e_w] - part * MOE_ROWS, 0, MOE_ROWS), 0).astype(jnp.int32)
    item_row0 = jnp.where(active, start[e_w] + part * MOE_ROWS, 0).astype(jnp.int32)
    item_e = jnp.where(active, e_w, e_w[jnp.maximum(item_end[-1] - 1, 0)]).astype(jnp.int32)
    return gates.astype(F32), item_e, item_cnt, item_row0, dstk


def _take_pieces(w, pieces, axis):
    parts = []
    for start, width in pieces:
        if start is None:
            shape = (w.shape[0], width) if axis == 1 else (width, w.shape[1])
            parts.append(jnp.zeros(shape, w.dtype))
        else:
            parts.append(w[:, start:start + width] if axis == 1 else w[start:start + width, :])
    return jnp.concatenate(parts, axis=axis)


def _column_layout_kernel(w_ref, *o_refs, groups):
    w = w_ref[...]
    for o_ref, pieces in zip(o_refs, groups):
        o_ref[...] = _take_pieces(w, pieces, 1).astype(o_ref.dtype)


def _column_layout_from_transpose_kernel(wt_ref, *o_refs, groups):
    wt = wt_ref[...]
    for o_ref, pieces in zip(o_refs, groups):
        o_ref[...] = _take_pieces(wt, pieces, 0).T.astype(o_ref.dtype)


def _column_layout(w, layer, groups, *, tk=256, name, read_transposed=False):
    _, d, n = w.shape
    groups = tuple(tuple(g) for g in groups)
    widths = [sum(width for _, width in g) for g in groups]
    if read_transposed:
        body = functools.partial(_column_layout_from_transpose_kernel, groups=groups)
        w = jnp.swapaxes(w, 1, 2)
        in_spec = pl.BlockSpec((None, n, tk), lambda i: (layer, 0, i))
    else:
        body = functools.partial(_column_layout_kernel, groups=groups)
        in_spec = pl.BlockSpec((None, tk, n), lambda i: (layer, i, 0))
    return pl.pallas_call(
        body,
        grid=(d // tk,),
        in_specs=[in_spec],
        out_specs=[pl.BlockSpec((tk, width), lambda i: (i, 0)) for width in widths],
        out_shape=[jax.ShapeDtypeStruct((d, width), BF16) for width in widths],
        compiler_params=_cparams(("parallel",)),
        name=name,
    )(w)


def _prep_in_weight(w_in, layer):
    off_kv = MLA_Q_RANK + MLA_KV_RANK
    off_kpe = off_kv + MLA_ROPE_DIM
    off_sq = off_kpe + SWA_HEADS * SWA_HEAD_DIM
    off_sk = off_sq + SWA_KV_HEADS * SWA_HEAD_DIM
    kv_group = []
    for h in range(SWA_KV_HEADS):
        kv_group += [(off_sq + h * SWA_HEAD_DIM, SWA_HEAD_DIM), (off_sk + h * SWA_HEAD_DIM, SWA_HEAD_DIM)]
    kv_group += [(off_kv, MLA_ROPE_DIM), (None, LANES - MLA_ROPE_DIM)]
    groups = [[(0, off_kv)], [(off_kpe, off_sq - off_kpe)], kv_group]
    return _column_layout(w_in, layer, groups, name="in_weight_layout", read_transposed=True)


def _prep_q_weight(w_q_up, layer):
    pieces = []
    for h in range(MLA_HEADS):
        pieces += [(h * MLA_QK_DIM, MLA_QK_DIM), (None, MLA_HEAD_PAD - MLA_QK_DIM)]
    return _column_layout(w_q_up, layer, [pieces], name="q_weight_layout")[0]


def kernel(x, positions, g_attn_norm, w_in, g_q_norm, w_q_up, g_kv_norm, w_kv_up, sinks, w_out,
           g_ffn_norm, w_router_group, b_router_group, w_router_expert, b_router_expert,
           w_expert_gate, w_expert_up, w_expert_down, g_final_norm):
    batch, seq, d = x.shape
    t = batch * seq
    assert g_attn_norm.shape[0] == 1, "the final rmsnorm is fused into the (single) layer's combine"
    h = x.reshape(t, d)
    cos, sin = _rope_tables(positions)
    for l in range(g_attn_norm.shape[0]):
        w_lat, w_sq, w_skv = _prep_in_weight(w_in, l)
        z_lat, n = _rmsnorm_matmul(h, g_attn_norm[l], w_lat, tm=512, name="attn_norm_in_proj_latents")

        def in_proj(w, kind, scale, name):
            blocks = w.shape[1] // LANES
            return _matmul_rope(n, w, cos, sin, tm=512, kinds=[kind] * blocks, scales=[scale] * blocks, name=name)

        z_sq = in_proj(w_sq, ROPE_FULL, SWA_HEAD_DIM ** -0.5, "in_proj_swa_q")
        z_skv = in_proj(w_skv, ROPE_HALF, 1.0, "in_proj_swa_kv")
        q_kinds = [ROPE_NONE, ROPE_HALF] * MLA_HEADS
        q = _norm_matmul(z_lat, 0, MLA_Q_RANK, g_q_norm[l], _prep_q_weight(w_q_up, l), cos, sin,
                         tm=1024, tn=2048, kinds=q_kinds, scales=[MLA_QK_DIM ** -0.5] * len(q_kinds), name="q_up")
        kv_kinds = [ROPE_NONE] * (w_kv_up.shape[2] // LANES)
        kv = _norm_matmul(z_lat, MLA_Q_RANK // MLA_KV_RANK, MLA_KV_RANK, g_kv_norm[l], w_kv_up[l], cos, sin,
                          tm=1024, tn=2048, kinds=kv_kinds, scales=[1.0] * len(kv_kinds), name="kv_up")
        mla_o = _mla_attention(q, kv, z_skv, batch, seq)
        swa_o = _swa_attention(z_sq, z_skv, sinks[l], batch, seq)
        w_r = jnp.concatenate([w_router_group[l], w_router_expert[l],
                               jnp.zeros((d, LANES - N_GROUPS - N_EXPERTS), F32)], axis=1)
        b_r = jnp.concatenate([b_router_group[l], b_router_expert[l],
                               jnp.zeros((LANES - N_GROUPS - N_EXPERTS,), F32)]).reshape(1, LANES)
        h = _out_proj(h, mla_o, swa_o, w_out[l])
        hn, logits = _ffn_norm_router(h, g_ffn_norm[l], w_r, b_r)
        gates, item_e, item_cnt, item_row0, dstk = _route(logits, t)
        pairs = _moe_experts(hn, w_expert_gate[l], w_expert_up[l], w_expert_down[l],
                             item_e, item_cnt, item_row0, dstk, t)
        out = _combine(h, pairs, gates, g_final_norm)
    return out.reshape(batch, seq, d)
```

```python
import functools

import jax
import jax.numpy as jnp
from jax import lax
from jax.experimental import pallas as pl
from jax.experimental.pallas import tpu as pltpu

F32 = jnp.float32
BF16 = jnp.bfloat16

D_MODEL = 4096
EPS = 1e-6
ROPE_THETA = 10000.0
NEG = -1e30
MLA_HEADS = 16
MLA_Q_RANK = 1024
MLA_KV_RANK = 512
MLA_NOPE_DIM = 128
MLA_ROPE_DIM = 64
MLA_V_DIM = 128
MLA_QK_DIM = MLA_NOPE_DIM + MLA_ROPE_DIM
MLA_HEAD_PAD = 256
SWA_HEADS = 32
SWA_KV_HEADS = 4
SWA_GROUP = SWA_HEADS // SWA_KV_HEADS
SWA_HEAD_DIM = 64
SWA_WINDOW = 128
N_GROUPS = 8
EXPERTS_PER_GROUP = 8
N_EXPERTS = N_GROUPS * EXPERTS_PER_GROUP
TOP_K = 2
D_EXPERT = 1024

LANES = 128
KV_KPE_BLOCK = SWA_KV_HEADS
ROPE_NONE, ROPE_FULL, ROPE_HALF = 0, 1, 2

MOE_ROWS = 512
MOE_ROW_VARIANTS = (256, 320, 384, MOE_ROWS)
MOE_CHUNK = 256
VMEM_LIMIT = 56 * 1024 * 1024


def _cparams(sem, vmem=VMEM_LIMIT):
    return pltpu.CompilerParams(dimension_semantics=sem, vmem_limit_bytes=vmem)


def _swap32(x):
    lane = lax.broadcasted_iota(jnp.int32, x.shape, 1)
    fwd = pltpu.roll(x, LANES - 32, 1)
    bwd = pltpu.roll(x, 32, 1)
    return jnp.where((lane % 64) < 32, fwd, bwd)


def _rope_tile(x, cos, sin, kind):
    if kind == ROPE_NONE:
        return x
    if kind == ROPE_HALF:
        lane = lax.broadcasted_iota(jnp.int32, x.shape, 1)
        cos = jnp.where(lane < 64, cos, 1.0)
        sin = jnp.where(lane < 64, sin, 0.0)
    return x * cos + _swap32(x) * sin


def _rope_table_kernel(pos_ref, invf_ref, sign_ref, cos_ref, sin_ref):
    ang = pos_ref[...].astype(F32) * invf_ref[...]
    cos_ref[...] = jnp.cos(ang)
    sin_ref[...] = jnp.sin(ang) * sign_ref[...]


def _rope_tables(positions):
    t = positions.size
    pos = positions.reshape(t, 1)
    half = SWA_HEAD_DIM // 2
    inv_freq = ROPE_THETA ** (-jnp.arange(0, SWA_HEAD_DIM, 2, dtype=F32) / SWA_HEAD_DIM)
    lane = jnp.arange(LANES)
    invf = inv_freq[lane % half].reshape(1, LANES)
    sign = jnp.where((lane % SWA_HEAD_DIM) < half, -1.0, 1.0).astype(F32).reshape(1, LANES)
    tm = 1024
    return pl.pallas_call(
        _rope_table_kernel,
        grid=(t // tm,),
        in_specs=[pl.BlockSpec((tm, 1), lambda i: (i, 0)),
                  pl.BlockSpec((1, LANES), lambda i: (0, 0)),
                  pl.BlockSpec((1, LANES), lambda i: (0, 0))],
        out_specs=[pl.BlockSpec((tm, LANES), lambda i: (i, 0))] * 2,
        out_shape=[jax.ShapeDtypeStruct((t, LANES), F32)] * 2,
        compiler_params=_cparams(("parallel",)),
        name="rope_tables",
    )(pos, invf, sign)


def _rmsnorm_matmul_kernel(x_ref, g_ref, w_ref, o_ref, n_ref):
    xf = x_ref[...].astype(F32)
    ms = jnp.mean(xf * xf, axis=-1, keepdims=True)
    n = (xf * lax.rsqrt(ms + EPS) * g_ref[...]).astype(n_ref.dtype)
    n_ref[...] = n
    o_ref[...] = jnp.dot(n, w_ref[...], preferred_element_type=F32).astype(o_ref.dtype)


def _rmsnorm_matmul(x, g, w, *, tm, name):
    t, d = x.shape
    n = w.shape[1]
    return pl.pallas_call(
        _rmsnorm_matmul_kernel,
        grid=(t // tm,),
        in_specs=[pl.BlockSpec((tm, d), lambda i: (i, 0)),
                  pl.BlockSpec((1, d), lambda i: (0, 0)),
                  pl.BlockSpec((d, n), lambda i: (0, 0), pipeline_mode=pl.Buffered(1))],
        out_specs=[pl.BlockSpec((tm, n), lambda i: (i, 0)), pl.BlockSpec((tm, d), lambda i: (i, 0))],
        out_shape=[jax.ShapeDtypeStruct((t, n), BF16), jax.ShapeDtypeStruct((t, d), BF16)],
        compiler_params=_cparams(("parallel",)),
        name=name,
    )(x, g.reshape(1, d), w)


def _rope_epilogue(acc, cos_ref, sin_ref, o_ref, j, kinds, scales):
    def epilogue(tile_kinds, tile_scales):
        needs_rope = any(k != ROPE_NONE for k in tile_kinds)
        cos = cos_ref[...] if needs_rope else None
        sin = sin_ref[...] if needs_rope else None
        for c, (kind, scale) in enumerate(zip(tile_kinds, tile_scales)):
            blk = _rope_tile(acc[:, c * LANES:(c + 1) * LANES], cos, sin, kind)
            if scale != 1.0:
                blk = blk * scale
            o_ref[:, c * LANES:(c + 1) * LANES] = blk.astype(o_ref.dtype)

    if all(k == kinds[0] for k in kinds) and all(s == scales[0] for s in scales):
        epilogue(kinds[0], scales[0])
    else:
        for jj in range(len(kinds)):
            pl.when(j == jj)(functools.partial(epilogue, kinds[jj], scales[jj]))


def _per_tile(values, n_j, per):
    return tuple(tuple(values[jj * per:(jj + 1) * per]) for jj in range(n_j))


def _matmul_rope_kernel(x_ref, w_ref, cos_ref, sin_ref, o_ref, *, kinds, scales):
    acc = jnp.dot(x_ref[...], w_ref[...], preferred_element_type=F32)
    _rope_epilogue(acc, cos_ref, sin_ref, o_ref, 0, (kinds,), (scales,))


def _matmul_rope(x, w, cos, sin, *, tm, kinds, scales, name):
    t, k = x.shape
    n = w.shape[1]
    return pl.pallas_call(
        functools.partial(_matmul_rope_kernel, kinds=tuple(kinds), scales=tuple(scales)),
        grid=(t // tm,),
        in_specs=[pl.BlockSpec((tm, k), lambda i: (i, 0)),
                  pl.BlockSpec((k, n), lambda i: (0, 0), pipeline_mode=pl.Buffered(1)),
                  pl.BlockSpec((tm, LANES), lambda i: (i, 0)),
                  pl.BlockSpec((tm, LANES), lambda i: (i, 0))],
        out_specs=pl.BlockSpec((tm, n), lambda i: (i, 0)),
        out_shape=jax.ShapeDtypeStruct((t, n), BF16),
        compiler_params=_cparams(("parallel",)),
        name=name,
    )(x, w, cos, sin)


def _norm_matmul_kernel(x_ref, g_ref, w_ref, cos_ref, sin_ref, o_ref, n_scr, *, kinds, scales):
    j = pl.program_id(1)

    @pl.when(j == 0)
    def _():
        xf = x_ref[...].astype(F32)
        ms = jnp.mean(xf * xf, axis=-1, keepdims=True)
        n_scr[...] = (xf * lax.rsqrt(ms + EPS) * g_ref[...]).astype(BF16)

    acc = jnp.dot(n_scr[...], w_ref[...].astype(BF16), preferred_element_type=F32)
    _rope_epilogue(acc, cos_ref, sin_ref, o_ref, j, kinds, scales)


def _norm_matmul(x, x_col_block, k, g, w, cos, sin, *, tm, tn, kinds, scales, name):
    t = x.shape[0]
    n = w.shape[1]
    n_j = n // tn
    kinds = _per_tile(kinds, n_j, tn // LANES)
    scales = _per_tile(scales, n_j, tn // LANES)
    return pl.pallas_call(
        functools.partial(_norm_matmul_kernel, kinds=kinds, scales=scales),
        grid=(t // tm, n_j),
        in_specs=[pl.BlockSpec((tm, k), lambda i, j: (i, x_col_block)),
                  pl.BlockSpec((1, k), lambda i, j: (0, 0)),
                  pl.BlockSpec((k, tn), lambda i, j: (0, j)),
                  pl.BlockSpec((tm, LANES), lambda i, j: (i, 0)),
                  pl.BlockSpec((tm, LANES), lambda i, j: (i, 0))],
        out_specs=pl.BlockSpec((tm, tn), lambda i, j: (i, j)),
        out_shape=jax.ShapeDtypeStruct((t, n), BF16),
        scratch_shapes=[pltpu.VMEM((tm, k), BF16)],
        compiler_params=_cparams(("parallel", "arbitrary")),
        name=name,
    )(x, g.reshape(1, k), w, cos, sin)


MLA_SCORE_LOOKAHEAD = 2


def _mla_kernel(q_ref, kv_ref, kpe_ref, o_ref, kcat_scr, *, tq):
    s_len = q_ref.shape[0]
    kcat_scr[:, :MLA_NOPE_DIM] = kv_ref[:, :MLA_NOPE_DIM]
    kcat_scr[:, MLA_NOPE_DIM:] = kpe_ref[...]
    row = lax.broadcasted_iota(jnp.int32, (tq, tq), 0)
    col = lax.broadcasted_iota(jnp.int32, (tq, tq), 1)
    tri = col <= row
    nt = (((1,), (1,)), ((), ()))
    n_blocks = s_len // tq

    def scores(qi):
        lo, hi = qi * tq, (qi + 1) * tq
        q = q_ref[lo:hi, :]
        s_diag = lax.dot_general(q, kcat_scr[lo:hi, :], nt, preferred_element_type=F32)
        s_past = lax.dot_general(q, kcat_scr[:lo, :], nt, preferred_element_type=F32) if qi > 0 else None
        return s_diag, s_past

    queue = [scores(qi) for qi in range(min(MLA_SCORE_LOOKAHEAD, n_blocks))]
    for qi in range(n_blocks):
        lo, hi = qi * tq, (qi + 1) * tq
        s_diag, s_past = queue.pop(0)
        if qi + MLA_SCORE_LOOKAHEAD < n_blocks:
            queue.append(scores(qi + MLA_SCORE_LOOKAHEAD))
        s_diag = jnp.where(tri, s_diag, NEG)
        m = jnp.max(s_diag, axis=-1, keepdims=True)
        if qi > 0:
            m = jnp.maximum(m, jnp.max(s_past, axis=-1, keepdims=True))
            p_past = jnp.exp(s_past - m)
        p_diag = jnp.exp(s_diag - m)
        l = jnp.sum(p_diag, axis=-1, keepdims=True)
        o = jnp.dot(p_diag.astype(BF16), kv_ref[lo:hi, MLA_NOPE_DIM:], preferred_element_type=F32)
        if qi > 0:
            l = l + jnp.sum(p_past, axis=-1, keepdims=True)
            o = o + jnp.dot(p_past.astype(BF16), kv_ref[:lo, MLA_NOPE_DIM:], preferred_element_type=F32)
        o_ref[lo:hi, :] = (o / l).astype(o_ref.dtype)


def _mla_attention(q, kv, z_kv, batch, seq):
    t = batch * seq
    kpe_block = KV_KPE_BLOCK
    return pl.pallas_call(
        functools.partial(_mla_kernel, tq=256),
        grid=(batch, MLA_HEADS),
        in_specs=[pl.BlockSpec((seq, MLA_HEAD_PAD), lambda b, h: (b, h)),
                  pl.BlockSpec((seq, MLA_NOPE_DIM + MLA_V_DIM), lambda b, h: (b, h)),
                  pl.BlockSpec((seq, LANES), lambda b, h: (b, kpe_block))],
        out_specs=pl.BlockSpec((seq, MLA_V_DIM), lambda b, h: (b, h)),
        out_shape=jax.ShapeDtypeStruct((t, MLA_HEADS * MLA_V_DIM), BF16),
        scratch_shapes=[pltpu.VMEM((seq, MLA_HEAD_PAD), BF16)],
        compiler_params=_cparams(("parallel", "arbitrary")),
        name="mla_attention",
    )(q, kv, z_kv)


SWA_BLOCK_GROUP = 5


def _swa_kernel(sinks_ref, q_ref, kv_ref, o_ref, klo, khi, vlo, vhi):
    h = pl.program_id(1)
    s_len = q_ref.shape[0]
    w = SWA_WINDOW
    pairs = SWA_GROUP // 2
    kv = kv_ref[...].astype(F32)
    lane = lax.broadcasted_iota(jnp.int32, kv.shape, 1)
    low = lane < SWA_HEAD_DIM
    rolled = pltpu.roll(kv, SWA_HEAD_DIM, 1)
    pad = jnp.zeros((w, LANES), BF16)
    for scr, val in ((klo, jnp.where(low, kv, 0.0)), (khi, jnp.where(low, 0.0, rolled)),
                     (vlo, jnp.where(low, rolled, 0.0)), (vhi, jnp.where(low, 0.0, kv))):
        scr[:w, :] = pad
        scr[w:, :] = val.astype(BF16)

    rows = pairs * w
    r = lax.broadcasted_iota(jnp.int32, (rows, 2 * w), 0) % w
    c = lax.broadcasted_iota(jnp.int32, (rows, 2 * w), 1)
    band = (c > r) & (c <= r + w)
    pair_of_row = lax.broadcasted_iota(jnp.int32, (rows, 1), 0) // w
    out_low = lax.broadcasted_iota(jnp.int32, (rows, LANES), 1) < SWA_HEAD_DIM

    def sink_column(parity):
        col = jnp.zeros((rows, 1), F32)
        for j in range(pairs):
            col = jnp.where(pair_of_row == j, sinks_ref[h * SWA_GROUP + 2 * j + parity], col)
        return col

    sink_even, sink_odd = sink_column(0), sink_column(1)
    nt = (((1,), (1,)), ((), ()))

    def row0(n):
        return n * w if isinstance(n, int) else pl.multiple_of(n * w, w)

    def scores(n):
        r0 = row0(n)
        qs = jnp.concatenate([q_ref[pl.ds(r0, w), j * LANES:(j + 1) * LANES] for j in range(pairs)], axis=0)
        return (lax.dot_general(qs, klo[pl.ds(r0, 2 * w), :], nt, preferred_element_type=F32),
                lax.dot_general(qs, khi[pl.ds(r0, 2 * w), :], nt, preferred_element_type=F32))

    def finish(n, block_scores, mask):
        r0 = row0(n)

        def head(s, v_scr, sink):
            s = jnp.where(mask, s, NEG)
            m = jnp.maximum(jnp.max(s, axis=-1, keepdims=True), sink)
            p = jnp.exp(s - m)
            l = jnp.sum(p, axis=-1, keepdims=True) + jnp.exp(sink - m)
            o = jnp.dot(p.astype(BF16), v_scr[pl.ds(r0, 2 * w), :], preferred_element_type=F32)
            return o, l

        o_even, l_even = head(block_scores[0], vlo, sink_even)
        o_odd, l_odd = head(block_scores[1], vhi, sink_odd)
        o = (o_even + o_odd) / jnp.where(out_low, l_even, l_odd)
        for j in range(pairs):
            o_ref[pl.ds(r0, w), j * LANES:(j + 1) * LANES] = o[j * w:(j + 1) * w].astype(o_ref.dtype)

    finish(0, scores(0), band & (c >= w))

    def group(g, carry):
        n0 = 1 + g * SWA_BLOCK_GROUP
        pending = scores(n0)
        for u in range(SWA_BLOCK_GROUP):
            current = pending
            if u + 1 < SWA_BLOCK_GROUP:
                pending = scores(n0 + u + 1)
            finish(n0 + u, current, band)
        return carry

    assert (s_len // w - 1) % SWA_BLOCK_GROUP == 0
    lax.fori_loop(0, (s_len // w - 1) // SWA_BLOCK_GROUP, group, 0)


def _swa_attention(z_q, z_kv, sinks, batch, seq):
    t = batch * seq
    gw = SWA_GROUP * SWA_HEAD_DIM
    return pl.pallas_call(
        _swa_kernel,
        grid_spec=pltpu.PrefetchScalarGridSpec(
            num_scalar_prefetch=1,
            grid=(batch, SWA_KV_HEADS),
            in_specs=[pl.BlockSpec((seq, gw), lambda b, h, s: (b, h)),
                      pl.BlockSpec((seq, LANES), lambda b, h, s: (b, h))],
            out_specs=pl.BlockSpec((seq, gw), lambda b, h, s: (b, h)),
            scratch_shapes=[pltpu.VMEM((seq + SWA_WINDOW, LANES), BF16)] * 4),
        out_shape=jax.ShapeDtypeStruct((t, SWA_HEADS * SWA_HEAD_DIM), BF16),
        compiler_params=_cparams(("parallel", "arbitrary")),
        name="swa_attention",
    )(sinks.astype(F32), z_q, z_kv)


def _out_proj_kernel(x_ref, a_ref, s_ref, wt_ref, wb_ref, h_ref):
    h_ref[...] = (x_ref[...] + jnp.dot(a_ref[...], wt_ref[...].astype(BF16), preferred_element_type=F32)
                  + jnp.dot(s_ref[...], wb_ref[...].astype(BF16), preferred_element_type=F32))


def _out_proj(x2, mla_o, swa_o, w_out, *, tm=1024, tn=512):
    t, d = x2.shape
    half = mla_o.shape[1]
    return pl.pallas_call(
        _out_proj_kernel,
        grid=(t // tm, d // tn),
        in_specs=[pl.BlockSpec((tm, tn), lambda i, j: (i, j)),
                  pl.BlockSpec((tm, half), lambda i, j: (i, 0)),
                  pl.BlockSpec((tm, half), lambda i, j: (i, 0)),
                  pl.BlockSpec((half, tn), lambda i, j: (0, j)),
                  pl.BlockSpec((half, tn), lambda i, j: (1, j))],
        out_specs=pl.BlockSpec((tm, tn), lambda i, j: (i, j)),
        out_shape=jax.ShapeDtypeStruct((t, d), F32),
        compiler_params=_cparams(("parallel", "parallel")),
        name="out_proj",
    )(x2, mla_o, swa_o, w_out, w_out)


ROW_CHUNKS = D_MODEL // (2 * LANES)
ROW_PITCH = 24
U32 = jnp.uint32
HIGH_HALF = 0xFFFF0000


def _pack_pair(lo, hi):
    lo_bits = lax.bitcast_convert_type(lo.astype(BF16).astype(F32), U32)
    hi_bits = lax.bitcast_convert_type(hi.astype(BF16).astype(F32), U32)
    return (lo_bits >> 16) | (hi_bits & U32(HIGH_HALF))


def _unpack_pair(words):
    return (lax.bitcast_convert_type(words << 16, F32),
            lax.bitcast_convert_type(words & U32(HIGH_HALF), F32))


def _chunk_cols(c):
    return (slice(2 * c * LANES, (2 * c + 1) * LANES), slice((2 * c + 1) * LANES, (2 * c + 2) * LANES))


def _split_bf16(v):
    hi = v.astype(BF16)
    return hi, (v - hi.astype(F32)).astype(BF16)


def _ffn_norm_router_kernel(h_ref, g_ref, wr_ref, br_ref, o_ref, lg_ref):
    tm = h_ref.shape[0]
    h = h_ref[...]
    inv = lax.rsqrt(jnp.mean(h * h, axis=-1, keepdims=True) + EPS)
    hg_hi, hg_lo = _split_bf16(h * g_ref[...])
    wr_hi, wr_lo = _split_bf16(wr_ref[...])
    acc = (jnp.dot(hg_hi, wr_hi, preferred_element_type=F32)
           + (jnp.dot(hg_hi, wr_lo, preferred_element_type=F32)
              + jnp.dot(hg_lo, wr_hi, preferred_element_type=F32)))
    lg_ref[...] = acc * inv + br_ref[...]
    for c in range(ROW_CHUNKS):
        lo, hi = _chunk_cols(c)
        o_ref[pl.ds(c, tm, stride=ROW_PITCH), :] = _pack_pair(h_ref[:, lo] * inv * g_ref[:, lo],
                                                              h_ref[:, hi] * inv * g_ref[:, hi])
    for c in range(ROW_CHUNKS, ROW_PITCH):
        o_ref[pl.ds(c, tm, stride=ROW_PITCH), :] = jnp.zeros((tm, LANES), o_ref.dtype)


def _ffn_norm_router(h, g, w_r, b_r, *, tm=256):
    t, d = h.shape
    return pl.pallas_call(
        _ffn_norm_router_kernel,
        grid=(t // tm,),
        in_specs=[pl.BlockSpec((tm, d), lambda i: (i, 0)),
                  pl.BlockSpec((1, d), lambda i: (0, 0)),
                  pl.BlockSpec((d, LANES), lambda i: (0, 0)),
                  pl.BlockSpec((1, LANES), lambda i: (0, 0))],
        out_specs=[pl.BlockSpec((tm * ROW_PITCH, LANES), lambda i: (i, 0)),
                   pl.BlockSpec((tm, LANES), lambda i: (i, 0))],
        out_shape=[jax.ShapeDtypeStruct((t * ROW_PITCH, LANES), U32),
                   jax.ShapeDtypeStruct((t, LANES), F32)],
        compiler_params=_cparams(("parallel",)),
        name="ffn_norm_router",
    )(h, g.reshape(1, d), w_r, b_r)


MOE_UNPACK_CHUNKS = 4
MOE_UP_STEPS = D_EXPERT // MOE_CHUNK
MOE_DOWN_STEPS = 2
MOE_STEPS = MOE_UP_STEPS + MOE_DOWN_STEPS
MOE_DOWN_TILE = D_MODEL // MOE_DOWN_STEPS


LANES_LOG2 = LANES.bit_length() - 1
IDX_TILE_ROWS = 8
IDX_TILE = IDX_TILE_ROWS * LANES
IDX_TILE_LOG2 = IDX_TILE.bit_length() - 1
IDX_WINDOW_ROWS = 2 * IDX_TILE_ROWS
assert MOE_ROWS <= IDX_TILE
ROW_UNROLL_LOG2 = 3
ROW_UNROLL = 1 << ROW_UNROLL_LOG2


def _moe_kernel(item_e, item_cnt, item_row0, dstk_ref, hn_ref, wg_ref, wu_ref, wd_ref, pairs_ref,
                idx_win, x_stage, x_bf, hid, out_stage, sem_idx, sem_in, sem_out, *, n_tokens):
    del item_e
    w = pl.program_id(0)
    s = pl.program_id(1)
    n_items = pl.num_programs(0)
    cnt = item_cnt[w]
    nxt_item = jnp.minimum(w + 1, n_items - 1)
    has_next = (w + 1 < n_items) & (item_cnt[nxt_item] > 0)
    cur = w & 1
    nxt = 1 - cur

    def idx_copy(item, buf):
        first = (item_row0[item] >> IDX_TILE_LOG2) * IDX_TILE_ROWS
        return pltpu.make_async_copy(dstk_ref.at[pl.ds(pl.multiple_of(first, IDX_TILE_ROWS), IDX_WINDOW_ROWS)],
                                     idx_win.at[buf], sem_idx)

    def unrolled_rows(n, one_row):
        groups = n >> ROW_UNROLL_LOG2

        def group(g, carry):
            for u in range(ROW_UNROLL):
                one_row(g * ROW_UNROLL + u)
            return carry

        def single(r, carry):
            one_row(r)
            return carry

        lax.fori_loop(0, groups, group, 0)
        lax.fori_loop(groups * ROW_UNROLL, n, single, 0)

    def for_rows(item, buf, fn):
        base = item_row0[item] & (IDX_TILE - 1)

        def one_row(r):
            off = base + r
            fn(idx_win[buf, off >> LANES_LOG2, off & (LANES - 1)], r)

        unrolled_rows(item_cnt[item], one_row)

    def row_in(tok, r):
        return pltpu.make_async_copy(hn_ref.at[pl.ds(pl.multiple_of(tok * ROW_PITCH, 8), ROW_PITCH)],
                                     x_stage.at[pl.ds(pl.multiple_of(r * ROW_PITCH, 8), ROW_PITCH)], sem_in)

    def row_out(slot, r):
        return pltpu.make_async_copy(out_stage.at[pl.ds(pl.multiple_of(r * ROW_PITCH, 8), ROW_PITCH)],
                                     pairs_ref.at[pl.ds(pl.multiple_of(slot * ROW_PITCH, 8), ROW_PITCH)], sem_out)

    def gather_start(item, buf):
        for_rows(item, buf, lambda v, r: row_in(v & (n_tokens - 1), r).start())

    def scatter_start(item, buf):
        for_rows(item, buf, lambda v, r: row_out(v, r).start(priority=1))

    def wait_rows(n, copy):
        unrolled_rows(n, lambda r: copy(0, r).wait())

    @pl.when((w == 0) & (s == 0))
    def _():
        x_stage[...] = jnp.zeros(x_stage.shape, x_stage.dtype)
        out_stage[...] = jnp.zeros(out_stage.shape, out_stage.dtype)

        @pl.when(cnt > 0)
        def _():
            first_window = idx_copy(0, 0)
            first_window.start()
            first_window.wait()
            gather_start(0, 0)

    @pl.when((s == 0) & (cnt > 0))
    def _():
        wait_rows(cnt, row_in)

    @pl.when((s == 0) & has_next)
    def _():
        idx_copy(nxt_item, nxt).start()

    @pl.when((s == MOE_UP_STEPS) & (w > 0) & (cnt > 0))
    def _():
        wait_rows(item_cnt[jnp.maximum(w - 1, 0)], row_out)

    def compute(m):
        @pl.when(s == 0)
        def _():
            g = jnp.zeros((m, MOE_CHUNK), F32)
            u = jnp.zeros((m, MOE_CHUNK), F32)
            for piece in range(ROW_CHUNKS // MOE_UNPACK_CHUNKS):
                tiles = []
                for c in range(piece * MOE_UNPACK_CHUNKS, (piece + 1) * MOE_UNPACK_CHUNKS):
                    lo, hi = _chunk_cols(c)
                    x_lo, x_hi = _unpack_pair(x_stage[pl.ds(c, m, stride=ROW_PITCH), :])
                    tiles += [x_lo.astype(BF16), x_hi.astype(BF16)]
                    x_bf[:m, lo] = tiles[-2]
                    x_bf[:m, hi] = tiles[-1]
                x_piece = jnp.concatenate(tiles, axis=1)
                k0 = piece * MOE_UNPACK_CHUNKS * 2 * LANES
                k1 = k0 + MOE_UNPACK_CHUNKS * 2 * LANES
                g = g + jnp.dot(x_piece, wg_ref[k0:k1, :].astype(BF16), preferred_element_type=F32)
                u = u + jnp.dot(x_piece, wu_ref[k0:k1, :].astype(BF16), preferred_element_type=F32)
            hid[0, :m, :] = ((g * jax.nn.sigmoid(g)) * u).astype(BF16)

        @pl.when((s > 0) & (s < MOE_UP_STEPS))
        def _():
            x = x_bf[:m, :]
            g = jnp.dot(x, wg_ref[...].astype(BF16), preferred_element_type=F32)
            u = jnp.dot(x, wu_ref[...].astype(BF16), preferred_element_type=F32)
            hid[s, :m, :] = ((g * jax.nn.sigmoid(g)) * u).astype(BF16)

        @pl.when(s >= MOE_UP_STEPS)
        def _():
            hidden = jnp.concatenate([hid[c, :m, :] for c in range(MOE_UP_STEPS)], axis=1)
            y = jnp.dot(hidden, wd_ref[...].astype(BF16), preferred_element_type=F32)
            tile_chunks = MOE_DOWN_TILE // (2 * LANES)
            base = (s - MOE_UP_STEPS) * tile_chunks
            for c in range(tile_chunks):
                lo, hi = _chunk_cols(c)
                out_stage[pl.ds(base + c, m, stride=ROW_PITCH), :] = _pack_pair(y[:, lo], y[:, hi])

    for below, m in zip((0,) + MOE_ROW_VARIANTS[:-1], MOE_ROW_VARIANTS):
        pl.when((cnt > below) & (cnt <= m))(functools.partial(compute, m))

    @pl.when((s == 1) & has_next)
    def _():
        idx_copy(nxt_item, nxt).wait()
        gather_start(nxt_item, nxt)

    @pl.when((s == MOE_STEPS - 1) & (cnt > 0))
    def _():
        scatter_start(w, cur)

        @pl.when(jnp.logical_not(has_next))
        def _():
            wait_rows(cnt, row_out)


def _moe_experts(hn_lin, w_gate, w_up, w_down, item_e, item_cnt, item_row0, dstk, n_tokens):
    n_items = item_e.shape[0]
    d = D_MODEL
    assert n_tokens & (n_tokens - 1) == 0, "token ids are unpacked from k * n_tokens + token with a mask"

    def up_chunk(s, cnt_ref, w):
        return jnp.where(cnt_ref[w] > 0, jnp.minimum(s, MOE_UP_STEPS - 1), MOE_UP_STEPS - 1)

    def down_block(w, s, e, cn):
        own = (cn[w] > 0) & (s >= MOE_UP_STEPS)
        return (jnp.where(own, e[w], e[jnp.maximum(w - 1, 0)]), 0,
                jnp.where(own, s - MOE_UP_STEPS, MOE_DOWN_STEPS - 1))

    return pl.pallas_call(
        functools.partial(_moe_kernel, n_tokens=n_tokens),
        grid_spec=pltpu.PrefetchScalarGridSpec(
            num_scalar_prefetch=3,
            grid=(n_items, MOE_STEPS),
            in_specs=[pl.BlockSpec(memory_space=pl.ANY),
                      pl.BlockSpec(memory_space=pl.ANY),
                      pl.BlockSpec((None, d, MOE_CHUNK), lambda w, s, e, cn, r0: (e[w], 0, up_chunk(s, cn, w))),
                      pl.BlockSpec((None, d, MOE_CHUNK), lambda w, s, e, cn, r0: (e[w], 0, up_chunk(s, cn, w))),
                      pl.BlockSpec((None, D_EXPERT, MOE_DOWN_TILE), lambda w, s, e, cn, r0: down_block(w, s, e, cn))],
            out_specs=pl.BlockSpec(memory_space=pl.ANY),
            scratch_shapes=[pltpu.SMEM((2, IDX_WINDOW_ROWS, LANES), jnp.int32),
                            pltpu.VMEM((MOE_ROWS * ROW_PITCH, LANES), U32),
                            pltpu.VMEM((MOE_ROWS, d), BF16),
                            pltpu.VMEM((MOE_UP_STEPS, MOE_ROWS, MOE_CHUNK), BF16),
                            pltpu.VMEM((MOE_ROWS * ROW_PITCH, LANES), U32),
                            pltpu.SemaphoreType.DMA(()),
                            pltpu.SemaphoreType.DMA(()),
                            pltpu.SemaphoreType.DMA(())]),
        out_shape=jax.ShapeDtypeStruct((TOP_K * n_tokens * ROW_PITCH, LANES), U32),
        compiler_params=_cparams(("arbitrary", "arbitrary")),
        name="moe_experts",
    )(item_e, item_cnt, item_row0, dstk, hn_lin, w_gate, w_up, w_down)


def _combine_kernel(h_ref, p0_ref, p1_ref, gate_ref, g_ref, o_ref):
    tm, d = h_ref.shape
    g0 = gate_ref[:, 0:1]
    g1 = gate_ref[:, 1:2]
    ss = jnp.zeros((tm, 1), F32)
    for c in range(ROW_CHUNKS):
        a = _unpack_pair(p0_ref[pl.ds(c, tm, stride=ROW_PITCH), :])
        b = _unpack_pair(p1_ref[pl.ds(c, tm, stride=ROW_PITCH), :])
        for sl, a_half, b_half in zip(_chunk_cols(c), a, b):
            y = h_ref[:, sl] + (a_half * g0 + b_half * g1)
            o_ref[:, sl] = y
            ss = ss + jnp.sum(y * y, axis=-1, keepdims=True)
    inv = lax.rsqrt(ss / d + EPS)
    o_ref[...] = o_ref[...] * inv * g_ref[...]


def _combine(h, pairs, gates, g_final, *, tm=256):
    t, d = h.shape
    k1 = t // tm
    return pl.pallas_call(
        _combine_kernel,
        grid=(t // tm,),
        in_specs=[pl.BlockSpec((tm, d), lambda i: (i, 0)),
                  pl.BlockSpec((tm * ROW_PITCH, LANES), lambda i: (i, 0)),
                  pl.BlockSpec((tm * ROW_PITCH, LANES), lambda i: (k1 + i, 0)),
                  pl.BlockSpec((tm, TOP_K), lambda i: (i, 0)),
                  pl.BlockSpec((1, d), lambda i: (0, 0))],
        out_specs=pl.BlockSpec((tm, d), lambda i: (i, 0)),
        out_shape=jax.ShapeDtypeStruct((t, d), F32),
        compiler_params=_cparams(("parallel",)),
        name="combine_final_norm",
    )(h, pairs, pairs, gates, g_final.reshape(1, d))


def _route(logits, t):
    group_prob = jax.nn.softmax(logits[:, :N_GROUPS], axis=-1)
    g_sel = jnp.argmax(group_prob, axis=-1)
    g_p = jnp.max(group_prob, axis=-1, keepdims=True)
    e_logits = logits[:, N_GROUPS:N_GROUPS + N_EXPERTS]
    in_group = (jnp.arange(N_EXPERTS) // EXPERTS_PER_GROUP)[None, :] == g_sel[:, None]
    e_prob = jax.nn.softmax(jnp.where(in_group, e_logits, -jnp.inf), axis=-1)
    key = jnp.where(in_group, e_prob, -1.0)
    i1 = jnp.argmax(key, axis=-1)
    key2 = jnp.where(jnp.arange(N_EXPERTS)[None, :] == i1[:, None], -2.0, key)
    i2 = jnp.argmax(key2, axis=-1)
    top_p = jnp.stack([jnp.max(key, axis=-1), jnp.max(key2, axis=-1)], axis=-1)
    expert_idx = jnp.stack([i1, i2], axis=-1)
    assert TOP_K == 2
    gates = g_p * top_p / jnp.sum(top_p, axis=-1, keepdims=True)

    tk = t * TOP_K
    n_items_max = tk // MOE_ROWS + N_EXPERTS
    flat_e = expert_idx.reshape(tk).astype(jnp.int32)
    order = jnp.argsort(flat_e).astype(jnp.int32)
    dstk = (order % TOP_K) * t + order // TOP_K
    dstk = jnp.concatenate([dstk, jnp.zeros((2 * IDX_TILE,), jnp.int32)]).reshape(-1, LANES)
    counts = jnp.sum((flat_e[:, None] == jnp.arange(N_EXPERTS, dtype=jnp.int32)[None, :]).astype(jnp.int32), axis=0)
    start = jnp.cumsum(counts) - counts
    items_of = (counts + MOE_ROWS - 1) // MOE_ROWS
    item_end = jnp.cumsum(items_of)
    wi = jnp.arange(n_items_max, dtype=jnp.int32)
    e_w = jnp.minimum(jnp.sum((item_end[None, :] <= wi[:, None]).astype(jnp.int32), axis=1), N_EXPERTS - 1)
    part = wi - (item_end[e_w] - items_of[e_w])
    active = wi < item_end[-1]
    item_cnt = jnp.where(active, jnp.clip(counts[e_w] - part * MOE_ROWS, 0, MOE_ROWS), 0).astype(jnp.int32)
    item_row0 = jnp.where(active, start[e_w] + part * MOE_ROWS, 0).astype(jnp.int32)
    item_e = jnp.where(active, e_w, e_w[jnp.maximum(item_end[-1] - 1, 0)]).astype(jnp.int32)
    return gates.astype(F32), item_e, item_cnt, item_row0, dstk


def _take_pieces(w, pieces, axis):
    parts = []
    for start, width in pieces:
        if start is None:
            shape = (w.shape[0], width) if axis == 1 else (width, w.shape[1])
            parts.append(jnp.zeros(shape, w.dtype))
        else:
            parts.append(w[:, start:start + width] if axis == 1 else w[start:start + width, :])
    return jnp.concatenate(parts, axis=axis)


def _column_layout_kernel(w_ref, *o_refs, groups):
    w = w_ref[...]
    for o_ref, pieces in zip(o_refs, groups):
        o_ref[...] = _take_pieces(w, pieces, 1).astype(o_ref.dtype)


def _column_layout_from_transpose_kernel(wt_ref, *o_refs, groups):
    wt = wt_ref[...]
    for o_ref, pieces in zip(o_refs, groups):
        o_ref[...] = _take_pieces(wt, pieces, 0).T.astype(o_ref.dtype)


def _column_layout(w, layer, groups, *, tk=256, name, read_transposed=False):
    _, d, n = w.shape
    groups = tuple(tuple(g) for g in groups)
    widths = [sum(width for _, width in g) for g in groups]
    if read_transposed:
        body = functools.partial(_column_layout_from_transpose_kernel, groups=groups)
        w = jnp.swapaxes(w, 1, 2)
        in_spec = pl.BlockSpec((None, n, tk), lambda i: (layer, 0, i))
    else:
        body = functools.partial(_column_layout_kernel, groups=groups)
        in_spec = pl.BlockSpec((None, tk, n), lambda i: (layer, i, 0))
    return pl.pallas_call(
        body,
        grid=(d // tk,),
        in_specs=[in_spec],
        out_specs=[pl.BlockSpec((tk, width), lambda i: (i, 0)) for width in widths],
        out_shape=[jax.ShapeDtypeStruct((d, width), BF16) for width in widths],
        compiler_params=_cparams(("parallel",)),
        name=name,
    )(w)


def _prep_in_weight(w_in, layer):
    off_kv = MLA_Q_RANK + MLA_KV_RANK
    off_kpe = off_kv + MLA_ROPE_DIM
    off_sq = off_kpe + SWA_HEADS * SWA_HEAD_DIM
    off_sk = off_sq + SWA_KV_HEADS * SWA_HEAD_DIM
    kv_group = []
    for h in range(SWA_KV_HEADS):
        kv_group += [(off_sq + h * SWA_HEAD_DIM, SWA_HEAD_DIM), (off_sk + h * SWA_HEAD_DIM, SWA_HEAD_DIM)]
    kv_group += [(off_kv, MLA_ROPE_DIM), (None, LANES - MLA_ROPE_DIM)]
    groups = [[(0, off_kv)], [(off_kpe, off_sq - off_kpe)], kv_group]
    return _column_layout(w_in, layer, groups, name="in_weight_layout", read_transposed=True)


def _prep_q_weight(w_q_up, layer):
    pieces = []
    for h in range(MLA_HEADS):
        pieces += [(h * MLA_QK_DIM, MLA_QK_DIM), (None, MLA_HEAD_PAD - MLA_QK_DIM)]
    return _column_layout(w_q_up, layer, [pieces], name="q_weight_layout")[0]


def kernel(x, positions, g_attn_norm, w_in, g_q_norm, w_q_up, g_kv_norm, w_kv_up, sinks, w_out,
           g_ffn_norm, w_router_group, b_router_group, w_router_expert, b_router_expert,
           w_expert_gate, w_expert_up, w_expert_down, g_final_norm):
    batch, seq, d = x.shape
    t = batch * seq
    assert g_attn_norm.shape[0] == 1, "the final rmsnorm is fused into the (single) layer's combine"
    h = x.reshape(t, d)
    cos, sin = _rope_tables(positions)
    for l in range(g_attn_norm.shape[0]):
        w_lat, w_sq, w_skv = _prep_in_weight(w_in, l)
        z_lat, n = _rmsnorm_matmul(h, g_attn_norm[l], w_lat, tm=512, name="attn_norm_in_proj_latents")

        def in_proj(w, kind, scale, name):
            blocks = w.shape[1] // LANES
            return _matmul_rope(n, w, cos, sin, tm=512, kinds=[kind] * blocks, scales=[scale] * blocks, name=name)

        z_sq = in_proj(w_sq, ROPE_FULL, SWA_HEAD_DIM ** -0.5, "in_proj_swa_q")
        z_skv = in_proj(w_skv, ROPE_HALF, 1.0, "in_proj_swa_kv")
        q_kinds = [ROPE_NONE, ROPE_HALF] * MLA_HEADS
        q = _norm_matmul(z_lat, 0, MLA_Q_RANK, g_q_norm[l], _prep_q_weight(w_q_up, l), cos, sin,
                         tm=1024, tn=2048, kinds=q_kinds, scales=[MLA_QK_DIM ** -0.5] * len(q_kinds), name="q_up")
        kv_kinds = [ROPE_NONE] * (w_kv_up.shape[2] // LANES)
        kv = _norm_matmul(z_lat, MLA_Q_RANK // MLA_KV_RANK, MLA_KV_RANK, g_kv_norm[l], w_kv_up[l], cos, sin,
                          tm=1024, tn=2048, kinds=kv_kinds, scales=[1.0] * len(kv_kinds), name="kv_up")
        mla_o = _mla_attention(q, kv, z_skv, batch, seq)
        swa_o = _swa_attention(z_sq, z_skv, sinks[l], batch, seq)
        w_r = jnp.concatenate([w_router_group[l], w_router_expert[l],
                               jnp.zeros((d, LANES - N_GROUPS - N_EXPERTS), F32)], axis=1)
        b_r = jnp.concatenate([b_router_group[l], b_router_expert[l],
                               jnp.zeros((LANES - N_GROUPS - N_EXPERTS,), F32)]).reshape(1, LANES)
        h = _out_proj(h, mla_o, swa_o, w_out[l])
        hn, logits = _ffn_norm_router(h, g_ffn_norm[l], w_r, b_r)
        gates, item_e, item_cnt, item_row0, dstk = _route(logits, t)
        pairs = _moe_experts(hn, w_expert_gate[l], w_expert_up[l], w_expert_down[l],
                             item_e, item_cnt, item_row0, dstk, t)
        out = _combine(h, pairs, gates, g_final_norm)
    return out.reshape(batch, seq, d)
```
